```python
import math
import jax, jax.numpy as jnp
from jax import lax
import numpy as np

D_MODEL = 1024
BATCH = 4
SEQ = 4096
DEPTH = 1

MEM_LEN = 256
MLA_HEADS = 8
MLA_Q_RANK = 384
MLA_KV_RANK = 256
MLA_NOPE = 64
MLA_ROPE = 32
MLA_V = 64
Q_BLOCK = 128
RET_HEADS = 4
RET_DK = 128
RET_DV = 128
RET_CHUNK = 128
MIX_WIDTH = MLA_HEADS * MLA_V + RET_HEADS * RET_DV
IN_WIDTHS = (MLA_Q_RANK, MLA_KV_RANK, MLA_ROPE,
             RET_HEADS * RET_DK, RET_HEADS * RET_DK, RET_HEADS * RET_DV, RET_HEADS * RET_DV)
IN_WIDTH = sum(IN_WIDTHS)
IN_SPLITS = tuple(int(s) for s in np.cumsum(IN_WIDTHS)[:-1])
X_HEADS = 4
X_HEAD_DIM = D_MODEL // X_HEADS
PEER_KEYS = 128
PEER_EXPERTS = PEER_KEYS * PEER_KEYS
PEER_HEADS = 8
PEER_TOPK = 16
PEER_DKEY = 256
PEER_BLOCK = 128
ROPE_BASE = 10000.0
EPS = 1e-6
NEG = -1e30

kernel_name = "hymba_mla_retention_peer_block"


def rmsnorm(x, g):
    xf = x.astype(jnp.float32)
    y = xf * lax.rsqrt(jnp.mean(xf * xf, axis=-1, keepdims=True) + EPS)
    return (y * g.astype(jnp.float32)).astype(x.dtype)


def rope_tables(pos, dim):
    inv = 1.0 / (ROPE_BASE ** (jnp.arange(0, dim, 2, dtype=jnp.float32) / dim))
    ang = pos[:, None] * inv[None, :]
    return jnp.cos(ang), jnp.sin(ang)


def apply_rope(x, cos, sin):
    half = x.shape[-1] // 2
    c = cos[None, :, None, :].astype(x.dtype)
    s = sin[None, :, None, :].astype(x.dtype)
    x1, x2 = x[..., :half], x[..., half:]
    return jnp.concatenate([x1 * c - x2 * s, x1 * s + x2 * c], axis=-1)


def blocked_causal_attention(q, k, v, scale):
    B, S, H, dq = q.shape
    dv = v.shape[-1]
    nb = S // Q_BLOCK
    qb = q.reshape(B, nb, Q_BLOCK, H, dq).transpose(1, 0, 2, 3, 4)
    kpos = jnp.arange(S)

    def one_block(args):
        qblk, i = args
        s = jnp.einsum('bqhd,bkhd->bhqk', qblk, k).astype(jnp.float32) * scale
        qpos = i * Q_BLOCK + jnp.arange(Q_BLOCK)
        mask = kpos[None, :] <= qpos[:, None]
        s = jnp.where(mask[None, None], s, NEG)
        p = jax.nn.softmax(s, axis=-1)
        return jnp.einsum('bhqk,bkhd->bqhd', p.astype(v.dtype), v)

    out = lax.map(one_block, (qb, jnp.arange(nb)))
    return out.transpose(1, 0, 2, 3, 4).reshape(B, S, H, dv)


def chunkwise_retention(q, k, v):
    B, S, H, dk = q.shape
    dv = v.shape[-1]
    C = RET_CHUNK
    n = S // C
    log_g = jnp.log(1.0 - 2.0 ** (-5.0 - jnp.arange(H, dtype=jnp.float32)))
    idx = jnp.arange(C, dtype=jnp.float32)
    rel = idx[:, None] - idx[None, :]
    decay = jnp.where(rel[None] >= 0, jnp.exp(jnp.maximum(rel, 0.0)[None] * log_g[:, None, None]), 0.0)
    zeta = jnp.exp((C - 1 - idx)[None, :] * log_g[:, None])
    xi = jnp.exp((idx + 1)[None, :] * log_g[:, None])
    chunk_decay = jnp.exp(C * log_g)
    qc = q.reshape(B, n, C, H, dk)
    kc = k.reshape(B, n, C, H, dk)
    vc = v.reshape(B, n, C, H, dv)
    scores = jnp.einsum('bnihd,bnjhd->bnhij', qc, kc) * decay[None, None]
    inner = jnp.einsum('bnhij,bnjhe->bnihe', scores, vc)
    kz = kc * zeta.T[None, None, :, :, None]
    kv = jnp.einsum('bnjhd,bnjhe->bnhde', kz, vc)

    def step(R, kv_i):
        return R * chunk_decay[None, :, None, None] + kv_i, R

    _, R_prev = lax.scan(step, jnp.zeros((B, H, dk, dv), jnp.float32), kv.transpose(1, 0, 2, 3, 4))
    R_prev = R_prev.transpose(1, 0, 2, 3, 4)
    cross = jnp.einsum('bnihd,bnhde->bnihe', qc, R_prev) * xi.T[None, None, :, :, None]
    return (inner + cross).reshape(B, S, H, dv)


def hybrid_mixer(a, w_in, g_q_lora, w_uq, g_kv_lora, w_ukv, g_ret_gn, w_o, cos_m, sin_m, cos_r, sin_r):
    B, S, _ = a.shape
    proj = a @ w_in
    c_q, c_kv, k_pe, r_q, r_k, r_v, r_g = jnp.split(proj, IN_SPLITS, axis=-1)
    q = (rmsnorm(c_q, g_q_lora) @ w_uq).reshape(B, S, MLA_HEADS, MLA_NOPE + MLA_ROPE)
    q = jnp.concatenate([q[..., :MLA_NOPE], apply_rope(q[..., MLA_NOPE:], cos_m, sin_m)], axis=-1)
    kv = (rmsnorm(c_kv, g_kv_lora) @ w_ukv).reshape(B, S, MLA_HEADS, MLA_NOPE + MLA_V)
    k_pe = apply_rope(k_pe[:, :, None, :], cos_m, sin_m)
    k = jnp.concatenate([kv[..., :MLA_NOPE], jnp.broadcast_to(k_pe, (B, S, MLA_HEADS, MLA_ROPE))], axis=-1)
    v = kv[..., MLA_NOPE:]
    o_mla = blocked_causal_attention(q, k, v, (MLA_NOPE + MLA_ROPE) ** -0.5).reshape(B, S, MLA_HEADS * MLA_V)
    rq = apply_rope(r_q.reshape(B, S, RET_HEADS, RET_DK), cos_r, sin_r)
    rk = apply_rope(r_k.reshape(B, S, RET_HEADS, RET_DK), cos_r, sin_r) * (RET_DK ** -0.5)
    rv = r_v.reshape(B, S, RET_HEADS, RET_DV)
    o = chunkwise_retention(rq.astype(jnp.float32), rk.astype(jnp.float32), rv.astype(jnp.float32))
    mu = jnp.mean(o, axis=-1, keepdims=True)
    var = jnp.mean(jnp.square(o - mu), axis=-1, keepdims=True)
    o = ((o - mu) * lax.rsqrt(var + EPS)).reshape(B, S, RET_HEADS * RET_DV) * g_ret_gn.astype(jnp.float32)
    o_ret = (jax.nn.silu(r_g.astype(jnp.float32)) * o).astype(a.dtype)
    return jnp.concatenate([o_mla, o_ret], axis=-1) @ w_o


def memory_cross_attention(a, mem_n, w_xq, w_xkv, w_xo):
    B, S, D = a.shape
    M = mem_n.shape[1]
    q = (a @ w_xq).reshape(B, S, X_HEADS, X_HEAD_DIM)
    kv = (mem_n @ w_xkv).reshape(B, M, 2, X_HEADS, X_HEAD_DIM)
    k, v = kv[:, :, 0], kv[:, :, 1]
    s = jnp.einsum('bshd,bmhd->bhsm', q, k).astype(jnp.float32) * (X_HEAD_DIM ** -0.5)
    p = jax.nn.softmax(s, axis=-1)
    o = jnp.einsum('bhsm,bmhd->bshd', p.astype(v.dtype), v).reshape(B, S, D)
    return o @ w_xo


def peer_ffn(a, w_pq, sub_keys, u_experts, v_experts):
    B, S, D = a.shape
    T = B * S
    t = a.reshape(T, D)
    q = (t @ w_pq).reshape(T, PEER_HEADS, 2, PEER_DKEY // 2)
    s = jnp.einsum('thpd,pkd->thpk', q, sub_keys).astype(jnp.float32)
    top_s, top_i = lax.top_k(s, PEER_TOPK)
    cand = top_s[:, :, 0, :, None] + top_s[:, :, 1, None, :]
    best_s, best_c = lax.top_k(cand.reshape(T, PEER_HEADS, PEER_TOPK * PEER_TOPK), PEER_TOPK)
    i1 = jnp.take_along_axis(top_i[:, :, 0], best_c // PEER_TOPK, axis=-1)
    i2 = jnp.take_along_axis(top_i[:, :, 1], best_c % PEER_TOPK, axis=-1)
    expert = i1 * PEER_KEYS + i2
    gate = jax.nn.softmax(best_s, axis=-1)
    nb = T // PEER_BLOCK

    def one_block(args):
        tb, eb, gb = args
        u = u_experts[eb]
        act = jax.nn.gelu(jnp.einsum('thkd,td->thk', u, tb).astype(jnp.float32), approximate=False)
        w = (gb * act).astype(tb.dtype)
        return jnp.einsum('thk,thkd->td', w, v_experts[eb])

    out = lax.map(one_block, (t.reshape(nb, PEER_BLOCK, D),
                              expert.reshape(nb, PEER_BLOCK, PEER_HEADS, PEER_TOPK),
                              gate.reshape(nb, PEER_BLOCK, PEER_HEADS, PEER_TOPK)))
    return out.reshape(B, S, D)


def setup_inputs(seed: int = 0) -> dict:
    key = jax.random.key(seed)
    ks = jax.random.split(key, 24)
    f32 = jnp.float32

    def nrm(k, shape, scale):
        return jax.random.normal(k, shape, f32) * scale

    def gain(k, shape):
        return 1.0 + 0.02 * jax.random.normal(k, shape, f32)

    L = DEPTH
    return {
        "x": nrm(ks[0], (BATCH, SEQ, D_MODEL), 1.0),
        "mem": nrm(ks[1], (BATCH, MEM_LEN, D_MODEL), 1.0),
        "g_mix": gain(ks[2], (L, D_MODEL)),
        "w_in": nrm(ks[3], (L, D_MODEL, IN_WIDTH), D_MODEL ** -0.5),
        "g_q_lora": gain(ks[4], (L, MLA_Q_RANK)),
        "w_uq": nrm(ks[5], (L, MLA_Q_RANK, MLA_HEADS * (MLA_NOPE + MLA_ROPE)), MLA_Q_RANK ** -0.5),
        "g_kv_lora": gain(ks[6], (L, MLA_KV_RANK)),
        "w_ukv": nrm(ks[7], (L, MLA_KV_RANK, MLA_HEADS * (MLA_NOPE + MLA_V)), MLA_KV_RANK ** -0.5),
        "g_ret_gn": gain(ks[8], (L, RET_HEADS * RET_DV)),
        "w_o": nrm(ks[9], (L, MIX_WIDTH, D_MODEL), MIX_WIDTH ** -0.5),
        "g_xattn": gain(ks[10], (L, D_MODEL)),
        "g_mem": gain(ks[11], (L, D_MODEL)),
        "w_xq": nrm(ks[12], (L, D_MODEL, D_MODEL), D_MODEL ** -0.5),
        "w_xkv": nrm(ks[13], (L, D_MODEL, 2 * D_MODEL), D_MODEL ** -0.5),
        "w_xo": nrm(ks[14], (L, D_MODEL, D_MODEL), D_MODEL ** -0.5),
        "g_ffn": gain(ks[15], (L, D_MODEL)),
        "w_pq": nrm(ks[16], (L, D_MODEL, PEER_HEADS * PEER_DKEY), D_MODEL ** -0.5),
        "sub_keys": nrm(ks[17], (L, 2, PEER_KEYS, PEER_DKEY // 2), (PEER_DKEY // 2) ** -0.5),
        "u_experts": nrm(ks[18], (L, PEER_EXPERTS, D_MODEL), D_MODEL ** -0.5),
        "v_experts": nrm(ks[19], (L, PEER_EXPERTS, D_MODEL), PEER_HEADS ** -0.5),
        "g_final": gain(ks[20], (D_MODEL,)),
    }


def reference(x, mem, g_mix, w_in, g_q_lora, w_uq, g_kv_lora, w_ukv, g_ret_gn, w_o,
              g_xattn, g_mem, w_xq, w_xkv, w_xo, g_ffn, w_pq, sub_keys, u_experts, v_experts, g_final):
    S = x.shape[1]
    pos = jnp.arange(S, dtype=jnp.float32)
    cos_m, sin_m = rope_tables(pos, MLA_ROPE)
    cos_r, sin_r = rope_tables(pos, RET_DK)
    h = x
    for l in range(DEPTH):
        a = rmsnorm(h, g_mix[l])
        h = h + hybrid_mixer(a, w_in[l], g_q_lora[l], w_uq[l], g_kv_lora[l], w_ukv[l], g_ret_gn[l], w_o[l],
                             cos_m, sin_m, cos_r, sin_r)
        a = rmsnorm(h, g_xattn[l])
        h = h + memory_cross_attention(a, rmsnorm(mem, g_mem[l]), w_xq[l], w_xkv[l], w_xo[l])
        a = rmsnorm(h, g_ffn[l])
        h = h + peer_ffn(a, w_pq[l], sub_keys[l], u_experts[l], v_experts[l])
    return rmsnorm(h, g_final)
```

```python
import functools
import math

import jax
import jax.numpy as jnp
from jax import lax
from jax.experimental import pallas as pl
from jax.experimental.pallas import tpu as pltpu

F32 = jnp.float32
BF16 = jnp.bfloat16

LANES = 128
EPS = 1e-6
NEG = -1e30
ROPE_BASE = 10000.0
VMEM_LIMIT = 56 * 1024 * 1024

MLA_HEADS, MLA_Q_RANK, MLA_KV_RANK = 8, 384, 256
MLA_NOPE, MLA_ROPE, MLA_V = 64, 32, 64
RET_HEADS, RET_DK, RET_DV, RET_CHUNK = 4, 128, 128, 128
X_HEADS = 4
PEER_KEYS, PEER_HEADS, PEER_TOPK, PEER_DKEY = 128, 8, 16, 256


def _params(sem, vmem=VMEM_LIMIT):
    return pltpu.CompilerParams(dimension_semantics=sem, vmem_limit_bytes=vmem)


def _rms(x, g):
    return x * lax.rsqrt(jnp.mean(x * x, axis=-1, keepdims=True) + EPS) * g


def _dot(a, b):
    return jnp.dot(a, b, preferred_element_type=F32)


def _dot_nt(a, b):
    return lax.dot_general(a, b, (((1,), (1,)), ((), ())), preferred_element_type=F32)


def _gelu(x):
    return 0.5 * x * (1.0 + lax.erf(x * (2.0 ** -0.5)))


def _const_spec(shape):
    nd = len(shape)
    return pl.BlockSpec(shape, lambda *_: (0,) * nd)


def _in_proj_kernel(x_ref, gmix_ref, win_ref, gq_ref, wuq_ref, gkv_ref, wk_ref, wv_ref,
                    cosm_ref, sinm_ref, cosr_ref, sinr_ref,
                    q_ref, k_ref, v_ref, rq_ref, rk_ref, rv_ref, rg_ref):
    x = x_ref[...]
    a = _rms(x, gmix_ref[...]).astype(BF16)
    proj = _dot(a, win_ref[...])
    o_cq, o_ckv, o_pe, o_rq = 0, MLA_Q_RANK, MLA_Q_RANK + MLA_KV_RANK, MLA_Q_RANK + MLA_KV_RANK + LANES
    rw = RET_HEADS * RET_DK
    cq = proj[:, o_cq:o_ckv]
    ckv = proj[:, o_ckv:o_pe]
    kpe = proj[:, o_pe:o_rq]
    cqn = _rms(cq, gq_ref[...]).astype(BF16)
    ckvn = _rms(ckv, gkv_ref[...]).astype(BF16)
    q = _dot(cqn, wuq_ref[...])
    kn = _dot(ckvn, wk_ref[...])
    v_ref[...] = _dot(ckvn, wv_ref[...]).astype(BF16)

    cosm, sinm = cosm_ref[...], sinm_ref[...]
    lane = lax.broadcasted_iota(jnp.int32, (x.shape[0], LANES), 1)
    first_half = lane < MLA_NOPE + MLA_ROPE // 2

    def rope_m(c):
        rot = jnp.where(first_half, pltpu.roll(c, LANES - MLA_ROPE // 2, axis=1),
                        pltpu.roll(c, MLA_ROPE // 2, axis=1))
        return c * cosm + rot * sinm

    qscale = (MLA_NOPE + MLA_ROPE) ** -0.5
    kpe_r = rope_m(kpe)
    for h in range(MLA_HEADS):
        sl = slice(h * LANES, (h + 1) * LANES)
        q_ref[:, sl] = (rope_m(q[:, sl]) * qscale).astype(BF16)
        k_ref[:, sl] = (kn[:, sl] + kpe_r).astype(BF16)

    cosr, sinr = cosr_ref[...], sinr_ref[...]
    kscale = RET_DK ** -0.5
    for h in range(RET_HEADS):
        sl = slice(h * LANES, (h + 1) * LANES)
        c = proj[:, o_rq + h * LANES:o_rq + (h + 1) * LANES]
        rq_ref[:, sl] = (c * cosr + pltpu.roll(c, RET_DK // 2, axis=1) * sinr).astype(BF16)
        c = proj[:, o_rq + rw + h * LANES:o_rq + rw + (h + 1) * LANES]
        rk_ref[:, sl] = ((c * cosr + pltpu.roll(c, RET_DK // 2, axis=1) * sinr) * kscale).astype(BF16)
    rv_ref[...] = proj[:, o_rq + 2 * rw:o_rq + 3 * rw].astype(BF16)
    rg_ref[...] = proj[:, o_rq + 3 * rw:o_rq + 4 * rw]


def _in_proj(x2, g_mix, w_in_ext, g_q, w_uq_pad, g_kv, w_k, w_v, cosm, sinm, cosr, sinr, seq, tm):
    T, D = x2.shape
    nS = seq // tm
    tok = lambda w: pl.BlockSpec((tm, w), lambda i: (i, 0))
    pos = lambda w: pl.BlockSpec((tm, w), lambda i: (i % nS, 0))
    outs = [(T, MLA_HEADS * LANES, BF16), (T, MLA_HEADS * LANES, BF16), (T, MLA_HEADS * MLA_V, BF16),
            (T, 512, BF16), (T, 512, BF16), (T, 512, BF16), (T, 512, F32)]
    return pl.pallas_call(
        _in_proj_kernel,
        grid=(T // tm,),
        in_specs=[tok(D), _const_spec(g_mix.shape), _const_spec(w_in_ext.shape), _const_spec(g_q.shape),
                  _const_spec(w_uq_pad.shape), _const_spec(g_kv.shape), _const_spec(w_k.shape),
                  _const_spec(w_v.shape), pos(LANES), pos(LANES), pos(LANES), pos(LANES)],
        out_specs=[tok(w) for (_, w, _) in outs],
        out_shape=[jax.ShapeDtypeStruct((t, w), d) for (t, w, d) in outs],
        compiler_params=_params(("parallel",)),
        name="in_proj",
    )(x2, g_mix, w_in_ext, g_q, w_uq_pad, g_kv, w_k, w_v, cosm, sinm, cosr, sinr)


def _mla_attn_kernel(q_ref, k_ref, v_ref, o_ref, *, tq, tk):
    i = pl.program_id(2)
    row = lax.broadcasted_iota(jnp.int32, (tq, tk), 0)
    col = lax.broadcasted_iota(jnp.int32, (tq, tk), 1)
    outs = []
    for hh in range(2):
        qh = q_ref[0, :, hh * LANES:(hh + 1) * LANES]

        def step(j, carry, masked):
            m, l, acc = carry
            k0 = pl.multiple_of(j * tk, tk)
            kj = k_ref[0, pl.ds(k0, tk), hh * LANES:(hh + 1) * LANES]
            vj = v_ref[0, pl.ds(k0, tk), hh * MLA_V:(hh + 1) * MLA_V]
            s = _dot_nt(qh, kj)
            if masked:
                s = jnp.where(col + j * tk <= row + i * tq, s, NEG)
            m_new = jnp.maximum(m, jnp.max(s, axis=-1, keepdims=True))
            p = jnp.exp(s - m_new)
            alpha = jnp.exp(m - m_new)
            l = alpha * l + jnp.sum(p, axis=-1, keepdims=True)
            acc = alpha * acc + _dot(p.astype(BF16), vj)
            return m_new, l, acc

        init = (jnp.full((tq, 1), NEG, F32), jnp.zeros((tq, 1), F32), jnp.zeros((tq, MLA_V), F32))
        nfull = (i * tq) // tk
        carry = lax.fori_loop(0, nfull, functools.partial(step, masked=False), init)
        for d in range(tq // tk):
            carry = step(nfull + d, carry, True)
        m, l, acc = carry
        outs.append(acc / l)
    o_ref[0] = jnp.concatenate(outs, axis=-1).astype(o_ref.dtype)


def _mla_attn(q, k, v, tq, tk):
    B, S, _ = q.shape
    return pl.pallas_call(
        functools.partial(_mla_attn_kernel, tq=tq, tk=tk),
        grid=(B, MLA_HEADS // 2, S // tq),
        in_specs=[pl.BlockSpec((1, tq, 2 * LANES), lambda b, h, i: (b, i, h)),
                  pl.BlockSpec((1, S, 2 * LANES), lambda b, h, i: (b, 0, h)),
                  pl.BlockSpec((1, S, 2 * MLA_V), lambda b, h, i: (b, 0, h))],
        out_specs=pl.BlockSpec((1, tq, 2 * MLA_V), lambda b, h, i: (b, i, h)),
        out_shape=jax.ShapeDtypeStruct((B, S, MLA_HEADS * MLA_V), BF16),
        compiler_params=_params(("parallel", "parallel", "arbitrary")),
        name="mla_attn",
    )(q, k, v)


def _retention_kernel(rq_ref, rk_ref, rv_ref, rg_ref, gn_ref, decay_ref, zeta_ref, xi_ref, cd_ref,
                      o_ref, *, n_chunks):
    C = RET_CHUNK
    decay = decay_ref[0]
    zeta = zeta_ref[0]
    xi = xi_ref[0]
    cd = cd_ref[0]
    gn = gn_ref[...]

    def chunk(n, R):
        r0 = pl.multiple_of(n * C, C)
        qc = rq_ref[0, pl.ds(r0, C), :]
        kc = rk_ref[0, pl.ds(r0, C), :]
        vc = rv_ref[0, pl.ds(r0, C), :]
        sc = _dot_nt(qc, kc) * decay
        inner = _dot(sc.astype(BF16), vc)
        cross = _dot(qc, R.astype(BF16)) * xi
        o = inner + cross
        kz = (kc.astype(F32) * zeta).T.astype(BF16)
        kv = _dot(kz, vc)
        mu = jnp.mean(o, axis=-1, keepdims=True)
        d = o - mu
        var = jnp.mean(d * d, axis=-1, keepdims=True)
        on = d * lax.rsqrt(var + EPS) * gn
        g = rg_ref[0, pl.ds(r0, C), :]
        o_ref[0, pl.ds(r0, C), :] = (g * jax.nn.sigmoid(g) * on).astype(o_ref.dtype)
        return R * cd + kv

    lax.fori_loop(0, n_chunks, chunk, jnp.zeros((RET_DK, RET_DV), F32))


def _retention(rq, rk, rv, rg, g_gn, decay, zeta, xi, cd):
    B, S, _ = rq.shape
    tokh = pl.BlockSpec((1, S, LANES), lambda b, h: (b, 0, h))
    tab = pl.BlockSpec((1, RET_CHUNK, LANES), lambda b, h: (h, 0, 0))
    return pl.pallas_call(
        functools.partial(_retention_kernel, n_chunks=S // RET_CHUNK),
        grid=(B, RET_HEADS),
        in_specs=[tokh, tokh, tokh, tokh, pl.BlockSpec((1, LANES), lambda b, h: (0, h)),
                  tab, tab, tab, tab],
        out_specs=tokh,
        out_shape=jax.ShapeDtypeStruct((B, S, RET_HEADS * RET_DV), BF16),
        compiler_params=_params(("parallel", "parallel")),
        name="retention",
    )(rq, rk, rv, rg, g_gn, decay, zeta, xi, cd)


def _mem_kv_kernel(mem_ref, g_ref, w_ref, k_ref, v_ref):
    D = mem_ref.shape[-1]
    mn = _rms(mem_ref[0], g_ref[...]).astype(BF16)
    kv = _dot(mn, w_ref[...])
    k_ref[0] = kv[:, :D].astype(BF16)
    v_ref[0] = kv[:, D:].astype(BF16)


def _mem_kv(mem, g_mem, w_xkv):
    B, M, D = mem.shape
    blk = pl.BlockSpec((1, M, D), lambda b: (b, 0, 0))
    return pl.pallas_call(
        _mem_kv_kernel,
        grid=(B,),
        in_specs=[blk, _const_spec(g_mem.shape), _const_spec(w_xkv.shape)],
        out_specs=[blk, blk],
        out_shape=[jax.ShapeDtypeStruct((B, M, D), BF16)] * 2,
        compiler_params=_params(("parallel",)),
        name="mem_kv",
    )(mem, g_mem, w_xkv)


def _post_mix_kernel(x_ref, omla_ref, oret_ref, woa_ref, wob_ref, gx_ref, wxq_ref, kx_ref, vx_ref, wxo_ref,
                     gf_ref, wpq_ref, keys_ref, h2_ref, a3t_ref, st_ref, ox_ref):
    D = x_ref.shape[-1]
    h1 = x_ref[...] + _dot(omla_ref[...], woa_ref[...]) + _dot(oret_ref[...], wob_ref[...])
    a2 = _rms(h1, gx_ref[...]).astype(BF16)
    qx = _dot(a2, wxq_ref[...])
    hd = D // X_HEADS
    for h in range(X_HEADS):
        sl = slice(h * hd, (h + 1) * hd)
        s = _dot_nt(qx[:, sl].astype(BF16), kx_ref[0, :, sl]) * (hd ** -0.5)
        m = jnp.max(s, axis=-1, keepdims=True)
        p = jnp.exp(s - m)
        p = p / jnp.sum(p, axis=-1, keepdims=True)
        ox_ref[:, sl] = _dot(p.astype(BF16), vx_ref[0, :, sl]).astype(BF16)
    h2 = h1 + _dot(ox_ref[...], wxo_ref[...])
    h2_ref[...] = h2
    a3 = _rms(h2, gf_ref[...])
    a3t_ref[...] = a3.T.astype(BF16)
    qp = _dot(a3.astype(BF16), wpq_ref[...]).astype(BF16)
    half = PEER_DKEY // 2
    for hp in range(2 * PEER_HEADS):
        st_ref[hp] = _dot_nt(keys_ref[hp % 2], qp[:, hp * half:(hp + 1) * half])


def _post_mix(x2, o_mla, o_ret, w_oa, w_ob, g_x, w_xq, kx, vx, w_xo, g_f, w_pq, keys, seq, tm):
    T, D = x2.shape
    nS = seq // tm
    tok = lambda w: pl.BlockSpec((tm, w), lambda i: (i, 0))
    memb = pl.BlockSpec((1,) + kx.shape[1:], lambda i: (i // nS, 0, 0))
    return pl.pallas_call(
        _post_mix_kernel,
        grid=(T // tm,),
        in_specs=[tok(D), tok(o_mla.shape[1]), tok(o_ret.shape[1]), _const_spec(w_oa.shape),
                  _const_spec(w_ob.shape), _const_spec(g_x.shape), _const_spec(w_xq.shape), memb, memb,
                  _const_spec(w_xo.shape), _const_spec(g_f.shape), _const_spec(w_pq.shape),
                  _const_spec(keys.shape)],
        out_specs=[tok(D), pl.BlockSpec((D, tm), lambda i: (0, i)),
                   pl.BlockSpec((2 * PEER_HEADS, PEER_KEYS, tm), lambda i: (0, 0, i))],
        out_shape=[jax.ShapeDtypeStruct((T, D), F32), jax.ShapeDtypeStruct((D, T), BF16),
                   jax.ShapeDtypeStruct((2 * PEER_HEADS, PEER_KEYS, T), F32)],
        scratch_shapes=[pltpu.VMEM((tm, D), BF16)],
        compiler_params=_params(("parallel",)),
        name="post_mix",
    )(x2, o_mla, o_ret, w_oa, w_ob, g_x, w_xq, kx, vx, w_xo, g_f, w_pq, keys)


N_TOP = PEER_TOPK + 1
TOP_ROWS = 24


def _peer_route_kernel(st_ref, stats_ref, vs_ref):
    tl = st_ref.shape[-1]
    vs_ref[...] = jnp.full(vs_ref.shape, NEG, F32)
    for h in range(PEER_HEADS):
        for p in range(2):
            cur = st_ref[2 * h + p]
            for it in range(N_TOP):
                m = jnp.max(cur, axis=0, keepdims=True)
                vs_ref[p, it:it + 1, :] = m
                if it + 1 < N_TOP:
                    cur = jnp.where(cur == m, NEG, cur)
        v1 = lambda i: vs_ref[0, i:i + 1, :]
        v2 = lambda i: vs_ref[1, i:i + 1, :]
        cands = [v1(0) + vs_ref[1, 8 * r:8 * r + 8, :] for r in range(TOP_ROWS // 8)]
        cands += [v1(i) + vs_ref[1, 0:8, :] for i in range(1, 8)]
        cands += [vs_ref[0, 8 * r:8 * r + 8, :] + v2(0) for r in range(1, TOP_ROWS // 8)]
        c = jnp.concatenate(cands, axis=0)
        top = v1(0) + v2(0)
        z = jnp.zeros((1, tl), F32)
        c16 = top
        c17 = top
        for it in range(N_TOP):
            m = jnp.max(c, axis=0, keepdims=True)
            if it < PEER_TOPK:
                z = z + jnp.exp(m - top)
                c16 = m
                c = jnp.where(c == m, NEG, c)
            else:
                c17 = m
        stats_ref[0, h:h + 1, :] = 0.5 * (c16 + c17)
        stats_ref[1, h:h + 1, :] = v1(0)
        stats_ref[2, h:h + 1, :] = v2(0)
        stats_ref[3, h:h + 1, :] = 1.0 / z


def _peer_route(st, tl):
    HP, K, T = st.shape
    return pl.pallas_call(
        _peer_route_kernel,
        grid=(T // tl,),
        in_specs=[pl.BlockSpec((HP, K, tl), lambda i: (0, 0, i))],
        out_specs=pl.BlockSpec((4, PEER_HEADS, tl), lambda i: (0, 0, i)),
        out_shape=jax.ShapeDtypeStruct((4, PEER_HEADS, T), F32),
        scratch_shapes=[pltpu.VMEM((2, TOP_ROWS, tl), F32)],
        compiler_params=_params(("parallel",)),
        name="peer_route",
    )(st)


def _peer_dense_kernel(a3t_ref, st_ref, stats_ref, u_ref, vt_ref, out_ref,
                       thr_ref, c1_ref, e2_ref, act_ref, p_ref, *, te, tm):
    j = pl.program_id(1)
    nk = PEER_KEYS

    @pl.when(j == 0)
    def _():
        out_ref[...] = jnp.zeros(out_ref.shape, F32)
        for h in range(PEER_HEADS):
            s1 = st_ref[2 * h]
            thr_ref[h] = stats_ref[0, h:h + 1, :] - s1
            c1_ref[h] = jnp.exp(s1 - stats_ref[1, h:h + 1, :]) * stats_ref[3, h:h + 1, :]
            e2_ref[h] = jnp.exp(st_ref[2 * h + 1] - stats_ref[2, h:h + 1, :])

    act_ref[...] = _dot(u_ref[...], a3t_ref[...])

    a0 = pl.multiple_of(j * (te // nk), te // nk)
    for tl in range(tm // LANES):
        ls = slice(tl * LANES, (tl + 1) * LANES)
        thr = [thr_ref[h, pl.ds(a0, te // nk), ls] for h in range(PEER_HEADS)]
        c1 = [c1_ref[h, pl.ds(a0, te // nk), ls] for h in range(PEER_HEADS)]
        for al in range(te // nk):
            w = jnp.zeros((nk, LANES), F32)
            for h in range(PEER_HEADS):
                sel = st_ref[2 * h + 1, :, ls] >= thr[h][al:al + 1, :]
                w = w + jnp.where(sel, e2_ref[h, :, ls] * c1[h][al:al + 1, :], 0.0)
            rs = slice(al * nk, (al + 1) * nk)
            p_ref[rs, ls] = (_gelu(act_ref[rs, ls]) * w).astype(BF16)
    out_ref[...] += _dot(vt_ref[...], p_ref[...])


def _peer_dense(a3t, st, stats, u_bf, vt_bf, tm, te):
    D, T = a3t.shape
    E = u_bf.shape[0]
    return pl.pallas_call(
        functools.partial(_peer_dense_kernel, te=te, tm=tm),
        grid=(T // tm, E // te),
        in_specs=[pl.BlockSpec((D, tm), lambda i, j: (0, i)),
                  pl.BlockSpec(st.shape[:2] + (tm,), lambda i, j: (0, 0, i)),
                  pl.BlockSpec(stats.shape[:2] + (tm,), lambda i, j: (0, 0, i)),
                  pl.BlockSpec((te, D), lambda i, j: (j, 0)),
                  pl.BlockSpec((D, te), lambda i, j: (0, j))],
        out_specs=pl.BlockSpec((D, tm), lambda i, j: (0, i)),
        out_shape=jax.ShapeDtypeStruct((D, T), F32),
        scratch_shapes=[pltpu.VMEM((PEER_HEADS, PEER_KEYS, tm), F32)] * 3
        + [pltpu.VMEM((te, tm), F32), pltpu.VMEM((te, tm), BF16)],
        compiler_params=_params(("parallel", "arbitrary")),
        name="peer_dense",
    )(a3t, st, stats, u_bf, vt_bf)


def _final_kernel(h2_ref, pt_ref, g_ref, o_ref):
    o_ref[...] = _rms(h2_ref[...] + pt_ref[...].T, g_ref[...])


def _final(h2, peer_t, g, tm):
    T, D = h2.shape
    return pl.pallas_call(
        _final_kernel,
        grid=(T // tm,),
        in_specs=[pl.BlockSpec((tm, D), lambda i: (i, 0)), pl.BlockSpec((D, tm), lambda i: (0, i)),
                  _const_spec(g.shape)],
        out_specs=pl.BlockSpec((tm, D), lambda i: (i, 0)),
        out_shape=jax.ShapeDtypeStruct((T, D), F32),
        compiler_params=_params(("parallel",)),
        name="final",
    )(h2, peer_t, g)


def _rope_tables(seq):
    pos = jnp.arange(seq, dtype=F32)

    def tab(dim):
        inv = 1.0 / (ROPE_BASE ** (jnp.arange(0, dim, 2, dtype=F32) / dim))
        ang = pos[:, None] * inv[None, :]
        return jnp.cos(ang), jnp.sin(ang)

    cm, sm = tab(MLA_ROPE)
    pad = LANES - MLA_NOPE - MLA_ROPE
    cosm = jnp.concatenate([jnp.ones((seq, MLA_NOPE), F32), cm, cm, jnp.zeros((seq, pad), F32)], axis=1)
    sinm = jnp.concatenate([jnp.zeros((seq, MLA_NOPE), F32), -sm, sm, jnp.zeros((seq, pad), F32)], axis=1)
    cr, sr = tab(RET_DK)
    cosr = jnp.concatenate([cr, cr], axis=1)
    sinr = jnp.concatenate([-sr, sr], axis=1)
    return cosm, sinm, cosr, sinr


def _retention_tables():
    C = RET_CHUNK
    log_g = jnp.log(1.0 - 2.0 ** (-5.0 - jnp.arange(RET_HEADS, dtype=F32)))
    idx = jnp.arange(C, dtype=F32)
    rel = idx[:, None] - idx[None, :]
    decay = jnp.where(rel[None] >= 0, jnp.exp(jnp.maximum(rel, 0.0)[None] * log_g[:, None, None]), 0.0)
    zeta = jnp.exp((C - 1 - idx)[None, :] * log_g[:, None])
    xi = jnp.exp((idx + 1)[None, :] * log_g[:, None])
    cd = jnp.exp(C * log_g)
    bc = lambda t: jnp.broadcast_to(t[:, :, None], (RET_HEADS, C, LANES))
    return decay, bc(zeta), bc(xi), jnp.broadcast_to(cd[:, None, None], (RET_HEADS, C, LANES))


def _layer(h, mem, g_mix, w_in, g_q_lora, w_uq, g_kv_lora, w_ukv, g_ret_gn, w_o, g_xattn, g_mem,
           w_xq, w_xkv, w_xo, g_ffn, w_pq, sub_keys, u_experts, v_experts, g_out):
    B, S, D = h.shape
    T = B * S
    row = lambda g: g.reshape(1, -1)

    o_pe = MLA_Q_RANK + MLA_KV_RANK
    pe_pad = jnp.zeros((D, LANES), F32).at[:, MLA_NOPE:MLA_NOPE + MLA_ROPE].set(w_in[:, o_pe:o_pe + MLA_ROPE])
    w_in_ext = jnp.concatenate([w_in[:, :o_pe], pe_pad, w_in[:, o_pe + MLA_ROPE:]], axis=1).astype(BF16)
    dq = MLA_NOPE + MLA_ROPE
    w_uq_pad = jnp.pad(w_uq.reshape(MLA_Q_RANK, MLA_HEADS, dq), ((0, 0), (0, 0), (0, LANES - dq)))
    w_uq_pad = w_uq_pad.reshape(MLA_Q_RANK, MLA_HEADS * LANES).astype(BF16)
    w_ukv3 = w_ukv.reshape(MLA_KV_RANK, MLA_HEADS, MLA_NOPE + MLA_V)
    w_k = jnp.pad(w_ukv3[:, :, :MLA_NOPE], ((0, 0), (0, 0), (0, LANES - MLA_NOPE)))
    w_k = w_k.reshape(MLA_KV_RANK, MLA_HEADS * LANES).astype(BF16)
    w_v = w_ukv3[:, :, MLA_NOPE:].reshape(MLA_KV_RANK, MLA_HEADS * MLA_V).astype(BF16)

    cosm, sinm, cosr, sinr = _rope_tables(S)
    q, k, v, rq, rk, rv, rg = _in_proj(h.reshape(T, D), row(g_mix), w_in_ext, row(g_q_lora), w_uq_pad,
                                       row(g_kv_lora), w_k, w_v, cosm, sinm, cosr, sinr, S, min(256, S))
    r3 = lambda t: t.reshape(B, S, t.shape[-1])
    o_mla = _mla_attn(r3(q), r3(k), r3(v), min(256, S), min(256, S))
    o_ret = _retention(r3(rq), r3(rk), r3(rv), r3(rg), row(g_ret_gn), *_retention_tables())

    kx, vx = _mem_kv(mem, row(g_mem), w_xkv.astype(BF16))
    n_mla = MLA_HEADS * MLA_V
    w_o_bf = w_o.astype(BF16)
    h2, a3t, st = _post_mix(h.reshape(T, D), o_mla.reshape(T, -1), o_ret.reshape(T, -1), w_o_bf[:n_mla],
                            w_o_bf[n_mla:], row(g_xattn), w_xq.astype(BF16), kx, vx, w_xo.astype(BF16),
                            row(g_ffn), w_pq.astype(BF16), sub_keys.astype(BF16), S, min(256, S))
    stats = _peer_route(st, LANES)
    peer_t = _peer_dense(a3t, st, stats, u_experts.astype(BF16), v_experts.T.astype(BF16),
                         min(512, T), 1024)
    return _final(h2, peer_t, row(g_out), min(512, T)).reshape(B, S, D)


def kernel(x, mem, g_mix, w_in, g_q_lora, w_uq, g_kv_lora, w_ukv, g_ret_gn, w_o, g_xattn, g_mem, w_xq, w_xkv,
           w_xo, g_ffn, w_pq, sub_keys, u_experts, v_experts, g_final):
    depth = g_mix.shape[0]
    assert depth == 1, "the final norm is fused into the single layer's last kernel"
    l = 0
    return _layer(x, mem, g_mix[l], w_in[l], g_q_lora[l], w_uq[l], g_kv_lora[l], w_ukv[l], g_ret_gn[l], w_o[l],
                  g_xattn[l], g_mem[l], w_xq[l], w_xkv[l], w_xo[l], g_ffn[l], w_pq[l], sub_keys[l],
                  u_experts[l], v_experts[l], g_final)
```

```python
import functools
import math

import jax
import jax.numpy as jnp
from jax import lax
from jax.experimental import pallas as pl
from jax.experimental.pallas import tpu as pltpu

F32 = jnp.float32
BF16 = jnp.bfloat16

LANES = 128
EPS = 1e-6
NEG = -1e30
ROPE_BASE = 10000.0
VMEM_LIMIT = 56 * 1024 * 1024

MLA_HEADS, MLA_Q_RANK, MLA_KV_RANK = 8, 384, 256
MLA_NOPE, MLA_ROPE, MLA_V = 64, 32, 64
RET_HEADS, RET_DK, RET_DV, RET_CHUNK = 4, 128, 128, 128
X_HEADS = 4
PEER_KEYS, PEER_HEADS, PEER_TOPK, PEER_DKEY = 128, 8, 16, 256


def _params(sem, vmem=VMEM_LIMIT):
    return pltpu.CompilerParams(dimension_semantics=sem, vmem_limit_bytes=vmem)


def _rms(x, g):
    return x * lax.rsqrt(jnp.mean(x * x, axis=-1, keepdims=True) + EPS) * g


def _dot(a, b):
    return jnp.dot(a, b, preferred_element_type=F32)


def _dot_nt(a, b):
    return lax.dot_general(a, b, (((1,), (1,)), ((), ())), preferred_element_type=F32)


def _gelu(x):
    return 0.5 * x * (1.0 + lax.erf(x * (2.0 ** -0.5)))


def _const_spec(shape):
    nd = len(shape)
    return pl.BlockSpec(shape, lambda *_: (0,) * nd)


def _in_proj_kernel(x_ref, gmix_ref, win_ref, gq_ref, wuq_ref, gkv_ref, wk_ref, wv_ref,
                    cosm_ref, sinm_ref, cosr_ref, sinr_ref,
                    q_ref, k_ref, v_ref, rq_ref, rk_ref, rv_ref, rg_ref):
    x = x_ref[...]
    a = _rms(x, gmix_ref[...]).astype(BF16)
    proj = _dot(a, win_ref[...])
    o_cq, o_ckv, o_pe, o_rq = 0, MLA_Q_RANK, MLA_Q_RANK + MLA_KV_RANK, MLA_Q_RANK + MLA_KV_RANK + LANES
    rw = RET_HEADS * RET_DK
    cq = proj[:, o_cq:o_ckv]
    ckv = proj[:, o_ckv:o_pe]
    kpe = proj[:, o_pe:o_rq]
    cqn = _rms(cq, gq_ref[...]).astype(BF16)
    ckvn = _rms(ckv, gkv_ref[...]).astype(BF16)
    q = _dot(cqn, wuq_ref[...])
    kn = _dot(ckvn, wk_ref[...])
    vlane = lax.broadcasted_iota(jnp.int32, (x.shape[0], MLA_HEADS * LANES), 1) % LANES
    v_ref[...] = jnp.where(vlane == MLA_V, 1.0, _dot(ckvn, wv_ref[...])).astype(BF16)

    cosm, sinm = cosm_ref[...], sinm_ref[...]
    lane = lax.broadcasted_iota(jnp.int32, (x.shape[0], LANES), 1)
    first_half = lane < MLA_NOPE + MLA_ROPE // 2

    def rope_m(c):
        rot = jnp.where(first_half, pltpu.roll(c, LANES - MLA_ROPE // 2, axis=1),
                        pltpu.roll(c, MLA_ROPE // 2, axis=1))
        return c * cosm + rot * sinm

    qscale = (MLA_NOPE + MLA_ROPE) ** -0.5
    kpe_r = rope_m(kpe)
    for h in range(MLA_HEADS):
        sl = slice(h * LANES, (h + 1) * LANES)
        q_ref[:, sl] = (rope_m(q[:, sl]) * qscale).astype(BF16)
        k_ref[:, sl] = (kn[:, sl] + kpe_r).astype(BF16)

    cosr, sinr = cosr_ref[...], sinr_ref[...]
    kscale = RET_DK ** -0.5
    for h in range(RET_HEADS):
        sl = slice(h * LANES, (h + 1) * LANES)
        c = proj[:, o_rq + h * LANES:o_rq + (h + 1) * LANES]
        rq_ref[:, sl] = (c * cosr + pltpu.roll(c, RET_DK // 2, axis=1) * sinr).astype(BF16)
        c = proj[:, o_rq + rw + h * LANES:o_rq + rw + (h + 1) * LANES]
        rk_ref[:, sl] = ((c * cosr + pltpu.roll(c, RET_DK // 2, axis=1) * sinr) * kscale).astype(BF16)
    rv_ref[...] = proj[:, o_rq + 2 * rw:o_rq + 3 * rw].astype(BF16)
    rg_ref[...] = proj[:, o_rq + 3 * rw:o_rq + 4 * rw]


def _in_proj(x2, g_mix, w_in_ext, g_q, w_uq_pad, g_kv, w_k, w_v, cosm, sinm, cosr, sinr, seq, tm):
    T, D = x2.shape
    nS = seq // tm
    tok = lambda w: pl.BlockSpec((tm, w), lambda i: (i, 0))
    pos = lambda w: pl.BlockSpec((tm, w), lambda i: (i % nS, 0))
    outs = [(T, MLA_HEADS * LANES, BF16), (T, MLA_HEADS * LANES, BF16), (T, MLA_HEADS * LANES, BF16),
            (T, 512, BF16), (T, 512, BF16), (T, 512, BF16), (T, 512, F32)]
    return pl.pallas_call(
        _in_proj_kernel,
        grid=(T // tm,),
        in_specs=[tok(D), _const_spec(g_mix.shape), _const_spec(w_in_ext.shape), _const_spec(g_q.shape),
                  _const_spec(w_uq_pad.shape), _const_spec(g_kv.shape), _const_spec(w_k.shape),
                  _const_spec(w_v.shape), pos(LANES), pos(LANES), pos(LANES), pos(LANES)],
        out_specs=[tok(w) for (_, w, _) in outs],
        out_shape=[jax.ShapeDtypeStruct((t, w), d) for (t, w, d) in outs],
        compiler_params=_params(("parallel",)),
        name="in_proj",
    )(x2, g_mix, w_in_ext, g_q, w_uq_pad, g_kv, w_k, w_v, cosm, sinm, cosr, sinr)


def _mla_attn_kernel(q_ref, k_ref, v_ref, o_ref, *, t):
    i = pl.program_id(2)
    causal = (lax.broadcasted_iota(jnp.int32, (t, t), 1) <= lax.broadcasted_iota(jnp.int32, (t, t), 0))

    def step(j, carry, masked):
        k0 = pl.multiple_of(j * t, t)
        new = []
        for hh in range(2):
            m, acc = carry[hh]
            hs = slice(hh * LANES, (hh + 1) * LANES)
            s = _dot_nt(q_ref[0, :, hs], k_ref[0, pl.ds(k0, t), hs])
            if masked:
                s = jnp.where(causal, s, NEG)
            m_new = jnp.maximum(m, jnp.max(s, axis=-1, keepdims=True))
            p = jnp.exp(s - m_new).astype(BF16)
            acc = jnp.exp(m - m_new) * acc + _dot(p, v_ref[0, pl.ds(k0, t), hs])
            new.append((m_new, acc))
        return tuple(new)

    init = tuple((jnp.full((t, 1), NEG, F32), jnp.zeros((t, LANES), F32)) for _ in range(2))
    carry = lax.fori_loop(0, i, functools.partial(step, masked=False), init)
    carry = step(i, carry, True)
    outs = [acc[:, :MLA_V] / acc[:, MLA_V:MLA_V + 1] for (_, acc) in carry]
    o_ref[0] = jnp.concatenate(outs, axis=-1).astype(o_ref.dtype)


def _mla_attn(q, k, v, t):
    B, S, _ = q.shape
    return pl.pallas_call(
        functools.partial(_mla_attn_kernel, t=t),
        grid=(B, MLA_HEADS // 2, S // t),
        in_specs=[pl.BlockSpec((1, t, 2 * LANES), lambda b, h, i: (b, i, h)),
                  pl.BlockSpec((1, S, 2 * LANES), lambda b, h, i: (b, 0, h)),
                  pl.BlockSpec((1, S, 2 * LANES), lambda b, h, i: (b, 0, h))],
        out_specs=pl.BlockSpec((1, t, 2 * MLA_V), lambda b, h, i: (b, i, h)),
        out_shape=jax.ShapeDtypeStruct((B, S, MLA_HEADS * MLA_V), BF16),
        compiler_params=_params(("parallel", "parallel", "arbitrary")),
        name="mla_attn",
    )(q, k, v)


def _retention_kernel(rq_ref, rk_ref, rv_ref, rg_ref, gn_ref, decay_ref, zeta_ref, xi_ref, cd_ref,
                      o_ref, *, n_chunks):
    C = RET_CHUNK
    decay = decay_ref[0]
    zeta = zeta_ref[0]
    xi = xi_ref[0]
    cd = cd_ref[0]
    gn = gn_ref[...]

    def chunk(n, R):
        r0 = pl.multiple_of(n * C, C)
        qc = rq_ref[0, pl.ds(r0, C), :]
        kc = rk_ref[0, pl.ds(r0, C), :]
        vc = rv_ref[0, pl.ds(r0, C), :]
        sc = _dot_nt(qc, kc) * decay
        inner = _dot(sc.astype(BF16), vc)
        cross = _dot(qc, R.astype(BF16)) * xi
        o = inner + cross
        kz = (kc.astype(F32) * zeta).T.astype(BF16)
        kv = _dot(kz, vc)
        mu = jnp.mean(o, axis=-1, keepdims=True)
        d = o - mu
        var = jnp.mean(d * d, axis=-1, keepdims=True)
        on = d * lax.rsqrt(var + EPS) * gn
        g = rg_ref[0, pl.ds(r0, C), :]
        o_ref[0, pl.ds(r0, C), :] = (g * jax.nn.sigmoid(g) * on).astype(o_ref.dtype)
        return R * cd + kv

    lax.fori_loop(0, n_chunks, chunk, jnp.zeros((RET_DK, RET_DV), F32))


def _retention(rq, rk, rv, rg, g_gn, decay, zeta, xi, cd):
    B, S, _ = rq.shape
    tokh = pl.BlockSpec((1, S, LANES), lambda b, h: (b, 0, h))
    tab = pl.BlockSpec((1, RET_CHUNK, LANES), lambda b, h: (h, 0, 0))
    return pl.pallas_call(
        functools.partial(_retention_kernel, n_chunks=S // RET_CHUNK),
        grid=(B, RET_HEADS),
        in_specs=[tokh, tokh, tokh, tokh, pl.BlockSpec((1, LANES), lambda b, h: (0, h)),
                  tab, tab, tab, tab],
        out_specs=tokh,
        out_shape=jax.ShapeDtypeStruct((B, S, RET_HEADS * RET_DV), BF16),
        compiler_params=_params(("parallel", "parallel")),
        name="retention",
    )(rq, rk, rv, rg, g_gn, decay, zeta, xi, cd)


def _mem_kv_kernel(mem_ref, g_ref, w_ref, k_ref, v_ref):
    D = mem_ref.shape[-1]
    mn = _rms(mem_ref[0], g_ref[...]).astype(BF16)
    kv = _dot(mn, w_ref[...])
    k_ref[0] = kv[:, :D].astype(BF16)
    v_ref[0] = kv[:, D:].astype(BF16)


def _mem_kv(mem, g_mem, w_xkv):
    B, M, D = mem.shape
    blk = pl.BlockSpec((1, M, D), lambda b: (b, 0, 0))
    return pl.pallas_call(
        _mem_kv_kernel,
        grid=(B,),
        in_specs=[blk, _const_spec(g_mem.shape), _const_spec(w_xkv.shape)],
        out_specs=[blk, blk],
        out_shape=[jax.ShapeDtypeStruct((B, M, D), BF16)] * 2,
        compiler_params=_params(("parallel",)),
        name="mem_kv",
    )(mem, g_mem, w_xkv)


def _post_mix_kernel(x_ref, omla_ref, oret_ref, woa_ref, wob_ref, gx_ref, wxq_ref, kx_ref, vx_ref, wxo_ref,
                     gf_ref, wpq_ref, keys_ref, h2_ref, a3t_ref, st_ref, ox_ref):
    D = x_ref.shape[-1]
    h1 = x_ref[...] + _dot(omla_ref[...], woa_ref[...]) + _dot(oret_ref[...], wob_ref[...])
    a2 = _rms(h1, gx_ref[...]).astype(BF16)
    qx = _dot(a2, wxq_ref[...])
    hd = D // X_HEADS
    for h in range(X_HEADS):
        sl = slice(h * hd, (h + 1) * hd)
        s = _dot_nt(qx[:, sl].astype(BF16), kx_ref[0, :, sl]) * (hd ** -0.5)
        m = jnp.max(s, axis=-1, keepdims=True)
        p = jnp.exp(s - m)
        p = p / jnp.sum(p, axis=-1, keepdims=True)
        ox_ref[:, sl] = _dot(p.astype(BF16), vx_ref[0, :, sl]).astype(BF16)
    h2 = h1 + _dot(ox_ref[...], wxo_ref[...])
    h2_ref[...] = h2
    a3 = _rms(h2, gf_ref[...])
    a3t_ref[...] = a3.T.astype(BF16)
    qp = _dot(a3.astype(BF16), wpq_ref[...]).astype(BF16)
    half = PEER_DKEY // 2
    for hp in range(2 * PEER_HEADS):
        st_ref[hp] = _dot_nt(keys_ref[hp % 2], qp[:, hp * half:(hp + 1) * half])


def _post_mix(x2, o_mla, o_ret, w_oa, w_ob, g_x, w_xq, kx, vx, w_xo, g_f, w_pq, keys, seq, tm):
    T, D = x2.shape
    nS = seq // tm
    tok = lambda w: pl.BlockSpec((tm, w), lambda i: (i, 0))
    memb = pl.BlockSpec((1,) + kx.shape[1:], lambda i: (i // nS, 0, 0))
    return pl.pallas_call(
        _post_mix_kernel,
        grid=(T // tm,),
        in_specs=[tok(D), tok(o_mla.shape[1]), tok(o_ret.shape[1]), _const_spec(w_oa.shape),
                  _const_spec(w_ob.shape), _const_spec(g_x.shape), _const_spec(w_xq.shape), memb, memb,
                  _const_spec(w_xo.shape), _const_spec(g_f.shape), _const_spec(w_pq.shape),
                  _const_spec(keys.shape)],
        out_specs=[tok(D), pl.BlockSpec((D, tm), lambda i: (0, i)),
                   pl.BlockSpec((2 * PEER_HEADS, PEER_KEYS, tm), lambda i: (0, 0, i))],
        out_shape=[jax.ShapeDtypeStruct((T, D), F32), jax.ShapeDtypeStruct((D, T), BF16),
                   jax.ShapeDtypeStruct((2 * PEER_HEADS, PEER_KEYS, T), F32)],
        scratch_shapes=[pltpu.VMEM((tm, D), BF16)],
        compiler_params=_params(("parallel",)),
        name="post_mix",
    )(x2, o_mla, o_ret, w_oa, w_ob, g_x, w_xq, kx, vx, w_xo, g_f, w_pq, keys)


N_TOP = PEER_TOPK + 1
TOP_ROWS = 24


def _peer_route_kernel(st_ref, stats_ref, vs_ref):
    tl = st_ref.shape[-1]
    vs_ref[...] = jnp.full(vs_ref.shape, NEG, F32)
    for h in range(PEER_HEADS):
        for p in range(2):
            cur = st_ref[2 * h + p]
            for it in range(N_TOP):
                m = jnp.max(cur, axis=0, keepdims=True)
                vs_ref[p, it:it + 1, :] = m
                if it + 1 < N_TOP:
                    cur = jnp.where(cur == m, NEG, cur)
        v1 = lambda i: vs_ref[0, i:i + 1, :]
        v2 = lambda i: vs_ref[1, i:i + 1, :]
        cands = [v1(0) + vs_ref[1, 8 * r:8 * r + 8, :] for r in range(TOP_ROWS // 8)]
        cands += [v1(i) + vs_ref[1, 0:8, :] for i in range(1, 8)]
        cands += [vs_ref[0, 8 * r:8 * r + 8, :] + v2(0) for r in range(1, TOP_ROWS // 8)]
        c = jnp.concatenate(cands, axis=0)
        top = v1(0) + v2(0)
        z = jnp.zeros((1, tl), F32)
        c16 = top
        c17 = top
        for it in range(N_TOP):
            m = jnp.max(c, axis=0, keepdims=True)
            if it < PEER_TOPK:
                z = z + jnp.exp(m - top)
                c16 = m
                c = jnp.where(c == m, NEG, c)
            else:
                c17 = m
        stats_ref[0, h:h + 1, :] = 0.5 * (c16 + c17)
        stats_ref[1, h:h + 1, :] = v1(0)
        stats_ref[2, h:h + 1, :] = v2(0)
        stats_ref[3, h:h + 1, :] = 1.0 / z


def _peer_route(st, tl):
    HP, K, T = st.shape
    return pl.pallas_call(
        _peer_route_kernel,
        grid=(T // tl,),
        in_specs=[pl.BlockSpec((HP, K, tl), lambda i: (0, 0, i))],
        out_specs=pl.BlockSpec((4, PEER_HEADS, tl), lambda i: (0, 0, i)),
        out_shape=jax.ShapeDtypeStruct((4, PEER_HEADS, T), F32),
        scratch_shapes=[pltpu.VMEM((2, TOP_ROWS, tl), F32)],
        compiler_params=_params(("parallel",)),
        name="peer_route",
    )(st)


def _peer_dense_kernel(a3t_ref, st_ref, stats_ref, u_ref, vt_ref, out_ref,
                       thr_ref, c1_ref, e2_ref, act_ref, p_ref, *, te, tm):
    j = pl.program_id(1)
    nk = PEER_KEYS

    @pl.when(j == 0)
    def _():
        out_ref[...] = jnp.zeros(out_ref.shape, F32)
        for h in range(PEER_HEADS):
            s1 = st_ref[2 * h]
            thr_ref[h] = stats_ref[0, h:h + 1, :] - s1
            c1_ref[h] = jnp.exp(s1 - stats_ref[1, h:h + 1, :]) * stats_ref[3, h:h + 1, :]
            e2_ref[h] = jnp.exp(st_ref[2 * h + 1] - stats_ref[2, h:h + 1, :])

    act_ref[...] = _dot(u_ref[...], a3t_ref[...])

    a0 = pl.multiple_of(j * (te // nk), te // nk)
    for tl in range(tm // LANES):
        ls = slice(tl * LANES, (tl + 1) * LANES)
        thr = [thr_ref[h, pl.ds(a0, te // nk), ls] for h in range(PEER_HEADS)]
        c1 = [c1_ref[h, pl.ds(a0, te // nk), ls] for h in range(PEER_HEADS)]
        for al in range(te // nk):
            w = jnp.zeros((nk, LANES), F32)
            for h in range(PEER_HEADS):
                sel = st_ref[2 * h + 1, :, ls] >= thr[h][al:al + 1, :]
                w = w + jnp.where(sel, e2_ref[h, :, ls] * c1[h][al:al + 1, :], 0.0)
            rs = slice(al * nk, (al + 1) * nk)
            p_ref[rs, ls] = (_gelu(act_ref[rs, ls]) * w).astype(BF16)
    out_ref[...] += _dot(vt_ref[...], p_ref[...])


def _peer_dense(a3t, st, stats, u_bf, vt_bf, tm, te):
    D, T = a3t.shape
    E = u_bf.shape[0]
    return pl.pallas_call(
        functools.partial(_peer_dense_kernel, te=te, tm=tm),
        grid=(T // tm, E // te),
        in_specs=[pl.BlockSpec((D, tm), lambda i, j: (0, i)),
                  pl.BlockSpec(st.shape[:2] + (tm,), lambda i, j: (0, 0, i)),
                  pl.BlockSpec(stats.shape[:2] + (tm,), lambda i, j: (0, 0, i)),
                  pl.BlockSpec((te, D), lambda i, j: (j, 0)),
                  pl.BlockSpec((D, te), lambda i, j: (0, j))],
        out_specs=pl.BlockSpec((D, tm), lambda i, j: (0, i)),
        out_shape=jax.ShapeDtypeStruct((D, T), F32),
        scratch_shapes=[pltpu.VMEM((PEER_HEADS, PEER_KEYS, tm), F32)] * 3
        + [pltpu.VMEM((te, tm), F32), pltpu.VMEM((te, tm), BF16)],
        compiler_params=_params(("parallel", "arbitrary")),
        name="peer_dense",
    )(a3t, st, stats, u_bf, vt_bf)


def _final_kernel(h2_ref, pt_ref, g_ref, o_ref):
    o_ref[...] = _rms(h2_ref[...] + pt_ref[...].T, g_ref[...])


def _final(h2, peer_t, g, tm):
    T, D = h2.shape
    return pl.pallas_call(
        _final_kernel,
        grid=(T // tm,),
        in_specs=[pl.BlockSpec((tm, D), lambda i: (i, 0)), pl.BlockSpec((D, tm), lambda i: (0, i)),
                  _const_spec(g.shape)],
        out_specs=pl.BlockSpec((tm, D), lambda i: (i, 0)),
        out_shape=jax.ShapeDtypeStruct((T, D), F32),
        compiler_params=_params(("parallel",)),
        name="final",
    )(h2, peer_t, g)


def _rope_tables(seq):
    pos = jnp.arange(seq, dtype=F32)

    def tab(dim):
        inv = 1.0 / (ROPE_BASE ** (jnp.arange(0, dim, 2, dtype=F32) / dim))
        ang = pos[:, None] * inv[None, :]
        return jnp.cos(ang), jnp.sin(ang)

    cm, sm = tab(MLA_ROPE)
    pad = LANES - MLA_NOPE - MLA_ROPE
    cosm = jnp.concatenate([jnp.ones((seq, MLA_NOPE), F32), cm, cm, jnp.zeros((seq, pad), F32)], axis=1)
    sinm = jnp.concatenate([jnp.zeros((seq, MLA_NOPE), F32), -sm, sm, jnp.zeros((seq, pad), F32)], axis=1)
    cr, sr = tab(RET_DK)
    cosr = jnp.concatenate([cr, cr], axis=1)
    sinr = jnp.concatenate([-sr, sr], axis=1)
    return cosm, sinm, cosr, sinr


def _retention_tables():
    C = RET_CHUNK
    log_g = jnp.log(1.0 - 2.0 ** (-5.0 - jnp.arange(RET_HEADS, dtype=F32)))
    idx = jnp.arange(C, dtype=F32)
    rel = idx[:, None] - idx[None, :]
    decay = jnp.where(rel[None] >= 0, jnp.exp(jnp.maximum(rel, 0.0)[None] * log_g[:, None, None]), 0.0)
    zeta = jnp.exp((C - 1 - idx)[None, :] * log_g[:, None])
    xi = jnp.exp((idx + 1)[None, :] * log_g[:, None])
    cd = jnp.exp(C * log_g)
    bc = lambda t: jnp.broadcast_to(t[:, :, None], (RET_HEADS, C, LANES))
    return decay, bc(zeta), bc(xi), jnp.broadcast_to(cd[:, None, None], (RET_HEADS, C, LANES))


def _layer(h, mem, g_mix, w_in, g_q_lora, w_uq, g_kv_lora, w_ukv, g_ret_gn, w_o, g_xattn, g_mem,
           w_xq, w_xkv, w_xo, g_ffn, w_pq, sub_keys, u_experts, v_experts, g_out):
    B, S, D = h.shape
    T = B * S
    row = lambda g: g.reshape(1, -1)

    o_pe = MLA_Q_RANK + MLA_KV_RANK
    pe_pad = jnp.zeros((D, LANES), F32).at[:, MLA_NOPE:MLA_NOPE + MLA_ROPE].set(w_in[:, o_pe:o_pe + MLA_ROPE])
    w_in_ext = jnp.concatenate([w_in[:, :o_pe], pe_pad, w_in[:, o_pe + MLA_ROPE:]], axis=1).astype(BF16)
    dq = MLA_NOPE + MLA_ROPE
    w_uq_pad = jnp.pad(w_uq.reshape(MLA_Q_RANK, MLA_HEADS, dq), ((0, 0), (0, 0), (0, LANES - dq)))
    w_uq_pad = w_uq_pad.reshape(MLA_Q_RANK, MLA_HEADS * LANES).astype(BF16)
    w_ukv3 = w_ukv.reshape(MLA_KV_RANK, MLA_HEADS, MLA_NOPE + MLA_V)
    w_k = jnp.pad(w_ukv3[:, :, :MLA_NOPE], ((0, 0), (0, 0), (0, LANES - MLA_NOPE)))
    w_k = w_k.reshape(MLA_KV_RANK, MLA_HEADS * LANES).astype(BF16)
    w_v = jnp.pad(w_ukv3[:, :, MLA_NOPE:], ((0, 0), (0, 0), (0, LANES - MLA_V)))
    w_v = w_v.reshape(MLA_KV_RANK, MLA_HEADS * LANES).astype(BF16)

    cosm, sinm, cosr, sinr = _rope_tables(S)
    q, k, v, rq, rk, rv, rg = _in_proj(h.reshape(T, D), row(g_mix), w_in_ext, row(g_q_lora), w_uq_pad,
                                       row(g_kv_lora), w_k, w_v, cosm, sinm, cosr, sinr, S, min(256, S))
    r3 = lambda t: t.reshape(B, S, t.shape[-1])
    o_mla = _mla_attn(r3(q), r3(k), r3(v), min(512, S))
    o_ret = _retention(r3(rq), r3(rk), r3(rv), r3(rg), row(g_ret_gn), *_retention_tables())

    kx, vx = _mem_kv(mem, row(g_mem), w_xkv.astype(BF16))
    n_mla = MLA_HEADS * MLA_V
    w_o_bf = w_o.astype(BF16)
    h2, a3t, st = _post_mix(h.reshape(T, D), o_mla.reshape(T, -1), o_ret.reshape(T, -1), w_o_bf[:n_mla],
                            w_o_bf[n_mla:], row(g_xattn), w_xq.astype(BF16), kx, vx, w_xo.astype(BF16),
                            row(g_ffn), w_pq.astype(BF16), sub_keys.astype(BF16), S, min(256, S))
    stats = _peer_route(st, LANES)
    peer_t = _peer_dense(a3t, st, stats, u_experts.astype(BF16), v_experts.T.astype(BF16),
                         min(512, T), 1024)
    return _final(h2, peer_t, row(g_out), min(512, T)).reshape(B, S, D)


def kernel(x, mem, g_mix, w_in, g_q_lora, w_uq, g_kv_lora, w_ukv, g_ret_gn, w_o, g_xattn, g_mem, w_xq, w_xkv,
           w_xo, g_ffn, w_pq, sub_keys, u_experts, v_experts, g_final):
    depth = g_mix.shape[0]
    assert depth == 1, "the final norm is fused into the single layer's last kernel"
    l = 0
    return _layer(x, mem, g_mix[l], w_in[l], g_q_lora[l], w_uq[l], g_kv_lora[l], w_ukv[l], g_ret_gn[l], w_o[l],
                  g_xattn[l], g_mem[l], w_xq[l], w_xkv[l], w_xo[l], g_ffn[l], w_pq[l], sub_keys[l],
                  u_experts[l], v_experts[l], g_final)
```

```python
import functools
import math

import jax
import jax.numpy as jnp
from jax import lax
from jax.experimental import pallas as pl
from jax.experimental.pallas import tpu as pltpu

F32 = jnp.float32
BF16 = jnp.bfloat16

LANES = 128
EPS = 1e-6
NEG = -1e30
ROPE_BASE = 10000.0
VMEM_LIMIT = 56 * 1024 * 1024

MLA_HEADS, MLA_Q_RANK, MLA_KV_RANK = 8, 384, 256
MLA_NOPE, MLA_ROPE, MLA_V = 64, 32, 64
RET_HEADS, RET_DK, RET_DV, RET_CHUNK = 4, 128, 128, 128
X_HEADS = 4
PEER_KEYS, PEER_HEADS, PEER_TOPK, PEER_DKEY = 128, 8, 16, 256


def _params(sem, vmem=VMEM_LIMIT):
    return pltpu.CompilerParams(dimension_semantics=sem, vmem_limit_bytes=vmem)


def _rms(x, g):
    return x * lax.rsqrt(jnp.mean(x * x, axis=-1, keepdims=True) + EPS) * g


def _dot(a, b):
    return jnp.dot(a, b, preferred_element_type=F32)


def _dot_nt(a, b):
    return lax.dot_general(a, b, (((1,), (1,)), ((), ())), preferred_element_type=F32)


def _gelu(x):
    return 0.5 * x * (1.0 + lax.erf(x * (2.0 ** -0.5)))


def _const_spec(shape):
    nd = len(shape)
    return pl.BlockSpec(shape, lambda *_: (0,) * nd)


def _in_proj_kernel(x_ref, gmix_ref, win_ref, gq_ref, wuq_ref, gkv_ref, wk_ref, wv_ref,
                    cosm_ref, sinm_ref, cosr_ref, sinr_ref,
                    q_ref, k_ref, v_ref, rq_ref, rk_ref, rv_ref, rg_ref):
    x = x_ref[...]
    a = _rms(x, gmix_ref[...]).astype(BF16)
    proj = _dot(a, win_ref[...])
    o_cq, o_ckv, o_pe, o_rq = 0, MLA_Q_RANK, MLA_Q_RANK + MLA_KV_RANK, MLA_Q_RANK + MLA_KV_RANK + LANES
    rw = RET_HEADS * RET_DK
    cq = proj[:, o_cq:o_ckv]
    ckv = proj[:, o_ckv:o_pe]
    kpe = proj[:, o_pe:o_rq]
    cqn = _rms(cq, gq_ref[...]).astype(BF16)
    ckvn = _rms(ckv, gkv_ref[...]).astype(BF16)
    q = _dot(cqn, wuq_ref[...])
    kn = _dot(ckvn, wk_ref[...])
    vlane = lax.broadcasted_iota(jnp.int32, (x.shape[0], MLA_HEADS * LANES), 1) % LANES
    v_ref[...] = jnp.where(vlane == MLA_V, 1.0, _dot(ckvn, wv_ref[...])).astype(BF16)

    cosm, sinm = cosm_ref[...], sinm_ref[...]
    lane = lax.broadcasted_iota(jnp.int32, (x.shape[0], LANES), 1)
    first_half = lane < MLA_NOPE + MLA_ROPE // 2

    def rope_m(c):
        rot = jnp.where(first_half, pltpu.roll(c, LANES - MLA_ROPE // 2, axis=1),
                        pltpu.roll(c, MLA_ROPE // 2, axis=1))
        return c * cosm + rot * sinm

    qscale = (MLA_NOPE + MLA_ROPE) ** -0.5
    kpe_r = rope_m(kpe)
    for h in range(MLA_HEADS):
        sl = slice(h * LANES, (h + 1) * LANES)
        q_ref[:, sl] = (rope_m(q[:, sl]) * qscale).astype(BF16)
        k_ref[:, sl] = (kn[:, sl] + kpe_r).astype(BF16)

    cosr, sinr = cosr_ref[...], sinr_ref[...]
    kscale = RET_DK ** -0.5
    for h in range(RET_HEADS):
        sl = slice(h * LANES, (h + 1) * LANES)
        c = proj[:, o_rq + h * LANES:o_rq + (h + 1) * LANES]
        rq_ref[:, sl] = (c * cosr + pltpu.roll(c, RET_DK // 2, axis=1) * sinr).astype(BF16)
        c = proj[:, o_rq + rw + h * LANES:o_rq + rw + (h + 1) * LANES]
        rk_ref[:, sl] = ((c * cosr + pltpu.roll(c, RET_DK // 2, axis=1) * sinr) * kscale).astype(BF16)
    rv_ref[...] = proj[:, o_rq + 2 * rw:o_rq + 3 * rw].astype(BF16)
    rg_ref[...] = proj[:, o_rq + 3 * rw:o_rq + 4 * rw]


def _in_proj(x2, g_mix, w_in_ext, g_q, w_uq_pad, g_kv, w_k, w_v, cosm, sinm, cosr, sinr, seq, tm):
    T, D = x2.shape
    nS = seq // tm
    tok = lambda w: pl.BlockSpec((tm, w), lambda i: (i, 0))
    pos = lambda w: pl.BlockSpec((tm, w), lambda i: (i % nS, 0))
    outs = [(T, MLA_HEADS * LANES, BF16), (T, MLA_HEADS * LANES, BF16), (T, MLA_HEADS * LANES, BF16),
            (T, 512, BF16), (T, 512, BF16), (T, 512, BF16), (T, 512, F32)]
    return pl.pallas_call(
        _in_proj_kernel,
        grid=(T // tm,),
        in_specs=[tok(D), _const_spec(g_mix.shape), _const_spec(w_in_ext.shape), _const_spec(g_q.shape),
                  _const_spec(w_uq_pad.shape), _const_spec(g_kv.shape), _const_spec(w_k.shape),
                  _const_spec(w_v.shape), pos(LANES), pos(LANES), pos(LANES), pos(LANES)],
        out_specs=[tok(w) for (_, w, _) in outs],
        out_shape=[jax.ShapeDtypeStruct((t, w), d) for (t, w, d) in outs],
        compiler_params=_params(("parallel",)),
        name="in_proj",
    )(x2, g_mix, w_in_ext, g_q, w_uq_pad, g_kv, w_k, w_v, cosm, sinm, cosr, sinr)


def _mla_attn_kernel(q_ref, k_ref, v_ref, o_ref, *, t):
    i = pl.program_id(2)
    causal = (lax.broadcasted_iota(jnp.int32, (t, t), 1) <= lax.broadcasted_iota(jnp.int32, (t, t), 0))

    def step(j, carry, masked):
        k0 = pl.multiple_of(j * t, t)
        new = []
        for hh in range(2):
            m, acc = carry[hh]
            hs = slice(hh * LANES, (hh + 1) * LANES)
            s = _dot_nt(q_ref[0, :, hs], k_ref[0, pl.ds(k0, t), hs])
            if masked:
                s = jnp.where(causal, s, NEG)
            m_new = jnp.maximum(m, jnp.max(s, axis=-1, keepdims=True))
            p = jnp.exp(s - m_new).astype(BF16)
            acc = jnp.exp(m - m_new) * acc + _dot(p, v_ref[0, pl.ds(k0, t), hs])
            new.append((m_new, acc))
        return tuple(new)

    init = tuple((jnp.full((t, 1), NEG, F32), jnp.zeros((t, LANES), F32)) for _ in range(2))
    carry = lax.fori_loop(0, i, functools.partial(step, masked=False), init)
    carry = step(i, carry, True)
    outs = [acc[:, :MLA_V] / acc[:, MLA_V:MLA_V + 1] for (_, acc) in carry]
    o_ref[0] = jnp.concatenate(outs, axis=-1).astype(o_ref.dtype)


def _mla_attn(q, k, v, t):
    B, S, _ = q.shape
    return pl.pallas_call(
        functools.partial(_mla_attn_kernel, t=t),
        grid=(B, MLA_HEADS // 2, S // t),
        in_specs=[pl.BlockSpec((1, t, 2 * LANES), lambda b, h, i: (b, i, h)),
                  pl.BlockSpec((1, S, 2 * LANES), lambda b, h, i: (b, 0, h)),
                  pl.BlockSpec((1, S, 2 * LANES), lambda b, h, i: (b, 0, h))],
        out_specs=pl.BlockSpec((1, t, 2 * MLA_V), lambda b, h, i: (b, i, h)),
        out_shape=jax.ShapeDtypeStruct((B, S, MLA_HEADS * MLA_V), BF16),
        compiler_params=_params(("parallel", "parallel", "arbitrary")),
        name="mla_attn",
    )(q, k, v)


def _retention_kernel(rq_ref, rk_ref, rv_ref, rg_ref, gn_ref, decay_ref, zeta_ref, xi_ref, cd_ref,
                      o_ref, r_ref, *, n_chunks):
    C = RET_CHUNK

    @pl.when(pl.program_id(1) == 0)
    def _():
        r_ref[...] = jnp.zeros(r_ref.shape, F32)

    def chunk(n, carry):
        r0 = pl.multiple_of(n * C, C)
        for h in range(RET_HEADS):
            hs = slice(h * LANES, (h + 1) * LANES)
            qc = rq_ref[0, pl.ds(r0, C), hs]
            kc = rk_ref[0, pl.ds(r0, C), hs]
            vc = rv_ref[0, pl.ds(r0, C), hs]
            state = r_ref[h]
            sc = _dot_nt(qc, kc) * decay_ref[h]
            inner = _dot(sc.astype(BF16), vc)
            cross = _dot(qc, state.astype(BF16)) * xi_ref[h]
            o = inner + cross
            kz = (kc.astype(F32) * zeta_ref[h]).T.astype(BF16)
            r_ref[h] = state * cd_ref[h] + _dot(kz, vc)
            mu = jnp.mean(o, axis=-1, keepdims=True)
            d = o - mu
            var = jnp.mean(d * d, axis=-1, keepdims=True)
            on = d * lax.rsqrt(var + EPS) * gn_ref[:, hs]
            g = rg_ref[0, pl.ds(r0, C), hs]
            o_ref[0, pl.ds(r0, C), hs] = (g * jax.nn.sigmoid(g) * on).astype(o_ref.dtype)
        return carry

    lax.fori_loop(0, n_chunks, chunk, 0)


def _retention(rq, rk, rv, rg, g_gn, decay, zeta, xi, cd, ts):
    B, S, W = rq.shape
    tok = pl.BlockSpec((1, ts, W), lambda b, i: (b, i, 0))
    return pl.pallas_call(
        functools.partial(_retention_kernel, n_chunks=ts // RET_CHUNK),
        grid=(B, S // ts),
        in_specs=[tok, tok, tok, tok, _const_spec(g_gn.shape), _const_spec(decay.shape),
                  _const_spec(zeta.shape), _const_spec(xi.shape), _const_spec(cd.shape)],
        out_specs=tok,
        out_shape=jax.ShapeDtypeStruct((B, S, W), BF16),
        scratch_shapes=[pltpu.VMEM((RET_HEADS, RET_DK, RET_DV), F32)],
        compiler_params=_params(("parallel", "arbitrary")),
        name="retention",
    )(rq, rk, rv, rg, g_gn, decay, zeta, xi, cd)


def _mem_kv_kernel(mem_ref, g_ref, w_ref, k_ref, v_ref):
    D = mem_ref.shape[-1]
    mn = _rms(mem_ref[0], g_ref[...]).astype(BF16)
    kv = _dot(mn, w_ref[...])
    k_ref[0] = kv[:, :D].astype(BF16)
    v_ref[0] = kv[:, D:].astype(BF16)


def _mem_kv(mem, g_mem, w_xkv):
    B, M, D = mem.shape
    blk = pl.BlockSpec((1, M, D), lambda b: (b, 0, 0))
    return pl.pallas_call(
        _mem_kv_kernel,
        grid=(B,),
        in_specs=[blk, _const_spec(g_mem.shape), _const_spec(w_xkv.shape)],
        out_specs=[blk, blk],
        out_shape=[jax.ShapeDtypeStruct((B, M, D), BF16)] * 2,
        compiler_params=_params(("parallel",)),
        name="mem_kv",
    )(mem, g_mem, w_xkv)


def _post_mix_kernel(x_ref, omla_ref, oret_ref, woa_ref, wob_ref, gx_ref, wxq_ref, kx_ref, vx_ref, wxo_ref,
                     gf_ref, wpq_ref, keys_ref, h2_ref, a3t_ref, st_ref, ox_ref):
    D = x_ref.shape[-1]
    h1 = x_ref[...] + _dot(omla_ref[...], woa_ref[...]) + _dot(oret_ref[...], wob_ref[...])
    a2 = _rms(h1, gx_ref[...]).astype(BF16)
    qx = _dot(a2, wxq_ref[...])
    hd = D // X_HEADS
    for h in range(X_HEADS):
        sl = slice(h * hd, (h + 1) * hd)
        s = _dot_nt(qx[:, sl].astype(BF16), kx_ref[0, :, sl]) * (hd ** -0.5)
        m = jnp.max(s, axis=-1, keepdims=True)
        p = jnp.exp(s - m)
        p = p / jnp.sum(p, axis=-1, keepdims=True)
        ox_ref[:, sl] = _dot(p.astype(BF16), vx_ref[0, :, sl]).astype(BF16)
    h2 = h1 + _dot(ox_ref[...], wxo_ref[...])
    h2_ref[...] = h2
    a3 = _rms(h2, gf_ref[...])
    a3t_ref[...] = a3.T.astype(BF16)
    qp = _dot(a3.astype(BF16), wpq_ref[...]).astype(BF16)
    half = PEER_DKEY // 2
    for hp in range(2 * PEER_HEADS):
        sc = _dot_nt(keys_ref[hp % 2], qp[:, hp * half:(hp + 1) * half])
        for tl in range(sc.shape[1] // LANES):
            st_ref[hp, tl] = sc[:, tl * LANES:(tl + 1) * LANES]


def _post_mix(x2, o_mla, o_ret, w_oa, w_ob, g_x, w_xq, kx, vx, w_xo, g_f, w_pq, keys, seq, tm):
    T, D = x2.shape
    nS = seq // tm
    tok = lambda w: pl.BlockSpec((tm, w), lambda i: (i, 0))
    memb = pl.BlockSpec((1,) + kx.shape[1:], lambda i: (i // nS, 0, 0))
    return pl.pallas_call(
        _post_mix_kernel,
        grid=(T // tm,),
        in_specs=[tok(D), tok(o_mla.shape[1]), tok(o_ret.shape[1]), _const_spec(w_oa.shape),
                  _const_spec(w_ob.shape), _const_spec(g_x.shape), _const_spec(w_xq.shape), memb, memb,
                  _const_spec(w_xo.shape), _const_spec(g_f.shape), _const_spec(w_pq.shape),
                  _const_spec(keys.shape)],
        out_specs=[tok(D), pl.BlockSpec((D, tm), lambda i: (0, i)),
                   pl.BlockSpec((2 * PEER_HEADS, tm // LANES, PEER_KEYS, LANES), lambda i: (0, i, 0, 0))],
        out_shape=[jax.ShapeDtypeStruct((T, D), F32), jax.ShapeDtypeStruct((D, T), BF16),
                   jax.ShapeDtypeStruct((2 * PEER_HEADS, T // LANES, PEER_KEYS, LANES), F32)],
        scratch_shapes=[pltpu.VMEM((tm, D), BF16)],
        compiler_params=_params(("parallel",)),
        name="post_mix",
    )(x2, o_mla, o_ret, w_oa, w_ob, g_x, w_xq, kx, vx, w_xo, g_f, w_pq, keys)


N_TOP = PEER_TOPK + 1
TOP_ROWS = 24


def _peer_route_kernel(st_ref, stats_ref, vs_ref):
    tl = st_ref.shape[-1]
    vs_ref[...] = jnp.full(vs_ref.shape, NEG, F32)
    for h in range(PEER_HEADS):
        for p in range(2):
            cur = st_ref[2 * h + p, 0]
            for it in range(N_TOP):
                m = jnp.max(cur, axis=0, keepdims=True)
                vs_ref[p, it:it + 1, :] = m
                if it + 1 < N_TOP:
                    cur = jnp.where(cur == m, NEG, cur)
        v1 = lambda i: vs_ref[0, i:i + 1, :]
        v2 = lambda i: vs_ref[1, i:i + 1, :]
        cands = [v1(0) + vs_ref[1, 8 * r:8 * r + 8, :] for r in range(TOP_ROWS // 8)]
        cands += [v1(i) + vs_ref[1, 0:8, :] for i in range(1, 8)]
        cands += [vs_ref[0, 8 * r:8 * r + 8, :] + v2(0) for r in range(1, TOP_ROWS // 8)]
        c = jnp.concatenate(cands, axis=0)
        top = v1(0) + v2(0)
        z = jnp.zeros((1, tl), F32)
        c16 = top
        c17 = top
        for it in range(N_TOP):
            m = jnp.max(c, axis=0, keepdims=True)
            if it < PEER_TOPK:
                z = z + jnp.exp(m - top)
                c16 = m
                c = jnp.where(c == m, NEG, c)
            else:
                c17 = m
        stats_ref[0, h:h + 1, :] = 0.5 * (c16 + c17)
        stats_ref[1, h:h + 1, :] = v1(0)
        stats_ref[2, h:h + 1, :] = v2(0)
        stats_ref[3, h:h + 1, :] = 1.0 / z


def _peer_route(st):
    HP, nb, K, tl = st.shape
    T = nb * tl
    return pl.pallas_call(
        _peer_route_kernel,
        grid=(nb,),
        in_specs=[pl.BlockSpec((HP, 1, K, tl), lambda i: (0, i, 0, 0))],
        out_specs=pl.BlockSpec((4, PEER_HEADS, tl), lambda i: (0, 0, i)),
        out_shape=jax.ShapeDtypeStruct((4, PEER_HEADS, T), F32),
        scratch_shapes=[pltpu.VMEM((2, TOP_ROWS, tl), F32)],
        compiler_params=_params(("parallel",)),
        name="peer_route",
    )(st)


def _peer_dense_kernel(a3t_ref, st_ref, stats_ref, u_ref, vt_ref, out_ref, thr_ref, c1_ref, e2_ref, *, te, tm):
    j = pl.program_id(1)
    nk = PEER_KEYS
    n_al = te // nk
    nt = tm // LANES

    @pl.when(j == 0)
    def _():
        out_ref[...] = jnp.zeros(out_ref.shape, F32)
        for h in range(PEER_HEADS):
            for tl in range(nt):
                ls = slice(tl * LANES, (tl + 1) * LANES)
                s1 = st_ref[2 * h, tl]
                thr_ref[h, :, tl, :] = stats_ref[0, h:h + 1, ls] - s1
                c1_ref[h, :, tl, :] = jnp.exp(s1 - stats_ref[1, h:h + 1, ls]) * stats_ref[3, h:h + 1, ls]
                e2_ref[h, tl] = jnp.exp(st_ref[2 * h + 1, tl] - stats_ref[2, h:h + 1, ls])

    act = _dot(u_ref[...], a3t_ref[...])
    p_rows = []
    for al in range(n_al):
        a = j * n_al + al
        rs = slice(al * nk, (al + 1) * nk)
        tiles = []
        for tl in range(nt):
            w = jnp.zeros((nk, LANES), F32)
            for h in range(PEER_HEADS):
                thr = thr_ref[h, a, tl:tl + 1, :]
                c1 = c1_ref[h, a, tl:tl + 1, :]
                w = w + jnp.where(st_ref[2 * h + 1, tl] >= thr, e2_ref[h, tl] * c1, 0.0)
            tiles.append((_gelu(act[rs, tl * LANES:(tl + 1) * LANES]) * w).astype(BF16))
        p_rows.append(jnp.concatenate(tiles, axis=1))
    out_ref[...] += _dot(vt_ref[...], jnp.concatenate(p_rows, axis=0))


def _peer_dense(a3t, st, stats, u_bf, vt_bf, tm, te):
    D, T = a3t.shape
    nt = tm // LANES
    return pl.pallas_call(
        functools.partial(_peer_dense_kernel, te=te, tm=tm),
        grid=(T // tm, u_bf.shape[0] // te),
        in_specs=[pl.BlockSpec((D, tm), lambda i, j: (0, i)),
                  pl.BlockSpec((st.shape[0], nt) + st.shape[2:], lambda i, j: (0, i, 0, 0)),
                  pl.BlockSpec(stats.shape[:2] + (tm,), lambda i, j: (0, 0, i)),
                  pl.BlockSpec((te, D), lambda i, j: (j, 0)),
                  pl.BlockSpec((D, te), lambda i, j: (0, j))],
        out_specs=pl.BlockSpec((D, tm), lambda i, j: (0, i)),
        out_shape=jax.ShapeDtypeStruct((D, T), F32),
        scratch_shapes=[pltpu.VMEM((PEER_HEADS, PEER_KEYS, nt, LANES), F32)] * 2
        + [pltpu.VMEM((PEER_HEADS, nt, PEER_KEYS, LANES), F32)],
        compiler_params=_params(("parallel", "arbitrary")),
        name="peer_dense",
    )(a3t, st, stats, u_bf, vt_bf)


def _final_kernel(h2_ref, pt_ref, g_ref, o_ref):
    o_ref[...] = _rms(h2_ref[...] + pt_ref[...].T, g_ref[...])


def _final(h2, peer_t, g, tm):
    T, D = h2.shape
    return pl.pallas_call(
        _final_kernel,
        grid=(T // tm,),
        in_specs=[pl.BlockSpec((tm, D), lambda i: (i, 0)), pl.BlockSpec((D, tm), lambda i: (0, i)),
                  _const_spec(g.shape)],
        out_specs=pl.BlockSpec((tm, D), lambda i: (i, 0)),
        out_shape=jax.ShapeDtypeStruct((T, D), F32),
        compiler_params=_params(("parallel",)),
        name="final",
    )(h2, peer_t, g)


def _rope_tables(seq):
    pos = jnp.arange(seq, dtype=F32)

    def tab(dim):
        inv = 1.0 / (ROPE_BASE ** (jnp.arange(0, dim, 2, dtype=F32) / dim))
        ang = pos[:, None] * inv[None, :]
        return jnp.cos(ang), jnp.sin(ang)

    cm, sm = tab(MLA_ROPE)
    pad = LANES - MLA_NOPE - MLA_ROPE
    cosm = jnp.concatenate([jnp.ones((seq, MLA_NOPE), F32), cm, cm, jnp.zeros((seq, pad), F32)], axis=1)
    sinm = jnp.concatenate([jnp.zeros((seq, MLA_NOPE), F32), -sm, sm, jnp.zeros((seq, pad), F32)], axis=1)
    cr, sr = tab(RET_DK)
    cosr = jnp.concatenate([cr, cr], axis=1)
    sinr = jnp.concatenate([-sr, sr], axis=1)
    return cosm, sinm, cosr, sinr


def _retention_tables():
    C = RET_CHUNK
    log_g = jnp.log(1.0 - 2.0 ** (-5.0 - jnp.arange(RET_HEADS, dtype=F32)))
    idx = jnp.arange(C, dtype=F32)
    rel = idx[:, None] - idx[None, :]
    decay = jnp.where(rel[None] >= 0, jnp.exp(jnp.maximum(rel, 0.0)[None] * log_g[:, None, None]), 0.0)
    zeta = jnp.exp((C - 1 - idx)[None, :] * log_g[:, None])
    xi = jnp.exp((idx + 1)[None, :] * log_g[:, None])
    cd = jnp.exp(C * log_g)
    bc = lambda t: jnp.broadcast_to(t[:, :, None], (RET_HEADS, C, LANES))
    return decay, bc(zeta), bc(xi), jnp.broadcast_to(cd[:, None, None], (RET_HEADS, C, LANES))


def _layer(h, mem, g_mix, w_in, g_q_lora, w_uq, g_kv_lora, w_ukv, g_ret_gn, w_o, g_xattn, g_mem,
           w_xq, w_xkv, w_xo, g_ffn, w_pq, sub_keys, u_experts, v_experts, g_out):
    B, S, D = h.shape
    T = B * S
    row = lambda g: g.reshape(1, -1)

    o_pe = MLA_Q_RANK + MLA_KV_RANK
    pe_pad = jnp.zeros((D, LANES), F32).at[:, MLA_NOPE:MLA_NOPE + MLA_ROPE].set(w_in[:, o_pe:o_pe + MLA_ROPE])
    w_in_ext = jnp.concatenate([w_in[:, :o_pe], pe_pad, w_in[:, o_pe + MLA_ROPE:]], axis=1).astype(BF16)
    dq = MLA_NOPE + MLA_ROPE
    w_uq_pad = jnp.pad(w_uq.reshape(MLA_Q_RANK, MLA_HEADS, dq), ((0, 0), (0, 0), (0, LANES - dq)))
    w_uq_pad = w_uq_pad.reshape(MLA_Q_RANK, MLA_HEADS * LANES).astype(BF16)
    w_ukv3 = w_ukv.reshape(MLA_KV_RANK, MLA_HEADS, MLA_NOPE + MLA_V)
    w_k = jnp.pad(w_ukv3[:, :, :MLA_NOPE], ((0, 0), (0, 0), (0, LANES - MLA_NOPE)))
    w_k = w_k.reshape(MLA_KV_RANK, MLA_HEADS * LANES).astype(BF16)
    w_v = jnp.pad(w_ukv3[:, :, MLA_NOPE:], ((0, 0), (0, 0), (0, LANES - MLA_V)))
    w_v = w_v.reshape(MLA_KV_RANK, MLA_HEADS * LANES).astype(BF16)

    cosm, sinm, cosr, sinr = _rope_tables(S)
    q, k, v, rq, rk, rv, rg = _in_proj(h.reshape(T, D), row(g_mix), w_in_ext, row(g_q_lora), w_uq_pad,
                                       row(g_kv_lora), w_k, w_v, cosm, sinm, cosr, sinr, S, min(256, S))
    r3 = lambda t: t.reshape(B, S, t.shape[-1])
    o_mla = _mla_attn(r3(q), r3(k), r3(v), min(512, S))
    o_ret = _retention(r3(rq), r3(rk), r3(rv), r3(rg), row(g_ret_gn), *_retention_tables(), min(1024, S))

    kx, vx = _mem_kv(mem, row(g_mem), w_xkv.astype(BF16))
    n_mla = MLA_HEADS * MLA_V
    w_o_bf = w_o.astype(BF16)
    h2, a3t, st = _post_mix(h.reshape(T, D), o_mla.reshape(T, -1), o_ret.reshape(T, -1), w_o_bf[:n_mla],
                            w_o_bf[n_mla:], row(g_xattn), w_xq.astype(BF16), kx, vx, w_xo.astype(BF16),
                            row(g_ffn), w_pq.astype(BF16), sub_keys.astype(BF16), S, min(256, S))
    stats = _peer_route(st)
    peer_t = _peer_dense(a3t, st, stats, u_experts.astype(BF16), v_experts.T.astype(BF16),
                         min(512, T), 1024)
    return _final(h2, peer_t, row(g_out), min(512, T)).reshape(B, S, D)


def kernel(x, mem, g_mix, w_in, g_q_lora, w_uq, g_kv_lora, w_ukv, g_ret_gn, w_o, g_xattn, g_mem, w_xq, w_xkv,
           w_xo, g_ffn, w_pq, sub_keys, u_experts, v_experts, g_final):
    depth = g_mix.shape[0]
    assert depth == 1, "the final norm is fused into the single layer's last kernel"
    l = 0
    return _layer(x, mem, g_mix[l], w_in[l], g_q_lora[l], w_uq[l], g_kv_lora[l], w_ukv[l], g_ret_gn[l], w_o[l],
                  g_xattn[l], g_mem[l], w_xq[l], w_xkv[l], w_xo[l], g_ffn[l], w_pq[l], sub_keys[l],
                  u_experts[l], v_experts[l], g_final)
```

```python
import functools
import math

import jax
import jax.numpy as jnp
from jax import lax
from jax.experimental import pallas as pl
from jax.experimental.pallas import tpu as pltpu

F32 = jnp.float32
BF16 = jnp.bfloat16

LANES = 128
EPS = 1e-6
NEG = -1e30
ROPE_BASE = 10000.0
VMEM_LIMIT = 56 * 1024 * 1024

MLA_HEADS, MLA_Q_RANK, MLA_KV_RANK = 8, 384, 256
MLA_NOPE, MLA_ROPE, MLA_V = 64, 32, 64
RET_HEADS, RET_DK, RET_DV, RET_CHUNK = 4, 128, 128, 128
X_HEADS = 4
PEER_KEYS, PEER_HEADS, PEER_TOPK, PEER_DKEY = 128, 8, 16, 256


def _params(sem, vmem=VMEM_LIMIT):
    return pltpu.CompilerParams(dimension_semantics=sem, vmem_limit_bytes=vmem)


def _rms(x, g):
    return x * lax.rsqrt(jnp.mean(x * x, axis=-1, keepdims=True) + EPS) * g


def _dot(a, b):
    return jnp.dot(a, b, preferred_element_type=F32)


def _dot_nt(a, b):
    return lax.dot_general(a, b, (((1,), (1,)), ((), ())), preferred_element_type=F32)


def _gelu(x):
    return 0.5 * x * (1.0 + lax.erf(x * (2.0 ** -0.5)))


def _const_spec(shape):
    nd = len(shape)
    return pl.BlockSpec(shape, lambda *_: (0,) * nd)


def _in_proj_kernel(x_ref, gmix_ref, win_ref, gq_ref, wuq_ref, gkv_ref, wk_ref, wv_ref,
                    cosm_ref, sinm_ref, cosr_ref, sinr_ref,
                    q_ref, k_ref, v_ref, rq_ref, rk_ref, rv_ref, rg_ref):
    x = x_ref[...]
    a = _rms(x, gmix_ref[...]).astype(BF16)
    proj = _dot(a, win_ref[...])
    o_cq, o_ckv, o_pe, o_rq = 0, MLA_Q_RANK, MLA_Q_RANK + MLA_KV_RANK, MLA_Q_RANK + MLA_KV_RANK + LANES
    rw = RET_HEADS * RET_DK
    cq = proj[:, o_cq:o_ckv]
    ckv = proj[:, o_ckv:o_pe]
    kpe = proj[:, o_pe:o_rq]
    cqn = _rms(cq, gq_ref[...]).astype(BF16)
    ckvn = _rms(ckv, gkv_ref[...]).astype(BF16)
    q = _dot(cqn, wuq_ref[...])
    kn = _dot(ckvn, wk_ref[...])
    vlane = lax.broadcasted_iota(jnp.int32, (x.shape[0], MLA_HEADS * LANES), 1) % LANES
    v_ref[...] = jnp.where(vlane == MLA_V, 1.0, _dot(ckvn, wv_ref[...])).astype(BF16)

    cosm, sinm = cosm_ref[...], sinm_ref[...]
    lane = lax.broadcasted_iota(jnp.int32, (x.shape[0], LANES), 1)
    first_half = lane < MLA_NOPE + MLA_ROPE // 2

    def rope_m(c):
        rot = jnp.where(first_half, pltpu.roll(c, LANES - MLA_ROPE // 2, axis=1),
                        pltpu.roll(c, MLA_ROPE // 2, axis=1))
        return c * cosm + rot * sinm

    qscale = (MLA_NOPE + MLA_ROPE) ** -0.5
    kpe_r = rope_m(kpe)
    for h in range(MLA_HEADS):
        sl = slice(h * LANES, (h + 1) * LANES)
        q_ref[:, sl] = (rope_m(q[:, sl]) * qscale).astype(BF16)
        k_ref[:, sl] = (kn[:, sl] + kpe_r).astype(BF16)

    cosr, sinr = cosr_ref[...], sinr_ref[...]
    kscale = RET_DK ** -0.5
    for h in range(RET_HEADS):
        sl = slice(h * LANES, (h + 1) * LANES)
        c = proj[:, o_rq + h * LANES:o_rq + (h + 1) * LANES]
        rq_ref[:, sl] = (c * cosr + pltpu.roll(c, RET_DK // 2, axis=1) * sinr).astype(BF16)
        c = proj[:, o_rq + rw + h * LANES:o_rq + rw + (h + 1) * LANES]
        rk_ref[:, sl] = ((c * cosr + pltpu.roll(c, RET_DK // 2, axis=1) * sinr) * kscale).astype(BF16)
    rv_ref[...] = proj[:, o_rq + 2 * rw:o_rq + 3 * rw].astype(BF16)
    rg_ref[...] = proj[:, o_rq + 3 * rw:o_rq + 4 * rw]


def _in_proj(x2, g_mix, w_in_ext, g_q, w_uq_pad, g_kv, w_k, w_v, cosm, sinm, cosr, sinr, seq, tm):
    T, D = x2.shape
    nS = seq // tm
    tok = lambda w: pl.BlockSpec((tm, w), lambda i: (i, 0))
    pos = lambda w: pl.BlockSpec((tm, w), lambda i: (i % nS, 0))
    outs = [(T, MLA_HEADS * LANES, BF16), (T, MLA_HEADS * LANES, BF16), (T, MLA_HEADS * LANES, BF16),
            (T, 512, BF16), (T, 512, BF16), (T, 512, BF16), (T, 512, F32)]
    return pl.pallas_call(
        _in_proj_kernel,
        grid=(T // tm,),
        in_specs=[tok(D), _const_spec(g_mix.shape), _const_spec(w_in_ext.shape), _const_spec(g_q.shape),
                  _const_spec(w_uq_pad.shape), _const_spec(g_kv.shape), _const_spec(w_k.shape),
                  _const_spec(w_v.shape), pos(LANES), pos(LANES), pos(LANES), pos(LANES)],
        out_specs=[tok(w) for (_, w, _) in outs],
        out_shape=[jax.ShapeDtypeStruct((t, w), d) for (t, w, d) in outs],
        compiler_params=_params(("parallel",)),
        name="in_proj",
    )(x2, g_mix, w_in_ext, g_q, w_uq_pad, g_kv, w_k, w_v, cosm, sinm, cosr, sinr)


def _mla_attn_kernel(q_ref, k_ref, v_ref, o_ref, *, t):
    i = pl.program_id(2)
    causal = (lax.broadcasted_iota(jnp.int32, (t, t), 1) <= lax.broadcasted_iota(jnp.int32, (t, t), 0))

    def step(j, carry, masked):
        k0 = pl.multiple_of(j * t, t)
        new = []
        for hh in range(2):
            m, acc = carry[hh]
            hs = slice(hh * LANES, (hh + 1) * LANES)
            s = _dot_nt(q_ref[0, :, hs], k_ref[0, pl.ds(k0, t), hs])
            if masked:
                s = jnp.where(causal, s, NEG)
            m_new = jnp.maximum(m, jnp.max(s, axis=-1, keepdims=True))
            p = jnp.exp(s - m_new).astype(BF16)
            acc = jnp.exp(m - m_new) * acc + _dot(p, v_ref[0, pl.ds(k0, t), hs])
            new.append((m_new, acc))
        return tuple(new)

    init = tuple((jnp.full((t, 1), NEG, F32), jnp.zeros((t, LANES), F32)) for _ in range(2))
    carry = lax.fori_loop(0, i, functools.partial(step, masked=False), init)
    carry = step(i, carry, True)
    outs = [acc[:, :MLA_V] / acc[:, MLA_V:MLA_V + 1] for (_, acc) in carry]
    o_ref[0] = jnp.concatenate(outs, axis=-1).astype(o_ref.dtype)


def _mla_attn(q, k, v, t):
    B, S, _ = q.shape
    return pl.pallas_call(
        functools.partial(_mla_attn_kernel, t=t),
        grid=(B, MLA_HEADS // 2, S // t),
        in_specs=[pl.BlockSpec((1, t, 2 * LANES), lambda b, h, i: (b, i, h)),
                  pl.BlockSpec((1, S, 2 * LANES), lambda b, h, i: (b, 0, h)),
                  pl.BlockSpec((1, S, 2 * LANES), lambda b, h, i: (b, 0, h))],
        out_specs=pl.BlockSpec((1, t, 2 * MLA_V), lambda b, h, i: (b, i, h)),
        out_shape=jax.ShapeDtypeStruct((B, S, MLA_HEADS * MLA_V), BF16),
        compiler_params=_params(("parallel", "parallel", "arbitrary")),
        name="mla_attn",
    )(q, k, v)


def _retention_kernel(rq_ref, rk_ref, rv_ref, rg_ref, gn_ref, decay_ref, zeta_ref, xi_ref, cd_ref,
                      o_ref, r_ref, *, n_chunks):
    C = RET_CHUNK

    @pl.when(pl.program_id(1) == 0)
    def _():
        r_ref[...] = jnp.zeros(r_ref.shape, F32)

    def chunk(n, carry):
        r0 = pl.multiple_of(n * C, C)
        for h in range(RET_HEADS):
            hs = slice(h * LANES, (h + 1) * LANES)
            qc = rq_ref[0, pl.ds(r0, C), hs]
            kc = rk_ref[0, pl.ds(r0, C), hs]
            vc = rv_ref[0, pl.ds(r0, C), hs]
            state = r_ref[h]
            sc = _dot_nt(qc, kc) * decay_ref[h]
            inner = _dot(sc.astype(BF16), vc)
            cross = _dot(qc, state.astype(BF16)) * xi_ref[h]
            o = inner + cross
            kz = (kc.astype(F32) * zeta_ref[h]).T.astype(BF16)
            r_ref[h] = state * cd_ref[h] + _dot(kz, vc)
            mu = jnp.mean(o, axis=-1, keepdims=True)
            d = o - mu
            var = jnp.mean(d * d, axis=-1, keepdims=True)
            on = d * lax.rsqrt(var + EPS) * gn_ref[:, hs]
            g = rg_ref[0, pl.ds(r0, C), hs]
            o_ref[0, pl.ds(r0, C), hs] = (g * jax.nn.sigmoid(g) * on).astype(o_ref.dtype)
        return carry

    lax.fori_loop(0, n_chunks, chunk, 0)


def _retention(rq, rk, rv, rg, g_gn, decay, zeta, xi, cd, ts):
    B, S, W = rq.shape
    tok = pl.BlockSpec((1, ts, W), lambda b, i: (b, i, 0))
    return pl.pallas_call(
        functools.partial(_retention_kernel, n_chunks=ts // RET_CHUNK),
        grid=(B, S // ts),
        in_specs=[tok, tok, tok, tok, _const_spec(g_gn.shape), _const_spec(decay.shape),
                  _const_spec(zeta.shape), _const_spec(xi.shape), _const_spec(cd.shape)],
        out_specs=tok,
        out_shape=jax.ShapeDtypeStruct((B, S, W), BF16),
        scratch_shapes=[pltpu.VMEM((RET_HEADS, RET_DK, RET_DV), F32)],
        compiler_params=_params(("parallel", "arbitrary")),
        name="retention",
    )(rq, rk, rv, rg, g_gn, decay, zeta, xi, cd)


def _mem_kv_kernel(mem_ref, g_ref, w_ref, k_ref, v_ref):
    D = mem_ref.shape[-1]
    mn = _rms(mem_ref[0], g_ref[...]).astype(BF16)
    kv = _dot(mn, w_ref[...])
    k_ref[0] = kv[:, :D].astype(BF16)
    v_ref[0] = kv[:, D:].astype(BF16)


def _mem_kv(mem, g_mem, w_xkv):
    B, M, D = mem.shape
    blk = pl.BlockSpec((1, M, D), lambda b: (b, 0, 0))
    return pl.pallas_call(
        _mem_kv_kernel,
        grid=(B,),
        in_specs=[blk, _const_spec(g_mem.shape), _const_spec(w_xkv.shape)],
        out_specs=[blk, blk],
        out_shape=[jax.ShapeDtypeStruct((B, M, D), BF16)] * 2,
        compiler_params=_params(("parallel",)),
        name="mem_kv",
    )(mem, g_mem, w_xkv)


def _post_mix_kernel(x_ref, omla_ref, oret_ref, woa_ref, wob_ref, gx_ref, wxq_ref, kx_ref, vx_ref, wxo_ref,
                     gf_ref, wpq_ref, keys_ref, h2_ref, a3t_ref, st_ref, ox_ref):
    D = x_ref.shape[-1]
    h1 = x_ref[...] + _dot(omla_ref[...], woa_ref[...]) + _dot(oret_ref[...], wob_ref[...])
    a2 = _rms(h1, gx_ref[...]).astype(BF16)
    qx = _dot(a2, wxq_ref[...])
    hd = D // X_HEADS
    for h in range(X_HEADS):
        sl = slice(h * hd, (h + 1) * hd)
        s = _dot_nt(qx[:, sl].astype(BF16), kx_ref[0, :, sl]) * (hd ** -0.5)
        m = jnp.max(s, axis=-1, keepdims=True)
        p = jnp.exp(s - m)
        p = p / jnp.sum(p, axis=-1, keepdims=True)
        ox_ref[:, sl] = _dot(p.astype(BF16), vx_ref[0, :, sl]).astype(BF16)
    h2 = h1 + _dot(ox_ref[...], wxo_ref[...])
    h2_ref[...] = h2
    a3 = _rms(h2, gf_ref[...])
    a3t_ref[...] = a3.T.astype(BF16)
    qp = _dot(a3.astype(BF16), wpq_ref[...]).astype(BF16)
    half = PEER_DKEY // 2
    for hp in range(2 * PEER_HEADS):
        sc = _dot_nt(keys_ref[hp % 2], qp[:, hp * half:(hp + 1) * half])
        for tl in range(sc.shape[1] // LANES):
            st_ref[hp, tl] = sc[:, tl * LANES:(tl + 1) * LANES]


def _post_mix(x2, o_mla, o_ret, w_oa, w_ob, g_x, w_xq, kx, vx, w_xo, g_f, w_pq, keys, seq, tm):
    T, D = x2.shape
    nS = seq // tm
    tok = lambda w: pl.BlockSpec((tm, w), lambda i: (i, 0))
    memb = pl.BlockSpec((1,) + kx.shape[1:], lambda i: (i // nS, 0, 0))
    return pl.pallas_call(
        _post_mix_kernel,
        grid=(T // tm,),
        in_specs=[tok(D), tok(o_mla.shape[1]), tok(o_ret.shape[1]), _const_spec(w_oa.shape),
                  _const_spec(w_ob.shape), _const_spec(g_x.shape), _const_spec(w_xq.shape), memb, memb,
                  _const_spec(w_xo.shape), _const_spec(g_f.shape), _const_spec(w_pq.shape),
                  _const_spec(keys.shape)],
        out_specs=[tok(D), pl.BlockSpec((D, tm), lambda i: (0, i)),
                   pl.BlockSpec((2 * PEER_HEADS, tm // LANES, PEER_KEYS, LANES), lambda i: (0, i, 0, 0))],
        out_shape=[jax.ShapeDtypeStruct((T, D), F32), jax.ShapeDtypeStruct((D, T), BF16),
                   jax.ShapeDtypeStruct((2 * PEER_HEADS, T // LANES, PEER_KEYS, LANES), F32)],
        scratch_shapes=[pltpu.VMEM((tm, D), BF16)],
        compiler_params=_params(("parallel",)),
        name="post_mix",
    )(x2, o_mla, o_ret, w_oa, w_ob, g_x, w_xq, kx, vx, w_xo, g_f, w_pq, keys)


N_TOP = PEER_TOPK + 1
TOP_ROWS = 24


def _peer_route_kernel(st_ref, r2_ref, e2_ref, n_ref, c1_ref, vs_ref):
    tl = st_ref.shape[-1]
    vs_ref[...] = jnp.full(vs_ref.shape, NEG, F32)
    for h in range(PEER_HEADS):
        for p in range(2):
            cur = st_ref[2 * h + p, 0]
            rank = jnp.full(cur.shape, float(PEER_TOPK), F32)
            for it in range(N_TOP):
                m = jnp.max(cur, axis=0, keepdims=True)
                vs_ref[p, it:it + 1, :] = m
                if it + 1 < N_TOP:
                    hit = cur == m
                    if p == 1:
                        rank = jnp.where(hit, float(it), rank)
                    cur = jnp.where(hit, NEG, cur)
            if p == 1:
                r2_ref[h, 0] = rank.astype(BF16)
        v1 = lambda i: vs_ref[0, i:i + 1, :]
        v2 = lambda i: vs_ref[1, i:i + 1, :]
        cands = [v1(0) + vs_ref[1, 8 * r:8 * r + 8, :] for r in range(TOP_ROWS // 8)]
        cands += [v1(i) + vs_ref[1, 0:8, :] for i in range(1, 8)]
        cands += [vs_ref[0, 8 * r:8 * r + 8, :] + v2(0) for r in range(1, TOP_ROWS // 8)]
        c = jnp.concatenate(cands, axis=0)
        top = v1(0) + v2(0)
        z = jnp.zeros((1, tl), F32)
        c16 = top
        c17 = top
        for it in range(N_TOP):
            m = jnp.max(c, axis=0, keepdims=True)
            if it < PEER_TOPK:
                z = z + jnp.exp(m - top)
                c16 = m
                c = jnp.where(c == m, NEG, c)
            else:
                c17 = m
        s1 = st_ref[2 * h, 0]
        thr = 0.5 * (c16 + c17) - s1
        ge = lambda row: row >= thr
        pick = lambda cond, hi, lo: jnp.where(cond, hi, lo)
        b8 = ge(v2(7))
        b4 = ge(pick(b8, v2(11), v2(3)))
        b2 = ge(pick(b4, pick(b8, v2(13), v2(5)), pick(b8, v2(9), v2(1))))
        b1 = ge(pick(b2, pick(b4, pick(b8, v2(14), v2(6)), pick(b8, v2(10), v2(2))),
                     pick(b4, pick(b8, v2(12), v2(4)), pick(b8, v2(8), v2(0)))))
        one = lambda cond, val: jnp.where(cond, val, 0.0)
        n = one(b8, 8.0) + one(b4, 4.0) + one(b2, 2.0) + one(b1, 1.0) + one(ge(v2(15)), 1.0)
        n_ref[h, 0] = n.astype(BF16)
        c1_ref[h, 0] = jnp.exp(s1 - v1(0)) * (1.0 / z)
        e2_ref[h, 0] = jnp.exp(st_ref[2 * h + 1, 0] - v2(0)).astype(BF16)


def _peer_route(st):
    HP, nb, K, tl = st.shape
    blk = pl.BlockSpec((PEER_HEADS, 1, K, tl), lambda i: (0, i, 0, 0))
    tab = lambda dt: jax.ShapeDtypeStruct((PEER_HEADS, nb, K, tl), dt)
    return pl.pallas_call(
        _peer_route_kernel,
        grid=(nb,),
        in_specs=[pl.BlockSpec((HP, 1, K, tl), lambda i: (0, i, 0, 0))],
        out_specs=[blk, blk, blk, blk],
        out_shape=[tab(BF16), tab(BF16), tab(BF16), tab(F32)],
        scratch_shapes=[pltpu.VMEM((2, TOP_ROWS, tl), F32)],
        compiler_params=_params(("parallel",)),
        name="peer_route",
    )(st)


def _peer_dense_kernel(a3t_ref, r2_ref, e2_ref, n_ref, c1_ref, u_ref, v_ref, h2_ref, g_ref, o_ref,
                       nrow_ref, crow_ref, acc_ref, r2s_ref, e2s_ref, *, te, tm):
    j = pl.program_id(1)
    nk = PEER_KEYS
    n_al = te // nk
    nt = tm // LANES

    @pl.when(j == 0)
    def _():
        acc_ref[...] = jnp.zeros(acc_ref.shape, F32)
        for h in range(PEER_HEADS):
            for tl in range(nt):
                nrow_ref[h, :, tl, :] = n_ref[h, tl].astype(F32)
                crow_ref[h, :, tl, :] = c1_ref[h, tl]
                r2s_ref[h, tl] = r2_ref[h, tl]
                e2s_ref[h, tl] = e2_ref[h, tl]

    act = _dot(u_ref[...], a3t_ref[...])
    zero = jnp.zeros((), BF16)
    p_rows = []
    for al in range(n_al):
        a = j * n_al + al
        rs = slice(al * nk, (al + 1) * nk)
        tiles = []
        for tl in range(nt):
            w = jnp.zeros((nk, LANES), BF16)
            for h in range(PEER_HEADS):
                nb = jnp.broadcast_to(nrow_ref[h, a, tl:tl + 1, :], (nk, LANES)).astype(BF16)
                cb = jnp.broadcast_to(crow_ref[h, a, tl:tl + 1, :], (nk, LANES)).astype(BF16)
                w = w + jnp.where(r2s_ref[h, tl] < nb, e2s_ref[h, tl], zero) * cb
            tiles.append(_gelu(act[rs, tl * LANES:(tl + 1) * LANES]).astype(BF16) * w)
        p_rows.append(jnp.concatenate(tiles, axis=1))
    p = jnp.concatenate(p_rows, axis=0)
    acc_ref[...] += lax.dot_general(v_ref[...], p, (((0,), (0,)), ((), ())), preferred_element_type=F32)

    @pl.when(j == pl.num_programs(1) - 1)
    def _():
        o_ref[...] = _rms(h2_ref[...] + acc_ref[...].T, g_ref[...])


def _peer_dense(a3t, r2, e2, n, c1, u_bf, v_bf, h2, g, tm, te):
    D, T = a3t.shape
    nt = tm // LANES
    tab = pl.BlockSpec((PEER_HEADS, nt) + r2.shape[2:], lambda i, j: (0, i, 0, 0))
    return pl.pallas_call(
        functools.partial(_peer_dense_kernel, te=te, tm=tm),
        grid=(T // tm, u_bf.shape[0] // te),
        in_specs=[pl.BlockSpec((D, tm), lambda i, j: (0, i)), tab, tab, tab, tab,
                  pl.BlockSpec((te, D), lambda i, j: (j, 0)),
                  pl.BlockSpec((te, D), lambda i, j: (j, 0)),
                  pl.BlockSpec((tm, D), lambda i, j: (i, 0)), _const_spec(g.shape)],
        out_specs=pl.BlockSpec((tm, D), lambda i, j: (i, 0)),
        out_shape=jax.ShapeDtypeStruct((T, D), F32),
        scratch_shapes=[pltpu.VMEM((PEER_HEADS, PEER_KEYS, nt, LANES), F32)] * 2
        + [pltpu.VMEM((D, tm), F32)] + [pltpu.VMEM((PEER_HEADS, nt, PEER_KEYS, LANES), BF16)] * 2,
        compiler_params=_params(("parallel", "arbitrary")),
        name="peer_dense",
    )(a3t, r2, e2, n, c1, u_bf, v_bf, h2, g)


def _rope_tables(seq):
    pos = jnp.arange(seq, dtype=F32)

    def tab(dim):
        inv = 1.0 / (ROPE_BASE ** (jnp.arange(0, dim, 2, dtype=F32) / dim))
        ang = pos[:, None] * inv[None, :]
        return jnp.cos(ang), jnp.sin(ang)

    cm, sm = tab(MLA_ROPE)
    pad = LANES - MLA_NOPE - MLA_ROPE
    cosm = jnp.concatenate([jnp.ones((seq, MLA_NOPE), F32), cm, cm, jnp.zeros((seq, pad), F32)], axis=1)
    sinm = jnp.concatenate([jnp.zeros((seq, MLA_NOPE), F32), -sm, sm, jnp.zeros((seq, pad), F32)], axis=1)
    cr, sr = tab(RET_DK)
    cosr = jnp.concatenate([cr, cr], axis=1)
    sinr = jnp.concatenate([-sr, sr], axis=1)
    return cosm, sinm, cosr, sinr


def _retention_tables():
    C = RET_CHUNK
    log_g = jnp.log(1.0 - 2.0 ** (-5.0 - jnp.arange(RET_HEADS, dtype=F32)))
    idx = jnp.arange(C, dtype=F32)
    rel = idx[:, None] - idx[None, :]
    decay = jnp.where(rel[None] >= 0, jnp.exp(jnp.maximum(rel, 0.0)[None] * log_g[:, None, None]), 0.0)
    zeta = jnp.exp((C - 1 - idx)[None, :] * log_g[:, None])
    xi = jnp.exp((idx + 1)[None, :] * log_g[:, None])
    cd = jnp.exp(C * log_g)
    bc = lambda t: jnp.broadcast_to(t[:, :, None], (RET_HEADS, C, LANES))
    return decay, bc(zeta), bc(xi), jnp.broadcast_to(cd[:, None, None], (RET_HEADS, C, LANES))


def _layer(h, mem, g_mix, w_in, g_q_lora, w_uq, g_kv_lora, w_ukv, g_ret_gn, w_o, g_xattn, g_mem,
           w_xq, w_xkv, w_xo, g_ffn, w_pq, sub_keys, u_experts, v_experts, g_out):
    B, S, D = h.shape
    T = B * S
    row = lambda g: g.reshape(1, -1)

    o_pe = MLA_Q_RANK + MLA_KV_RANK
    pe_pad = jnp.zeros((D, LANES), F32).at[:, MLA_NOPE:MLA_NOPE + MLA_ROPE].set(w_in[:, o_pe:o_pe + MLA_ROPE])
    w_in_ext = jnp.concatenate([w_in[:, :o_pe], pe_pad, w_in[:, o_pe + MLA_ROPE:]], axis=1).astype(BF16)
    dq = MLA_NOPE + MLA_ROPE
    w_uq_pad = jnp.pad(w_uq.reshape(MLA_Q_RANK, MLA_HEADS, dq), ((0, 0), (0, 0), (0, LANES - dq)))
    w_uq_pad = w_uq_pad.reshape(MLA_Q_RANK, MLA_HEADS * LANES).astype(BF16)
    w_ukv3 = w_ukv.reshape(MLA_KV_RANK, MLA_HEADS, MLA_NOPE + MLA_V)
    w_k = jnp.pad(w_ukv3[:, :, :MLA_NOPE], ((0, 0), (0, 0), (0, LANES - MLA_NOPE)))
    w_k = w_k.reshape(MLA_KV_RANK, MLA_HEADS * LANES).astype(BF16)
    w_v = jnp.pad(w_ukv3[:, :, MLA_NOPE:], ((0, 0), (0, 0), (0, LANES - MLA_V)))
    w_v = w_v.reshape(MLA_KV_RANK, MLA_HEADS * LANES).astype(BF16)

    cosm, sinm, cosr, sinr = _rope_tables(S)
    q, k, v, rq, rk, rv, rg = _in_proj(h.reshape(T, D), row(g_mix), w_in_ext, row(g_q_lora), w_uq_pad,
                                       row(g_kv_lora), w_k, w_v, cosm, sinm, cosr, sinr, S, min(256, S))
    r3 = lambda t: t.reshape(B, S, t.shape[-1])
    o_mla = _mla_attn(r3(q), r3(k), r3(v), min(512, S))
    o_ret = _retention(r3(rq), r3(rk), r3(rv), r3(rg), row(g_ret_gn), *_retention_tables(), min(1024, S))

    kx, vx = _mem_kv(mem, row(g_mem), w_xkv.astype(BF16))
    n_mla = MLA_HEADS * MLA_V
    w_o_bf = w_o.astype(BF16)
    h2, a3t, st = _post_mix(h.reshape(T, D), o_mla.reshape(T, -1), o_ret.reshape(T, -1), w_o_bf[:n_mla],
                            w_o_bf[n_mla:], row(g_xattn), w_xq.astype(BF16), kx, vx, w_xo.astype(BF16),
                            row(g_ffn), w_pq.astype(BF16), sub_keys.astype(BF16), S, min(256, S))
    r2, e2, n, c1 = _peer_route(st)
    out = _peer_dense(a3t, r2, e2, n, c1, u_experts.astype(BF16), v_experts.astype(BF16), h2, row(g_out),
                      min(512, T), 1024)
    return out.reshape(B, S, D)


def kernel(x, mem, g_mix, w_in, g_q_lora, w_uq, g_kv_lora, w_ukv, g_ret_gn, w_o, g_xattn, g_mem, w_xq, w_xkv,
           w_xo, g_ffn, w_pq, sub_keys, u_experts, v_experts, g_final):
    depth = g_mix.shape[0]
    assert depth == 1, "the final norm is fused into the single layer's last kernel"
    l = 0
    return _layer(x, mem, g_mix[l], w_in[l], g_q_lora[l], w_uq[l], g_kv_lora[l], w_ukv[l], g_ret_gn[l], w_o[l],
                  g_xattn[l], g_mem[l], w_xq[l], w_xkv[l], w_xo[l], g_ffn[l], w_pq[l], sub_keys[l],
                  u_experts[l], v_experts[l], g_final)
```

```python
import functools
import math

import jax
import jax.numpy as jnp
from jax import lax
from jax.experimental import pallas as pl
from jax.experimental.pallas import tpu as pltpu

F32 = jnp.float32
BF16 = jnp.bfloat16

LANES = 128
EPS = 1e-6
NEG = -1e30
ROPE_BASE = 10000.0
VMEM_LIMIT = 56 * 1024 * 1024

MLA_HEADS, MLA_Q_RANK, MLA_KV_RANK = 8, 384, 256
MLA_NOPE, MLA_ROPE, MLA_V = 64, 32, 64
RET_HEADS, RET_DK, RET_DV, RET_CHUNK = 4, 128, 128, 128
X_HEADS = 4
PEER_KEYS, PEER_HEADS, PEER_TOPK, PEER_DKEY = 128, 8, 16, 256


def _params(sem, vmem=VMEM_LIMIT):
    return pltpu.CompilerParams(dimension_semantics=sem, vmem_limit_bytes=vmem)


def _rms(x, g):
    return x * lax.rsqrt(jnp.mean(x * x, axis=-1, keepdims=True) + EPS) * g


def _dot(a, b):
    return jnp.dot(a, b, preferred_element_type=F32)


def _dot_nt(a, b):
    return lax.dot_general(a, b, (((1,), (1,)), ((), ())), preferred_element_type=F32)


def _gelu(x):
    return 0.5 * x * (1.0 + lax.erf(x * (2.0 ** -0.5)))


def _const_spec(shape):
    nd = len(shape)
    return pl.BlockSpec(shape, lambda *_: (0,) * nd)


def _in_proj_kernel(x_ref, gmix_ref, win_ref, gq_ref, wuq_ref, gkv_ref, wk_ref, wv_ref,
                    cosm_ref, sinm_ref, cosr_ref, sinr_ref,
                    q_ref, k_ref, v_ref, rq_ref, rk_ref, rv_ref, rg_ref):
    x = x_ref[...]
    a = _rms(x, gmix_ref[...]).astype(BF16)
    proj = _dot(a, win_ref[...])
    o_cq, o_ckv, o_pe, o_rq = 0, MLA_Q_RANK, MLA_Q_RANK + MLA_KV_RANK, MLA_Q_RANK + MLA_KV_RANK + LANES
    rw = RET_HEADS * RET_DK
    cq = proj[:, o_cq:o_ckv]
    ckv = proj[:, o_ckv:o_pe]
    kpe = proj[:, o_pe:o_rq]
    cqn = _rms(cq, gq_ref[...]).astype(BF16)
    ckvn = _rms(ckv, gkv_ref[...]).astype(BF16)
    q = _dot(cqn, wuq_ref[...])
    kn = _dot(ckvn, wk_ref[...])
    vlane = lax.broadcasted_iota(jnp.int32, (x.shape[0], MLA_HEADS * LANES), 1) % LANES
    v_ref[...] = jnp.where(vlane == MLA_V, 1.0, _dot(ckvn, wv_ref[...])).astype(BF16)

    cosm, sinm = cosm_ref[...], sinm_ref[...]
    lane = lax.broadcasted_iota(jnp.int32, (x.shape[0], LANES), 1)
    first_half = lane < MLA_NOPE + MLA_ROPE // 2

    def rope_m(c):
        rot = jnp.where(first_half, pltpu.roll(c, LANES - MLA_ROPE // 2, axis=1),
                        pltpu.roll(c, MLA_ROPE // 2, axis=1))
        return c * cosm + rot * sinm

    qscale = (MLA_NOPE + MLA_ROPE) ** -0.5
    kpe_r = rope_m(kpe)
    for h in range(MLA_HEADS):
        sl = slice(h * LANES, (h + 1) * LANES)
        q_ref[:, sl] = (rope_m(q[:, sl]) * qscale).astype(BF16)
        k_ref[:, sl] = (kn[:, sl] + kpe_r).astype(BF16)

    cosr, sinr = cosr_ref[...], sinr_ref[...]
    kscale = RET_DK ** -0.5
    for h in range(RET_HEADS):
        sl = slice(h * LANES, (h + 1) * LANES)
        c = proj[:, o_rq + h * LANES:o_rq + (h + 1) * LANES]
        rq_ref[:, sl] = (c * cosr + pltpu.roll(c, RET_DK // 2, axis=1) * sinr).astype(BF16)
        c = proj[:, o_rq + rw + h * LANES:o_rq + rw + (h + 1) * LANES]
        rk_ref[:, sl] = ((c * cosr + pltpu.roll(c, RET_DK // 2, axis=1) * sinr) * kscale).astype(BF16)
    rv_ref[...] = proj[:, o_rq + 2 * rw:o_rq + 3 * rw].astype(BF16)
    rg_ref[...] = proj[:, o_rq + 3 * rw:o_rq + 4 * rw]


def _in_proj(x2, g_mix, w_in_ext, g_q, w_uq_pad, g_kv, w_k, w_v, cosm, sinm, cosr, sinr, seq, tm):
    T, D = x2.shape
    nS = seq // tm
    tok = lambda w: pl.BlockSpec((tm, w), lambda i: (i, 0))
    pos = lambda w: pl.BlockSpec((tm, w), lambda i: (i % nS, 0))
    outs = [(T, MLA_HEADS * LANES, BF16), (T, MLA_HEADS * LANES, BF16), (T, MLA_HEADS * LANES, BF16),
            (T, 512, BF16), (T, 512, BF16), (T, 512, BF16), (T, 512, F32)]
    return pl.pallas_call(
        _in_proj_kernel,
        grid=(T // tm,),
        in_specs=[tok(D), _const_spec(g_mix.shape), _const_spec(w_in_ext.shape), _const_spec(g_q.shape),
                  _const_spec(w_uq_pad.shape), _const_spec(g_kv.shape), _const_spec(w_k.shape),
                  _const_spec(w_v.shape), pos(LANES), pos(LANES), pos(LANES), pos(LANES)],
        out_specs=[tok(w) for (_, w, _) in outs],
        out_shape=[jax.ShapeDtypeStruct((t, w), d) for (t, w, d) in outs],
        compiler_params=_params(("parallel",)),
        name="in_proj",
    )(x2, g_mix, w_in_ext, g_q, w_uq_pad, g_kv, w_k, w_v, cosm, sinm, cosr, sinr)


def _mla_attn_kernel(q_ref, k_ref, v_ref, o_ref, *, t):
    i = pl.program_id(2)
    causal = (lax.broadcasted_iota(jnp.int32, (t, t), 1) <= lax.broadcasted_iota(jnp.int32, (t, t), 0))

    def step(j, carry, masked):
        k0 = pl.multiple_of(j * t, t)
        new = []
        for hh in range(2):
            m, acc = carry[hh]
            hs = slice(hh * LANES, (hh + 1) * LANES)
            s = _dot_nt(q_ref[0, :, hs], k_ref[0, pl.ds(k0, t), hs])
            if masked:
                s = jnp.where(causal, s, NEG)
            m_new = jnp.maximum(m, jnp.max(s, axis=-1, keepdims=True))
            p = jnp.exp(s - m_new).astype(BF16)
            acc = jnp.exp(m - m_new) * acc + _dot(p, v_ref[0, pl.ds(k0, t), hs])
            new.append((m_new, acc))
        return tuple(new)

    init = tuple((jnp.full((t, 1), NEG, F32), jnp.zeros((t, LANES), F32)) for _ in range(2))
    carry = lax.fori_loop(0, i, functools.partial(step, masked=False), init)
    carry = step(i, carry, True)
    outs = [acc[:, :MLA_V] / acc[:, MLA_V:MLA_V + 1] for (_, acc) in carry]
    o_ref[0] = jnp.concatenate(outs, axis=-1).astype(o_ref.dtype)


def _mla_attn(q, k, v, t):
    B, S, _ = q.shape
    return pl.pallas_call(
        functools.partial(_mla_attn_kernel, t=t),
        grid=(B, MLA_HEADS // 2, S // t),
        in_specs=[pl.BlockSpec((1, t, 2 * LANES), lambda b, h, i: (b, i, h)),
                  pl.BlockSpec((1, S, 2 * LANES), lambda b, h, i: (b, 0, h)),
                  pl.BlockSpec((1, S, 2 * LANES), lambda b, h, i: (b, 0, h))],
        out_specs=pl.BlockSpec((1, t, 2 * MLA_V), lambda b, h, i: (b, i, h)),
        out_shape=jax.ShapeDtypeStruct((B, S, MLA_HEADS * MLA_V), BF16),
        compiler_params=_params(("parallel", "parallel", "arbitrary")),
        name="mla_attn",
    )(q, k, v)


def _retention_kernel(rq_ref, rk_ref, rv_ref, rg_ref, gn_ref, decay_ref, zeta_ref, xi_ref, cd_ref,
                      o_ref, r_ref, *, n_chunks):
    C = RET_CHUNK

    @pl.when(pl.program_id(1) == 0)
    def _():
        r_ref[...] = jnp.zeros(r_ref.shape, F32)

    def chunk(n, carry):
        r0 = pl.multiple_of(n * C, C)
        for h in range(RET_HEADS):
            hs = slice(h * LANES, (h + 1) * LANES)
            qc = rq_ref[0, pl.ds(r0, C), hs]
            kc = rk_ref[0, pl.ds(r0, C), hs]
            vc = rv_ref[0, pl.ds(r0, C), hs]
            state = r_ref[h]
            sc = _dot_nt(qc, kc) * decay_ref[h]
            inner = _dot(sc.astype(BF16), vc)
            cross = _dot(qc, state.astype(BF16)) * xi_ref[h]
            o = inner + cross
            kz = (kc.astype(F32) * zeta_ref[h]).T.astype(BF16)
            r_ref[h] = state * cd_ref[h] + _dot(kz, vc)
            mu = jnp.mean(o, axis=-1, keepdims=True)
            d = o - mu
            var = jnp.mean(d * d, axis=-1, keepdims=True)
            on = d * lax.rsqrt(var + EPS) * gn_ref[:, hs]
            g = rg_ref[0, pl.ds(r0, C), hs]
            o_ref[0, pl.ds(r0, C), hs] = (g * jax.nn.sigmoid(g) * on).astype(o_ref.dtype)
        return carry

    lax.fori_loop(0, n_chunks, chunk, 0)


def _retention(rq, rk, rv, rg, g_gn, decay, zeta, xi, cd, ts):
    B, S, W = rq.shape
    tok = pl.BlockSpec((1, ts, W), lambda b, i: (b, i, 0))
    return pl.pallas_call(
        functools.partial(_retention_kernel, n_chunks=ts // RET_CHUNK),
        grid=(B, S // ts),
        in_specs=[tok, tok, tok, tok, _const_spec(g_gn.shape), _const_spec(decay.shape),
                  _const_spec(zeta.shape), _const_spec(xi.shape), _const_spec(cd.shape)],
        out_specs=tok,
        out_shape=jax.ShapeDtypeStruct((B, S, W), BF16),
        scratch_shapes=[pltpu.VMEM((RET_HEADS, RET_DK, RET_DV), F32)],
        compiler_params=_params(("parallel", "arbitrary")),
        name="retention",
    )(rq, rk, rv, rg, g_gn, decay, zeta, xi, cd)


def _mem_kv_kernel(mem_ref, g_ref, w_ref, k_ref, v_ref):
    D = mem_ref.shape[-1]
    mn = _rms(mem_ref[0], g_ref[...]).astype(BF16)
    kv = _dot(mn, w_ref[...])
    k_ref[0] = kv[:, :D].astype(BF16)
    v_ref[0] = kv[:, D:].astype(BF16)


def _mem_kv(mem, g_mem, w_xkv):
    B, M, D = mem.shape
    blk = pl.BlockSpec((1, M, D), lambda b: (b, 0, 0))
    return pl.pallas_call(
        _mem_kv_kernel,
        grid=(B,),
        in_specs=[blk, _const_spec(g_mem.shape), _const_spec(w_xkv.shape)],
        out_specs=[blk, blk],
        out_shape=[jax.ShapeDtypeStruct((B, M, D), BF16)] * 2,
        compiler_params=_params(("parallel",)),
        name="mem_kv",
    )(mem, g_mem, w_xkv)


def _post_mix_kernel(x_ref, omla_ref, oret_ref, woa_ref, wob_ref, gx_ref, wxq_ref, kx_ref, vx_ref, wxo_ref,
                     gf_ref, wpq_ref, keys_ref, h2_ref, a3t_ref, st_ref, ox_ref):
    D = x_ref.shape[-1]
    h1 = x_ref[...] + _dot(omla_ref[...], woa_ref[...]) + _dot(oret_ref[...], wob_ref[...])
    a2 = _rms(h1, gx_ref[...]).astype(BF16)
    qx = _dot(a2, wxq_ref[...])
    hd = D // X_HEADS
    for h in range(X_HEADS):
        sl = slice(h * hd, (h + 1) * hd)
        s = _dot_nt(qx[:, sl].astype(BF16), kx_ref[0, :, sl]) * (hd ** -0.5)
        m = jnp.max(s, axis=-1, keepdims=True)
        p = jnp.exp(s - m)
        p = p / jnp.sum(p, axis=-1, keepdims=True)
        ox_ref[:, sl] = _dot(p.astype(BF16), vx_ref[0, :, sl]).astype(BF16)
    h2 = h1 + _dot(ox_ref[...], wxo_ref[...])
    h2_ref[...] = h2
    a3 = _rms(h2, gf_ref[...])
    a3t_ref[...] = a3.T.astype(BF16)
    qp = _dot(a3.astype(BF16), wpq_ref[...]).astype(BF16)
    half = PEER_DKEY // 2
    for hp in range(2 * PEER_HEADS):
        sc = _dot_nt(keys_ref[hp % 2], qp[:, hp * half:(hp + 1) * half])
        for tl in range(sc.shape[1] // LANES):
            st_ref[hp, tl] = sc[:, tl * LANES:(tl + 1) * LANES]


def _post_mix(x2, o_mla, o_ret, w_oa, w_ob, g_x, w_xq, kx, vx, w_xo, g_f, w_pq, keys, seq, tm):
    T, D = x2.shape
    nS = seq // tm
    tok = lambda w: pl.BlockSpec((tm, w), lambda i: (i, 0))
    memb = pl.BlockSpec((1,) + kx.shape[1:], lambda i: (i // nS, 0, 0))
    return pl.pallas_call(
        _post_mix_kernel,
        grid=(T // tm,),
        in_specs=[tok(D), tok(o_mla.shape[1]), tok(o_ret.shape[1]), _const_spec(w_oa.shape),
                  _const_spec(w_ob.shape), _const_spec(g_x.shape), _const_spec(w_xq.shape), memb, memb,
                  _const_spec(w_xo.shape), _const_spec(g_f.shape), _const_spec(w_pq.shape),
                  _const_spec(keys.shape)],
        out_specs=[tok(D), pl.BlockSpec((D, tm), lambda i: (0, i)),
                   pl.BlockSpec((2 * PEER_HEADS, tm // LANES, PEER_KEYS, LANES), lambda i: (0, i, 0, 0))],
        out_shape=[jax.ShapeDtypeStruct((T, D), F32), jax.ShapeDtypeStruct((D, T), BF16),
                   jax.ShapeDtypeStruct((2 * PEER_HEADS, T // LANES, PEER_KEYS, LANES), F32)],
        scratch_shapes=[pltpu.VMEM((tm, D), BF16)],
        compiler_params=_params(("parallel",)),
        name="post_mix",
    )(x2, o_mla, o_ret, w_oa, w_ob, g_x, w_xq, kx, vx, w_xo, g_f, w_pq, keys)


N_TOP = PEER_TOPK + 1
TOP_ROWS = 24


def _peer_route_kernel(st_ref, r2_ref, e2_ref, n_ref, c1_ref, vs_ref):
    tl = st_ref.shape[-1]
    vs_ref[...] = jnp.full(vs_ref.shape, NEG, F32)
    for h in range(PEER_HEADS):
        for p in range(2):
            cur = st_ref[2 * h + p, 0]
            rank = jnp.full(cur.shape, float(PEER_TOPK), F32)
            for it in range(N_TOP):
                m = jnp.max(cur, axis=0, keepdims=True)
                vs_ref[p, it:it + 1, :] = m
                if it + 1 < N_TOP:
                    hit = cur == m
                    if p == 1:
                        rank = jnp.where(hit, float(it), rank)
                    cur = jnp.where(hit, NEG, cur)
            if p == 1:
                r2_ref[h, 0] = rank.astype(BF16)
        v1 = lambda i: vs_ref[0, i:i + 1, :]
        v2 = lambda i: vs_ref[1, i:i + 1, :]
        cands = [v1(0) + vs_ref[1, 8 * r:8 * r + 8, :] for r in range(TOP_ROWS // 8)]
        cands += [v1(i) + vs_ref[1, 0:8, :] for i in range(1, 8)]
        cands += [vs_ref[0, 8 * r:8 * r + 8, :] + v2(0) for r in range(1, TOP_ROWS // 8)]
        c = jnp.concatenate(cands, axis=0)
        top = v1(0) + v2(0)
        z = jnp.zeros((1, tl), F32)
        c16 = top
        c17 = top
        for it in range(N_TOP):
            m = jnp.max(c, axis=0, keepdims=True)
            if it < PEER_TOPK:
                z = z + jnp.exp(m - top)
                c16 = m
                c = jnp.where(c == m, NEG, c)
            else:
                c17 = m
        s1 = st_ref[2 * h, 0]
        thr = 0.5 * (c16 + c17) - s1
        ge = lambda row: row >= thr
        pick = lambda cond, hi, lo: jnp.where(cond, hi, lo)
        b8 = ge(v2(7))
        b4 = ge(pick(b8, v2(11), v2(3)))
        b2 = ge(pick(b4, pick(b8, v2(13), v2(5)), pick(b8, v2(9), v2(1))))
        b1 = ge(pick(b2, pick(b4, pick(b8, v2(14), v2(6)), pick(b8, v2(10), v2(2))),
                     pick(b4, pick(b8, v2(12), v2(4)), pick(b8, v2(8), v2(0)))))
        one = lambda cond, val: jnp.where(cond, val, 0.0)
        n = one(b8, 8.0) + one(b4, 4.0) + one(b2, 2.0) + one(b1, 1.0) + one(ge(v2(15)), 1.0)
        n_ref[h, 0] = n.astype(BF16)
        c1_ref[h, 0] = jnp.exp(s1 - v1(0)) * (1.0 / z)
        e2_ref[h, 0] = jnp.exp(st_ref[2 * h + 1, 0] - v2(0)).astype(BF16)


def _peer_route(st):
    HP, nb, K, tl = st.shape
    blk = pl.BlockSpec((PEER_HEADS, 1, K, tl), lambda i: (0, i, 0, 0))
    tab = lambda dt: jax.ShapeDtypeStruct((PEER_HEADS, nb, K, tl), dt)
    return pl.pallas_call(
        _peer_route_kernel,
        grid=(nb,),
        in_specs=[pl.BlockSpec((HP, 1, K, tl), lambda i: (0, i, 0, 0))],
        out_specs=[blk, blk, blk, blk],
        out_shape=[tab(BF16), tab(BF16), tab(BF16), tab(F32)],
        scratch_shapes=[pltpu.VMEM((2, TOP_ROWS, tl), F32)],
        compiler_params=_params(("parallel",)),
        name="peer_route",
    )(st)


N_SPLIT = 4


def _peer_dense_kernel(a3t_ref, r2_ref, e2_ref, n_ref, c1_ref, *rest, te, tm):
    u_refs, v_refs = rest[:N_SPLIT], rest[N_SPLIT:2 * N_SPLIT]
    h2_ref, g_ref, o_ref, nrow_ref, crow_ref, acc_ref, r2s_ref, e2s_ref = rest[2 * N_SPLIT:]
    j = pl.program_id(1)
    nk = PEER_KEYS
    n_al = te // nk
    nt = tm // LANES

    @pl.when(j == 0)
    def _():
        acc_ref[...] = jnp.zeros(acc_ref.shape, F32)
        for h in range(PEER_HEADS):
            for tl in range(nt):
                nrow_ref[h, :, tl, :] = n_ref[h, tl].astype(F32)
                crow_ref[h, :, tl, :] = c1_ref[h, tl]
                r2s_ref[h, tl] = r2_ref[h, tl]
                e2s_ref[h, tl] = e2_ref[h, tl]

    act = _dot(jnp.concatenate([r[...] for r in u_refs], axis=0), a3t_ref[...])
    zero = jnp.zeros((), BF16)
    p_rows = []
    for al in range(n_al):
        a = j * n_al + al
        rs = slice(al * nk, (al + 1) * nk)
        tiles = []
        for tl in range(nt):
            w = jnp.zeros((nk, LANES), BF16)
            for h in range(PEER_HEADS):
                nb = jnp.broadcast_to(nrow_ref[h, a, tl:tl + 1, :], (nk, LANES)).astype(BF16)
                cb = jnp.broadcast_to(crow_ref[h, a, tl:tl + 1, :], (nk, LANES)).astype(BF16)
                w = w + jnp.where(r2s_ref[h, tl] < nb, e2s_ref[h, tl], zero) * cb
            tiles.append(_gelu(act[rs, tl * LANES:(tl + 1) * LANES]).astype(BF16) * w)
        p_rows.append(jnp.concatenate(tiles, axis=1))
    p = jnp.concatenate(p_rows, axis=0)
    v = jnp.concatenate([r[...] for r in v_refs], axis=0)
    acc_ref[...] += lax.dot_general(v, p, (((0,), (0,)), ((), ())), preferred_element_type=F32)

    @pl.when(j == pl.num_programs(1) - 1)
    def _():
        o_ref[...] = _rms(h2_ref[...] + acc_ref[...].T, g_ref[...])


def _peer_dense(a3t, r2, e2, n, c1, u_bf, v_bf, h2, g, tm, te):
    D, T = a3t.shape
    nt = tm // LANES
    tab = pl.BlockSpec((PEER_HEADS, nt) + r2.shape[2:], lambda i, j: (0, i, 0, 0))
    chunks = [pl.BlockSpec((te // N_SPLIT, D), lambda i, j, c=c: (N_SPLIT * j + c, 0)) for c in range(N_SPLIT)]
    return pl.pallas_call(
        functools.partial(_peer_dense_kernel, te=te, tm=tm),
        grid=(T // tm, u_bf.shape[0] // te),
        in_specs=[pl.BlockSpec((D, tm), lambda i, j: (0, i)), tab, tab, tab, tab,
                  *chunks, *chunks,
                  pl.BlockSpec((tm, D), lambda i, j: (i, 0)), _const_spec(g.shape)],
        out_specs=pl.BlockSpec((tm, D), lambda i, j: (i, 0)),
        out_shape=jax.ShapeDtypeStruct((T, D), F32),
        scratch_shapes=[pltpu.VMEM((PEER_HEADS, PEER_KEYS, nt, LANES), F32)] * 2
        + [pltpu.VMEM((D, tm), F32)] + [pltpu.VMEM((PEER_HEADS, nt, PEER_KEYS, LANES), BF16)] * 2,
        compiler_params=_params(("parallel", "arbitrary")),
        name="peer_dense",
    )(a3t, r2, e2, n, c1, *([u_bf] * N_SPLIT), *([v_bf] * N_SPLIT), h2, g)


def _rope_tables(seq):
    pos = jnp.arange(seq, dtype=F32)

    def tab(dim):
        inv = 1.0 / (ROPE_BASE ** (jnp.arange(0, dim, 2, dtype=F32) / dim))
        ang = pos[:, None] * inv[None, :]
        return jnp.cos(ang), jnp.sin(ang)

    cm, sm = tab(MLA_ROPE)
    pad = LANES - MLA_NOPE - MLA_ROPE
    cosm = jnp.concatenate([jnp.ones((seq, MLA_NOPE), F32), cm, cm, jnp.zeros((seq, pad), F32)], axis=1)
    sinm = jnp.concatenate([jnp.zeros((seq, MLA_NOPE), F32), -sm, sm, jnp.zeros((seq, pad), F32)], axis=1)
    cr, sr = tab(RET_DK)
    cosr = jnp.concatenate([cr, cr], axis=1)
    sinr = jnp.concatenate([-sr, sr], axis=1)
    return cosm, sinm, cosr, sinr


def _retention_tables():
    C = RET_CHUNK
    log_g = jnp.log(1.0 - 2.0 ** (-5.0 - jnp.arange(RET_HEADS, dtype=F32)))
    idx = jnp.arange(C, dtype=F32)
    rel = idx[:, None] - idx[None, :]
    decay = jnp.where(rel[None] >= 0, jnp.exp(jnp.maximum(rel, 0.0)[None] * log_g[:, None, None]), 0.0)
    zeta = jnp.exp((C - 1 - idx)[None, :] * log_g[:, None])
    xi = jnp.exp((idx + 1)[None, :] * log_g[:, None])
    cd = jnp.exp(C * log_g)
    bc = lambda t: jnp.broadcast_to(t[:, :, None], (RET_HEADS, C, LANES))
    return decay, bc(zeta), bc(xi), jnp.broadcast_to(cd[:, None, None], (RET_HEADS, C, LANES))


def _layer(h, mem, g_mix, w_in, g_q_lora, w_uq, g_kv_lora, w_ukv, g_ret_gn, w_o, g_xattn, g_mem,
           w_xq, w_xkv, w_xo, g_ffn, w_pq, sub_keys, u_experts, v_experts, g_out):
    B, S, D = h.shape
    T = B * S
    row = lambda g: g.reshape(1, -1)

    o_pe = MLA_Q_RANK + MLA_KV_RANK
    pe_pad = jnp.zeros((D, LANES), F32).at[:, MLA_NOPE:MLA_NOPE + MLA_ROPE].set(w_in[:, o_pe:o_pe + MLA_ROPE])
    w_in_ext = jnp.concatenate([w_in[:, :o_pe], pe_pad, w_in[:, o_pe + MLA_ROPE:]], axis=1).astype(BF16)
    dq = MLA_NOPE + MLA_ROPE
    w_uq_pad = jnp.pad(w_uq.reshape(MLA_Q_RANK, MLA_HEADS, dq), ((0, 0), (0, 0), (0, LANES - dq)))
    w_uq_pad = w_uq_pad.reshape(MLA_Q_RANK, MLA_HEADS * LANES).astype(BF16)
    w_ukv3 = w_ukv.reshape(MLA_KV_RANK, MLA_HEADS, MLA_NOPE + MLA_V)
    w_k = jnp.pad(w_ukv3[:, :, :MLA_NOPE], ((0, 0), (0, 0), (0, LANES - MLA_NOPE)))
    w_k = w_k.reshape(MLA_KV_RANK, MLA_HEADS * LANES).astype(BF16)
    w_v = jnp.pad(w_ukv3[:, :, MLA_NOPE:], ((0, 0), (0, 0), (0, LANES - MLA_V)))
    w_v = w_v.reshape(MLA_KV_RANK, MLA_HEADS * LANES).astype(BF16)

    cosm, sinm, cosr, sinr = _rope_tables(S)
    q, k, v, rq, rk, rv, rg = _in_proj(h.reshape(T, D), row(g_mix), w_in_ext, row(g_q_lora), w_uq_pad,
                                       row(g_kv_lora), w_k, w_v, cosm, sinm, cosr, sinr, S, min(256, S))
    r3 = lambda t: t.reshape(B, S, t.shape[-1])
    o_mla = _mla_attn(r3(q), r3(k), r3(v), min(512, S))
    o_ret = _retention(r3(rq), r3(rk), r3(rv), r3(rg), row(g_ret_gn), *_retention_tables(), min(1024, S))

    kx, vx = _mem_kv(mem, row(g_mem), w_xkv.astype(BF16))
    n_mla = MLA_HEADS * MLA_V
    w_o_bf = w_o.astype(BF16)
    h2, a3t, st = _post_mix(h.reshape(T, D), o_mla.reshape(T, -1), o_ret.reshape(T, -1), w_o_bf[:n_mla],
                            w_o_bf[n_mla:], row(g_xattn), w_xq.astype(BF16), kx, vx, w_xo.astype(BF16),
                            row(g_ffn), w_pq.astype(BF16), sub_keys.astype(BF16), S, min(256, S))
    r2, e2, n, c1 = _peer_route(st)
    out = _peer_dense(a3t, r2, e2, n, c1, u_experts.astype(BF16), v_experts.astype(BF16), h2, row(g_out),
                      min(512, T), 1024)
    return out.reshape(B, S, D)


def kernel(x, mem, g_mix, w_in, g_q_lora, w_uq, g_kv_lora, w_ukv, g_ret_gn, w_o, g_xattn, g_mem, w_xq, w_xkv,
           w_xo, g_ffn, w_pq, sub_keys, u_experts, v_experts, g_final):
    depth = g_mix.shape[0]
    assert depth == 1, "the final norm is fused into the single layer's last kernel"
    l = 0
    return _layer(x, mem, g_mix[l], w_in[l], g_q_lora[l], w_uq[l], g_kv_lora[l], w_ukv[l], g_ret_gn[l], w_o[l],
                  g_xattn[l], g_mem[l], w_xq[l], w_xkv[l], w_xo[l], g_ffn[l], w_pq[l], sub_keys[l],
                  u_experts[l], v_experts[l], g_final)
```

```python
import functools
import math

import jax
import jax.numpy as jnp
from jax import lax
from jax.experimental import pallas as pl
from jax.experimental.pallas import tpu as pltpu

F32 = jnp.float32
BF16 = jnp.bfloat16

LANES = 128
EPS = 1e-6
NEG = -1e30
ROPE_BASE = 10000.0
VMEM_LIMIT = 56 * 1024 * 1024

MLA_HEADS, MLA_Q_RANK, MLA_KV_RANK = 8, 384, 256
MLA_NOPE, MLA_ROPE, MLA_V = 64, 32, 64
RET_HEADS, RET_DK, RET_DV, RET_CHUNK = 4, 128, 128, 128
X_HEADS = 4
PEER_KEYS, PEER_HEADS, PEER_TOPK, PEER_DKEY = 128, 8, 16, 256


def _params(sem, vmem=VMEM_LIMIT):
    return pltpu.CompilerParams(dimension_semantics=sem, vmem_limit_bytes=vmem)


def _rms(x, g):
    return x * lax.rsqrt(jnp.mean(x * x, axis=-1, keepdims=True) + EPS) * g


def _dot(a, b):
    return jnp.dot(a, b, preferred_element_type=F32)


def _dot_nt(a, b):
    return lax.dot_general(a, b, (((1,), (1,)), ((), ())), preferred_element_type=F32)


def _gelu(x):
    return 0.5 * x * (1.0 + lax.erf(x * (2.0 ** -0.5)))


def _const_spec(shape):
    nd = len(shape)
    return pl.BlockSpec(shape, lambda *_: (0,) * nd)


def _in_proj_kernel(x_ref, gmix_ref, win_ref, gq_ref, wuq_ref, gkv_ref, wk_ref, wv_ref,
                    cosm_ref, sinm_ref, cosr_ref, sinr_ref,
                    q_ref, k_ref, v_ref, rq_ref, rk_ref, rv_ref, rg_ref):
    x = x_ref[...]
    a = _rms(x, gmix_ref[...]).astype(BF16)
    proj = _dot(a, win_ref[...])
    o_cq, o_ckv, o_pe, o_rq = 0, MLA_Q_RANK, MLA_Q_RANK + MLA_KV_RANK, MLA_Q_RANK + MLA_KV_RANK + LANES
    rw = RET_HEADS * RET_DK
    cq = proj[:, o_cq:o_ckv]
    ckv = proj[:, o_ckv:o_pe]
    kpe = proj[:, o_pe:o_rq]
    cqn = _rms(cq, gq_ref[...]).astype(BF16)
    ckvn = _rms(ckv, gkv_ref[...]).astype(BF16)
    q = _dot(cqn, wuq_ref[...])
    kn = _dot(ckvn, wk_ref[...])
    vlane = lax.broadcasted_iota(jnp.int32, (x.shape[0], MLA_HEADS * LANES), 1) % LANES
    v_ref[...] = jnp.where(vlane == MLA_V, 1.0, _dot(ckvn, wv_ref[...])).astype(BF16)

    cosm, sinm = cosm_ref[...], sinm_ref[...]
    lane = lax.broadcasted_iota(jnp.int32, (x.shape[0], LANES), 1)
    first_half = lane < MLA_NOPE + MLA_ROPE // 2

    def rope_m(c):
        rot = jnp.where(first_half, pltpu.roll(c, LANES - MLA_ROPE // 2, axis=1),
                        pltpu.roll(c, MLA_ROPE // 2, axis=1))
        return c * cosm + rot * sinm

    qscale = (MLA_NOPE + MLA_ROPE) ** -0.5 * math.log2(math.e)
    kpe_r = rope_m(kpe)
    for h in range(MLA_HEADS):
        sl = slice(h * LANES, (h + 1) * LANES)
        q_ref[:, sl] = (rope_m(q[:, sl]) * qscale).astype(BF16)
        k_ref[:, sl] = (kn[:, sl] + kpe_r).astype(BF16)

    cosr, sinr = cosr_ref[...], sinr_ref[...]
    kscale = RET_DK ** -0.5
    for h in range(RET_HEADS):
        sl = slice(h * LANES, (h + 1) * LANES)
        c = proj[:, o_rq + h * LANES:o_rq + (h + 1) * LANES]
        rq_ref[:, sl] = (c * cosr + pltpu.roll(c, RET_DK // 2, axis=1) * sinr).astype(BF16)
        c = proj[:, o_rq + rw + h * LANES:o_rq + rw + (h + 1) * LANES]
        rk_ref[:, sl] = ((c * cosr + pltpu.roll(c, RET_DK // 2, axis=1) * sinr) * kscale).astype(BF16)
    rv_ref[...] = proj[:, o_rq + 2 * rw:o_rq + 3 * rw].astype(BF16)
    rg_ref[...] = proj[:, o_rq + 3 * rw:o_rq + 4 * rw]


def _in_proj(x2, g_mix, w_in_ext, g_q, w_uq_pad, g_kv, w_k, w_v, cosm, sinm, cosr, sinr, seq, tm):
    T, D = x2.shape
    nS = seq // tm
    tok = lambda w: pl.BlockSpec((tm, w), lambda i: (i, 0))
    pos = lambda w: pl.BlockSpec((tm, w), lambda i: (i % nS, 0))
    outs = [(T, MLA_HEADS * LANES, BF16), (T, MLA_HEADS * LANES, BF16), (T, MLA_HEADS * LANES, BF16),
            (T, 512, BF16), (T, 512, BF16), (T, 512, BF16), (T, 512, F32)]
    return pl.pallas_call(
        _in_proj_kernel,
        grid=(T // tm,),
        in_specs=[tok(D), _const_spec(g_mix.shape), _const_spec(w_in_ext.shape), _const_spec(g_q.shape),
                  _const_spec(w_uq_pad.shape), _const_spec(g_kv.shape), _const_spec(w_k.shape),
                  _const_spec(w_v.shape), pos(LANES), pos(LANES), pos(LANES), pos(LANES)],
        out_specs=[tok(w) for (_, w, _) in outs],
        out_shape=[jax.ShapeDtypeStruct((t, w), d) for (t, w, d) in outs],
        compiler_params=_params(("parallel",)),
        name="in_proj",
    )(x2, g_mix, w_in_ext, g_q, w_uq_pad, g_kv, w_k, w_v, cosm, sinm, cosr, sinr)


def _mla_attn_kernel(q_ref, k_ref, v_ref, o_ref, *, t):
    i = pl.program_id(2)
    causal = (lax.broadcasted_iota(jnp.int32, (t, t), 1) <= lax.broadcasted_iota(jnp.int32, (t, t), 0))

    def step(j, carry, masked):
        k0 = pl.multiple_of(j * t, t)
        new = []
        for hh in range(2):
            m, acc = carry[hh]
            hs = slice(hh * LANES, (hh + 1) * LANES)
            s = _dot_nt(q_ref[0, :, hs], k_ref[0, pl.ds(k0, t), hs])
            if masked:
                s = jnp.where(causal, s, NEG)
            m_new = jnp.maximum(m, jnp.max(s, axis=-1, keepdims=True))
            p = jnp.exp2(s - m_new).astype(BF16)
            acc = jnp.exp2(m - m_new) * acc + _dot(p, v_ref[0, pl.ds(k0, t), hs])
            new.append((m_new, acc))
        return tuple(new)

    init = tuple((jnp.full((t, 1), NEG, F32), jnp.zeros((t, LANES), F32)) for _ in range(2))

    def pair(pi, carry):
        return step(2 * pi + 1, step(2 * pi, carry, False), False)

    carry = lax.fori_loop(0, i // 2, pair, init)
    carry = lax.cond(i % 2 == 1, lambda c: step(i - 1, c, False), lambda c: c, carry)
    carry = step(i, carry, True)
    outs = [acc[:, :MLA_V] / acc[:, MLA_V:MLA_V + 1] for (_, acc) in carry]
    o_ref[0] = jnp.concatenate(outs, axis=-1).astype(o_ref.dtype)


def _mla_attn(q, k, v, t):
    B, S, _ = q.shape
    return pl.pallas_call(
        functools.partial(_mla_attn_kernel, t=t),
        grid=(B, MLA_HEADS // 2, S // t),
        in_specs=[pl.BlockSpec((1, t, 2 * LANES), lambda b, h, i: (b, i, h)),
                  pl.BlockSpec((1, S, 2 * LANES), lambda b, h, i: (b, 0, h)),
                  pl.BlockSpec((1, S, 2 * LANES), lambda b, h, i: (b, 0, h))],
        out_specs=pl.BlockSpec((1, t, 2 * MLA_V), lambda b, h, i: (b, i, h)),
        out_shape=jax.ShapeDtypeStruct((B, S, MLA_HEADS * MLA_V), BF16),
        compiler_params=_params(("parallel", "parallel", "arbitrary")),
        name="mla_attn",
    )(q, k, v)


def _retention_kernel(rq_ref, rk_ref, rv_ref, rg_ref, gn_ref, decay_ref, zeta_ref, xi_ref, cd_ref,
                      o_ref, r_ref, *, n_chunks):
    C = RET_CHUNK

    @pl.when(pl.program_id(1) == 0)
    def _():
        r_ref[...] = jnp.zeros(r_ref.shape, F32)

    def chunk(n, carry):
        r0 = pl.multiple_of(n * C, C)
        for h in range(RET_HEADS):
            hs = slice(h * LANES, (h + 1) * LANES)
            qc = rq_ref[0, pl.ds(r0, C), hs]
            kc = rk_ref[0, pl.ds(r0, C), hs]
            vc = rv_ref[0, pl.ds(r0, C), hs]
            state = r_ref[h]
            sc = _dot_nt(qc, kc) * decay_ref[h]
            inner = _dot(sc.astype(BF16), vc)
            cross = _dot(qc, state.astype(BF16)) * xi_ref[h]
            o = inner + cross
            kz = (kc.astype(F32) * zeta_ref[h]).T.astype(BF16)
            r_ref[h] = state * cd_ref[h] + _dot(kz, vc)
            mu = jnp.mean(o, axis=-1, keepdims=True)
            d = o - mu
            var = jnp.mean(d * d, axis=-1, keepdims=True)
            on = d * lax.rsqrt(var + EPS) * gn_ref[:, hs]
            g = rg_ref[0, pl.ds(r0, C), hs]
            o_ref[0, pl.ds(r0, C), hs] = (g * jax.nn.sigmoid(g) * on).astype(o_ref.dtype)
        return carry

    lax.fori_loop(0, n_chunks, chunk, 0)


def _retention(rq, rk, rv, rg, g_gn, decay, zeta, xi, cd, ts):
    B, S, W = rq.shape
    tok = pl.BlockSpec((1, ts, W), lambda b, i: (b, i, 0))
    return pl.pallas_call(
        functools.partial(_retention_kernel, n_chunks=ts // RET_CHUNK),
        grid=(B, S // ts),
        in_specs=[tok, tok, tok, tok, _const_spec(g_gn.shape), _const_spec(decay.shape),
                  _const_spec(zeta.shape), _const_spec(xi.shape), _const_spec(cd.shape)],
        out_specs=tok,
        out_shape=jax.ShapeDtypeStruct((B, S, W), BF16),
        scratch_shapes=[pltpu.VMEM((RET_HEADS, RET_DK, RET_DV), F32)],
        compiler_params=_params(("parallel", "arbitrary")),
        name="retention",
    )(rq, rk, rv, rg, g_gn, decay, zeta, xi, cd)


def _mem_kv_kernel(mem_ref, g_ref, w_ref, k_ref, v_ref):
    D = mem_ref.shape[-1]
    mn = _rms(mem_ref[0], g_ref[...]).astype(BF16)
    kv = _dot(mn, w_ref[...])
    k_ref[0] = kv[:, :D].astype(BF16)
    v_ref[0] = kv[:, D:].astype(BF16)


def _mem_kv(mem, g_mem, w_xkv):
    B, M, D = mem.shape
    blk = pl.BlockSpec((1, M, D), lambda b: (b, 0, 0))
    return pl.pallas_call(
        _mem_kv_kernel,
        grid=(B,),
        in_specs=[blk, _const_spec(g_mem.shape), _const_spec(w_xkv.shape)],
        out_specs=[blk, blk],
        out_shape=[jax.ShapeDtypeStruct((B, M, D), BF16)] * 2,
        compiler_params=_params(("parallel",)),
        name="mem_kv",
    )(mem, g_mem, w_xkv)


def _post_mix_kernel(x_ref, omla_ref, oret_ref, woa_ref, wob_ref, gx_ref, wxq_ref, kx_ref, vx_ref, wxo_ref,
                     gf_ref, wpq_ref, keys_ref, h2_ref, a3t_ref, st_ref, ox_ref):
    D = x_ref.shape[-1]
    h1 = x_ref[...] + _dot(omla_ref[...], woa_ref[...]) + _dot(oret_ref[...], wob_ref[...])
    a2 = _rms(h1, gx_ref[...]).astype(BF16)
    qx = _dot(a2, wxq_ref[...])
    hd = D // X_HEADS
    for h in range(X_HEADS):
        sl = slice(h * hd, (h + 1) * hd)
        s = _dot_nt(qx[:, sl].astype(BF16), kx_ref[0, :, sl]) * (hd ** -0.5)
        m = jnp.max(s, axis=-1, keepdims=True)
        p = jnp.exp(s - m)
        p = p / jnp.sum(p, axis=-1, keepdims=True)
        ox_ref[:, sl] = _dot(p.astype(BF16), vx_ref[0, :, sl]).astype(BF16)
    h2 = h1 + _dot(ox_ref[...], wxo_ref[...])
    h2_ref[...] = h2
    a3 = _rms(h2, gf_ref[...])
    a3t_ref[...] = a3.T.astype(BF16)
    qp = _dot(a3.astype(BF16), wpq_ref[...]).astype(BF16)
    half = PEER_DKEY // 2
    for hp in range(2 * PEER_HEADS):
        sc = _dot_nt(keys_ref[hp % 2], qp[:, hp * half:(hp + 1) * half])
        for tl in range(sc.shape[1] // LANES):
            st_ref[hp, tl] = sc[:, tl * LANES:(tl + 1) * LANES]


def _post_mix(x2, o_mla, o_ret, w_oa, w_ob, g_x, w_xq, kx, vx, w_xo, g_f, w_pq, keys, seq, tm):
    T, D = x2.shape
    nS = seq // tm
    tok = lambda w: pl.BlockSpec((tm, w), lambda i: (i, 0))
    memb = pl.BlockSpec((1,) + kx.shape[1:], lambda i: (i // nS, 0, 0))
    return pl.pallas_call(
        _post_mix_kernel,
        grid=(T // tm,),
        in_specs=[tok(D), tok(o_mla.shape[1]), tok(o_ret.shape[1]), _const_spec(w_oa.shape),
                  _const_spec(w_ob.shape), _const_spec(g_x.shape), _const_spec(w_xq.shape), memb, memb,
                  _const_spec(w_xo.shape), _const_spec(g_f.shape), _const_spec(w_pq.shape),
                  _const_spec(keys.shape)],
        out_specs=[tok(D), pl.BlockSpec((D, tm), lambda i: (0, i)),
                   pl.BlockSpec((2 * PEER_HEADS, tm // LANES, PEER_KEYS, LANES), lambda i: (0, i, 0, 0))],
        out_shape=[jax.ShapeDtypeStruct((T, D), F32), jax.ShapeDtypeStruct((D, T), BF16),
                   jax.ShapeDtypeStruct((2 * PEER_HEADS, T // LANES, PEER_KEYS, LANES), F32)],
        scratch_shapes=[pltpu.VMEM((tm, D), BF16)],
        compiler_params=_params(("parallel",)),
        name="post_mix",
    )(x2, o_mla, o_ret, w_oa, w_ob, g_x, w_xq, kx, vx, w_xo, g_f, w_pq, keys)


N_TOP = PEER_TOPK + 1
TOP_ROWS = 24


def _peer_route_kernel(st_ref, stats_ref, vs_ref):
    tl = st_ref.shape[-1]
    vs_ref[...] = jnp.full(vs_ref.shape, NEG, F32)
    for h in range(PEER_HEADS):
        for p in range(2):
            cur = st_ref[2 * h + p, 0]
            for it in range(N_TOP):
                m = jnp.max(cur, axis=0, keepdims=True)
                vs_ref[p, it:it + 1, :] = m
                if it + 1 < N_TOP:
                    cur = jnp.where(cur == m, NEG, cur)
        v1 = lambda i: vs_ref[0, i:i + 1, :]
        v2 = lambda i: vs_ref[1, i:i + 1, :]
        cands = [v1(0) + vs_ref[1, 8 * r:8 * r + 8, :] for r in range(TOP_ROWS // 8)]
        cands += [v1(i) + vs_ref[1, 0:8, :] for i in range(1, 8)]
        cands += [vs_ref[0, 8 * r:8 * r + 8, :] + v2(0) for r in range(1, TOP_ROWS // 8)]
        c = jnp.concatenate(cands, axis=0)
        top = v1(0) + v2(0)
        z = jnp.zeros((1, tl), F32)
        c16 = top
        c17 = top
        for it in range(N_TOP):
            m = jnp.max(c, axis=0, keepdims=True)
            if it < PEER_TOPK:
                z = z + jnp.exp(m - top)
                c16 = m
                c = jnp.where(c == m, NEG, c)
            else:
                c17 = m
        stats_ref[0, h:h + 1, :] = 0.5 * (c16 + c17)
        stats_ref[1, h:h + 1, :] = v1(0)
        stats_ref[2, h:h + 1, :] = v2(0)
        stats_ref[3, h:h + 1, :] = 1.0 / z


def _peer_route(st):
    HP, nb, K, tl = st.shape
    T = nb * tl
    return pl.pallas_call(
        _peer_route_kernel,
        grid=(nb,),
        in_specs=[pl.BlockSpec((HP, 1, K, tl), lambda i: (0, i, 0, 0))],
        out_specs=pl.BlockSpec((4, PEER_HEADS, tl), lambda i: (0, 0, i)),
        out_shape=jax.ShapeDtypeStruct((4, PEER_HEADS, T), F32),
        scratch_shapes=[pltpu.VMEM((2, TOP_ROWS, tl), F32)],
        compiler_params=_params(("parallel",)),
        name="peer_route",
    )(st)


def _peer_dense_kernel(a3t_ref, st_ref, stats_ref, u_ref, v_ref, h2_ref, g_ref, o_ref,
                       thr_ref, c1_ref, e2_ref, acc_ref, *, te, tm):
    j = pl.program_id(1)
    nk = PEER_KEYS
    n_al = te // nk
    nt = tm // LANES

    @pl.when(j == 0)
    def _():
        acc_ref[...] = jnp.zeros(acc_ref.shape, F32)
        for h in range(PEER_HEADS):
            for tl in range(nt):
                ls = slice(tl * LANES, (tl + 1) * LANES)
                s1 = st_ref[2 * h, tl]
                thr_ref[h, :, tl, :] = stats_ref[0, h:h + 1, ls] - s1
                c1_ref[h, :, tl, :] = jnp.exp(s1 - stats_ref[1, h:h + 1, ls]) * stats_ref[3, h:h + 1, ls]
                e2_ref[h, tl] = jnp.exp(st_ref[2 * h + 1, tl] - stats_ref[2, h:h + 1, ls])

    act = _dot(u_ref[...], a3t_ref[...])
    p_rows = []
    for al in range(n_al):
        a = j * n_al + al
        rs = slice(al * nk, (al + 1) * nk)
        tiles = []
        for tl in range(nt):
            w = jnp.zeros((nk, LANES), F32)
            for h in range(PEER_HEADS):
                thr = thr_ref[h, a, tl:tl + 1, :]
                c1 = c1_ref[h, a, tl:tl + 1, :]
                w = w + jnp.where(st_ref[2 * h + 1, tl] >= thr, e2_ref[h, tl] * c1, 0.0)
            tiles.append((_gelu(act[rs, tl * LANES:(tl + 1) * LANES]) * w).astype(BF16))
        p_rows.append(jnp.concatenate(tiles, axis=1))
    p = jnp.concatenate(p_rows, axis=0)
    acc_ref[...] += lax.dot_general(v_ref[...], p, (((0,), (0,)), ((), ())), preferred_element_type=F32)

    @pl.when(j == pl.num_programs(1) - 1)
    def _():
        o_ref[...] = _rms(h2_ref[...] + acc_ref[...].T, g_ref[...])


def _peer_dense(a3t, st, stats, u_bf, v_bf, h2, g, tm, te):
    D, T = a3t.shape
    nt = tm // LANES
    return pl.pallas_call(
        functools.partial(_peer_dense_kernel, te=te, tm=tm),
        grid=(T // tm, u_bf.shape[0] // te),
        in_specs=[pl.BlockSpec((D, tm), lambda i, j: (0, i)),
                  pl.BlockSpec((st.shape[0], nt) + st.shape[2:], lambda i, j: (0, i, 0, 0)),
                  pl.BlockSpec(stats.shape[:2] + (tm,), lambda i, j: (0, 0, i)),
                  pl.BlockSpec((te, D), lambda i, j: (j, 0)),
                  pl.BlockSpec((te, D), lambda i, j: (j, 0)),
                  pl.BlockSpec((tm, D), lambda i, j: (i, 0)), _const_spec(g.shape)],
        out_specs=pl.BlockSpec((tm, D), lambda i, j: (i, 0)),
        out_shape=jax.ShapeDtypeStruct((T, D), F32),
        scratch_shapes=[pltpu.VMEM((PEER_HEADS, PEER_KEYS, nt, LANES), F32)] * 2
        + [pltpu.VMEM((PEER_HEADS, nt, PEER_KEYS, LANES), F32), pltpu.VMEM((D, tm), F32)],
        compiler_params=_params(("parallel", "arbitrary")),
        name="peer_dense",
    )(a3t, st, stats, u_bf, v_bf, h2, g)


def _rope_tables(seq):
    pos = jnp.arange(seq, dtype=F32)

    def tab(dim):
        inv = 1.0 / (ROPE_BASE ** (jnp.arange(0, dim, 2, dtype=F32) / dim))
        ang = pos[:, None] * inv[None, :]
        return jnp.cos(ang), jnp.sin(ang)

    cm, sm = tab(MLA_ROPE)
    pad = LANES - MLA_NOPE - MLA_ROPE
    cosm = jnp.concatenate([jnp.ones((seq, MLA_NOPE), F32), cm, cm, jnp.zeros((seq, pad), F32)], axis=1)
    sinm = jnp.concatenate([jnp.zeros((seq, MLA_NOPE), F32), -sm, sm, jnp.zeros((seq, pad), F32)], axis=1)
    cr, sr = tab(RET_DK)
    cosr = jnp.concatenate([cr, cr], axis=1)
    sinr = jnp.concatenate([-sr, sr], axis=1)
    return cosm, sinm, cosr, sinr


def _retention_tables():
    C = RET_CHUNK
    log_g = jnp.log(1.0 - 2.0 ** (-5.0 - jnp.arange(RET_HEADS, dtype=F32)))
    idx = jnp.arange(C, dtype=F32)
    rel = idx[:, None] - idx[None, :]
    decay = jnp.where(rel[None] >= 0, jnp.exp(jnp.maximum(rel, 0.0)[None] * log_g[:, None, None]), 0.0)
    zeta = jnp.exp((C - 1 - idx)[None, :] * log_g[:, None])
    xi = jnp.exp((idx + 1)[None, :] * log_g[:, None])
    cd = jnp.exp(C * log_g)
    bc = lambda t: jnp.broadcast_to(t[:, :, None], (RET_HEADS, C, LANES))
    return decay, bc(zeta), bc(xi), jnp.broadcast_to(cd[:, None, None], (RET_HEADS, C, LANES))


def _layer(h, mem, g_mix, w_in, g_q_lora, w_uq, g_kv_lora, w_ukv, g_ret_gn, w_o, g_xattn, g_mem,
           w_xq, w_xkv, w_xo, g_ffn, w_pq, sub_keys, u_experts, v_experts, g_out):
    B, S, D = h.shape
    T = B * S
    row = lambda g: g.reshape(1, -1)

    o_pe = MLA_Q_RANK + MLA_KV_RANK
    pe_pad = jnp.zeros((D, LANES), F32).at[:, MLA_NOPE:MLA_NOPE + MLA_ROPE].set(w_in[:, o_pe:o_pe + MLA_ROPE])
    w_in_ext = jnp.concatenate([w_in[:, :o_pe], pe_pad, w_in[:, o_pe + MLA_ROPE:]], axis=1).astype(BF16)
    dq = MLA_NOPE + MLA_ROPE
    w_uq_pad = jnp.pad(w_uq.reshape(MLA_Q_RANK, MLA_HEADS, dq), ((0, 0), (0, 0), (0, LANES - dq)))
    w_uq_pad = w_uq_pad.reshape(MLA_Q_RANK, MLA_HEADS * LANES).astype(BF16)
    w_ukv3 = w_ukv.reshape(MLA_KV_RANK, MLA_HEADS, MLA_NOPE + MLA_V)
    w_k = jnp.pad(w_ukv3[:, :, :MLA_NOPE], ((0, 0), (0, 0), (0, LANES - MLA_NOPE)))
    w_k = w_k.reshape(MLA_KV_RANK, MLA_HEADS * LANES).astype(BF16)
    w_v = jnp.pad(w_ukv3[:, :, MLA_NOPE:], ((0, 0), (0, 0), (0, LANES - MLA_V)))
    w_v = w_v.reshape(MLA_KV_RANK, MLA_HEADS * LANES).astype(BF16)

    cosm, sinm, cosr, sinr = _rope_tables(S)
    q, k, v, rq, rk, rv, rg = _in_proj(h.reshape(T, D), row(g_mix), w_in_ext, row(g_q_lora), w_uq_pad,
                                       row(g_kv_lora), w_k, w_v, cosm, sinm, cosr, sinr, S, min(256, S))
    r3 = lambda t: t.reshape(B, S, t.shape[-1])
    o_mla = _mla_attn(r3(q), r3(k), r3(v), min(512, S))
    o_ret = _retention(r3(rq), r3(rk), r3(rv), r3(rg), row(g_ret_gn), *_retention_tables(), min(1024, S))

    kx, vx = _mem_kv(mem, row(g_mem), w_xkv.astype(BF16))
    n_mla = MLA_HEADS * MLA_V
    w_o_bf = w_o.astype(BF16)
    h2, a3t, st = _post_mix(h.reshape(T, D), o_mla.reshape(T, -1), o_ret.reshape(T, -1), w_o_bf[:n_mla],
                            w_o_bf[n_mla:], row(g_xattn), w_xq.astype(BF16), kx, vx, w_xo.astype(BF16),
                            row(g_ffn), w_pq.astype(BF16), sub_keys.astype(BF16), S, min(256, S))
    stats = _peer_route(st)
    out = _peer_dense(a3t, st, stats, u_experts.astype(BF16), v_experts.astype(BF16), h2, row(g_out),
                      min(512, T), 1024)
    return out.reshape(B, S, D)


def kernel(x, mem, g_mix, w_in, g_q_lora, w_uq, g_kv_lora, w_ukv, g_ret_gn, w_o, g_xattn, g_mem, w_xq, w_xkv,
           w_xo, g_ffn, w_pq, sub_keys, u_experts, v_experts, g_final):
    depth = g_mix.shape[0]
    assert depth == 1, "the final norm is fused into the single layer's last kernel"
    l = 0
    return _layer(x, mem, g_mix[l], w_in[l], g_q_lora[l], w_uq[l], g_kv_lora[l], w_ukv[l], g_ret_gn[l], w_o[l],
                  g_xattn[l], g_mem[l], w_xq[l], w_xkv[l], w_xo[l], g_ffn[l], w_pq[l], sub_keys[l],
                  u_experts[l], v_experts[l], g_final)
```

```python
import functools
import math

import jax
import jax.numpy as jnp
from jax import lax
from jax.experimental import pallas as pl
from jax.experimental.pallas import tpu as pltpu

F32 = jnp.float32
BF16 = jnp.bfloat16

LANES = 128
EPS = 1e-6
NEG = -1e30
ROPE_BASE = 10000.0
VMEM_LIMIT = 56 * 1024 * 1024

MLA_HEADS, MLA_Q_RANK, MLA_KV_RANK = 8, 384, 256
MLA_NOPE, MLA_ROPE, MLA_V = 64, 32, 64
RET_HEADS, RET_DK, RET_DV, RET_CHUNK = 4, 128, 128, 128
X_HEADS = 4
PEER_KEYS, PEER_HEADS, PEER_TOPK, PEER_DKEY = 128, 8, 16, 256


def _params(sem, vmem=VMEM_LIMIT):
    return pltpu.CompilerParams(dimension_semantics=sem, vmem_limit_bytes=vmem)


def _rms(x, g):
    return x * lax.rsqrt(jnp.mean(x * x, axis=-1, keepdims=True) + EPS) * g


def _dot(a, b):
    return jnp.dot(a, b, preferred_element_type=F32)


def _dot_nt(a, b):
    return lax.dot_general(a, b, (((1,), (1,)), ((), ())), preferred_element_type=F32)


def _gelu(x):
    return 0.5 * x * (1.0 + lax.erf(x * (2.0 ** -0.5)))


def _const_spec(shape):
    nd = len(shape)
    return pl.BlockSpec(shape, lambda *_: (0,) * nd)


def _in_proj_kernel(x_ref, gmix_ref, win_ref, gq_ref, wuq_ref, gkv_ref, wk_ref, wv_ref,
                    cosm_ref, sinm_ref, cosr_ref, sinr_ref,
                    q_ref, k_ref, v_ref, rq_ref, rk_ref, rv_ref, rg_ref):
    x = x_ref[...]
    a = _rms(x, gmix_ref[...]).astype(BF16)
    proj = _dot(a, win_ref[...])
    o_cq, o_ckv, o_pe, o_rq = 0, MLA_Q_RANK, MLA_Q_RANK + MLA_KV_RANK, MLA_Q_RANK + MLA_KV_RANK + LANES
    rw = RET_HEADS * RET_DK
    cq = proj[:, o_cq:o_ckv]
    ckv = proj[:, o_ckv:o_pe]
    kpe = proj[:, o_pe:o_rq]
    cqn = _rms(cq, gq_ref[...]).astype(BF16)
    ckvn = _rms(ckv, gkv_ref[...]).astype(BF16)
    q = _dot(cqn, wuq_ref[...])
    kn = _dot(ckvn, wk_ref[...])
    vlane = lax.broadcasted_iota(jnp.int32, (x.shape[0], MLA_HEADS * LANES), 1) % LANES
    v_ref[...] = jnp.where(vlane == MLA_V, 1.0, _dot(ckvn, wv_ref[...])).astype(BF16)

    cosm, sinm = cosm_ref[...], sinm_ref[...]
    lane = lax.broadcasted_iota(jnp.int32, (x.shape[0], LANES), 1)
    first_half = lane < MLA_NOPE + MLA_ROPE // 2

    def rope_m(c):
        rot = jnp.where(first_half, pltpu.roll(c, LANES - MLA_ROPE // 2, axis=1),
                        pltpu.roll(c, MLA_ROPE // 2, axis=1))
        return c * cosm + rot * sinm

    qscale = (MLA_NOPE + MLA_ROPE) ** -0.5 * math.log2(math.e)
    kpe_r = rope_m(kpe)
    for h in range(MLA_HEADS):
        sl = slice(h * LANES, (h + 1) * LANES)
        q_ref[:, sl] = (rope_m(q[:, sl]) * qscale).astype(BF16)
        k_ref[:, sl] = (kn[:, sl] + kpe_r).astype(BF16)

    cosr, sinr = cosr_ref[...], sinr_ref[...]
    kscale = RET_DK ** -0.5
    for h in range(RET_HEADS):
        sl = slice(h * LANES, (h + 1) * LANES)
        c = proj[:, o_rq + h * LANES:o_rq + (h + 1) * LANES]
        rq_ref[:, sl] = (c * cosr + pltpu.roll(c, RET_DK // 2, axis=1) * sinr).astype(BF16)
        c = proj[:, o_rq + rw + h * LANES:o_rq + rw + (h + 1) * LANES]
        rk_ref[:, sl] = ((c * cosr + pltpu.roll(c, RET_DK // 2, axis=1) * sinr) * kscale).astype(BF16)
    rv_ref[...] = proj[:, o_rq + 2 * rw:o_rq + 3 * rw].astype(BF16)
    rg_ref[...] = proj[:, o_rq + 3 * rw:o_rq + 4 * rw]


def _in_proj(x2, g_mix, w_in_ext, g_q, w_uq_pad, g_kv, w_k, w_v, cosm, sinm, cosr, sinr, seq, tm):
    T, D = x2.shape
    nS = seq // tm
    tok = lambda w: pl.BlockSpec((tm, w), lambda i: (i, 0))
    pos = lambda w: pl.BlockSpec((tm, w), lambda i: (i % nS, 0))
    outs = [(T, MLA_HEADS * LANES, BF16), (T, MLA_HEADS * LANES, BF16), (T, MLA_HEADS * LANES, BF16),
            (T, RET_HEADS * RET_DK, BF16), (T, RET_HEADS * RET_DK, BF16), (T, RET_HEADS * RET_DV, BF16),
            (T, RET_HEADS * RET_DV, F32)]
    return pl.pallas_call(
        _in_proj_kernel,
        grid=(T // tm,),
        in_specs=[tok(D), _const_spec(g_mix.shape), _const_spec(w_in_ext.shape), _const_spec(g_q.shape),
                  _const_spec(w_uq_pad.shape), _const_spec(g_kv.shape), _const_spec(w_k.shape),
                  _const_spec(w_v.shape), pos(LANES), pos(LANES), pos(LANES), pos(LANES)],
        out_specs=[tok(w) for (_, w, _) in outs],
        out_shape=[jax.ShapeDtypeStruct((t, w), d) for (t, w, d) in outs],
        compiler_params=_params(("parallel",)),
        name="in_proj",
    )(x2, g_mix, w_in_ext, g_q, w_uq_pad, g_kv, w_k, w_v, cosm, sinm, cosr, sinr)


def _mla_attn_kernel(q_ref, k_ref, v_ref, o_ref, *, t):
    i = pl.program_id(2)
    causal = (lax.broadcasted_iota(jnp.int32, (t, t), 1) <= lax.broadcasted_iota(jnp.int32, (t, t), 0))

    def step(j, carry, masked):
        k0 = pl.multiple_of(j * t, t)
        new = []
        for hh in range(2):
            m, acc = carry[hh]
            hs = slice(hh * LANES, (hh + 1) * LANES)
            s = _dot_nt(q_ref[0, :, hs], k_ref[0, pl.ds(k0, t), hs])
            if masked:
                s = jnp.where(causal, s, NEG)
            m_new = jnp.maximum(m, jnp.max(s, axis=-1, keepdims=True))
            p = jnp.exp2(s - m_new).astype(BF16)
            acc = jnp.exp2(m - m_new) * acc + _dot(p, v_ref[0, pl.ds(k0, t), hs])
            new.append((m_new, acc))
        return tuple(new)

    init = tuple((jnp.full((t, 1), NEG, F32), jnp.zeros((t, LANES), F32)) for _ in range(2))

    def pair(pi, carry):
        return step(2 * pi + 1, step(2 * pi, carry, False), False)

    carry = lax.fori_loop(0, i // 2, pair, init)
    carry = lax.cond(i % 2 == 1, lambda c: step(i - 1, c, False), lambda c: c, carry)
    carry = step(i, carry, True)
    outs = [acc[:, :MLA_V] / acc[:, MLA_V:MLA_V + 1] for (_, acc) in carry]
    o_ref[0] = jnp.concatenate(outs, axis=-1).astype(o_ref.dtype)


def _mla_attn(q, k, v, t):
    B, S, _ = q.shape
    return pl.pallas_call(
        functools.partial(_mla_attn_kernel, t=t),
        grid=(B, MLA_HEADS // 2, S // t),
        in_specs=[pl.BlockSpec((1, t, 2 * LANES), lambda b, h, i: (b, i, h)),
                  pl.BlockSpec((1, S, 2 * LANES), lambda b, h, i: (b, 0, h)),
                  pl.BlockSpec((1, S, 2 * LANES), lambda b, h, i: (b, 0, h))],
        out_specs=pl.BlockSpec((1, t, 2 * MLA_V), lambda b, h, i: (b, i, h)),
        out_shape=jax.ShapeDtypeStruct((B, S, MLA_HEADS * MLA_V), BF16),
        compiler_params=_params(("parallel", "parallel", "arbitrary")),
        name="mla_attn",
    )(q, k, v)


def _retention_kernel(rq_ref, rk_ref, rv_ref, rg_ref, gn_ref, decay_ref, zeta_ref, xi_ref, cd_ref,
                      o_ref, r_ref, *, n_chunks):
    C = RET_CHUNK

    @pl.when(pl.program_id(1) == 0)
    def _():
        r_ref[...] = jnp.zeros(r_ref.shape, F32)

    def chunk(n, carry):
        r0 = pl.multiple_of(n * C, C)
        for h in range(RET_HEADS):
            hs = slice(h * LANES, (h + 1) * LANES)
            qc = rq_ref[0, pl.ds(r0, C), hs]
            kc = rk_ref[0, pl.ds(r0, C), hs]
            vc = rv_ref[0, pl.ds(r0, C), hs]
            state = r_ref[h]
            sc = _dot_nt(qc, kc) * decay_ref[h]
            inner = _dot(sc.astype(BF16), vc)
            cross = _dot(qc, state.astype(BF16)) * xi_ref[h]
            o = inner + cross
            kz = (kc.astype(F32) * zeta_ref[h]).T.astype(BF16)
            r_ref[h] = state * cd_ref[h] + _dot(kz, vc)
            mu = jnp.mean(o, axis=-1, keepdims=True)
            d = o - mu
            var = jnp.mean(d * d, axis=-1, keepdims=True)
            on = d * lax.rsqrt(var + EPS) * gn_ref[:, hs]
            g = rg_ref[0, pl.ds(r0, C), hs]
            o_ref[0, pl.ds(r0, C), hs] = (g * jax.nn.sigmoid(g) * on).astype(o_ref.dtype)
        return carry

    lax.fori_loop(0, n_chunks, chunk, 0)


def _retention(rq, rk, rv, rg, g_gn, decay, zeta, xi, cd, ts):
    B, S, W = rq.shape
    tok = pl.BlockSpec((1, ts, W), lambda b, i: (b, i, 0))
    return pl.pallas_call(
        functools.partial(_retention_kernel, n_chunks=ts // RET_CHUNK),
        grid=(B, S // ts),
        in_specs=[tok, tok, tok, tok, _const_spec(g_gn.shape), _const_spec(decay.shape),
                  _const_spec(zeta.shape), _const_spec(xi.shape), _const_spec(cd.shape)],
        out_specs=tok,
        out_shape=jax.ShapeDtypeStruct((B, S, W), BF16),
        scratch_shapes=[pltpu.VMEM((RET_HEADS, RET_DK, RET_DV), F32)],
        compiler_params=_params(("parallel", "arbitrary")),
        name="retention",
    )(rq, rk, rv, rg, g_gn, decay, zeta, xi, cd)


def _mem_kv_kernel(mem_ref, g_ref, w_ref, k_ref, v_ref):
    D = mem_ref.shape[-1]
    mn = _rms(mem_ref[0], g_ref[...]).astype(BF16)
    kv = _dot(mn, w_ref[...])
    k_ref[0] = kv[:, :D].astype(BF16)
    v_ref[0] = kv[:, D:].astype(BF16)


def _mem_kv(mem, g_mem, w_xkv):
    B, M, D = mem.shape
    blk = pl.BlockSpec((1, M, D), lambda b: (b, 0, 0))
    return pl.pallas_call(
        _mem_kv_kernel,
        grid=(B,),
        in_specs=[blk, _const_spec(g_mem.shape), _const_spec(w_xkv.shape)],
        out_specs=[blk, blk],
        out_shape=[jax.ShapeDtypeStruct((B, M, D), BF16)] * 2,
        compiler_params=_params(("parallel",)),
        name="mem_kv",
    )(mem, g_mem, w_xkv)


def _post_mix_kernel(x_ref, omla_ref, oret_ref, woa_ref, wob_ref, gx_ref, wxq_ref, kx_ref, vx_ref, wxo_ref,
                     gf_ref, wpq_ref, keys_ref, h2_ref, a3t_ref, st_ref, ox_ref):
    D = x_ref.shape[-1]
    h1 = x_ref[...] + _dot(omla_ref[...], woa_ref[...]) + _dot(oret_ref[...], wob_ref[...])
    a2 = _rms(h1, gx_ref[...]).astype(BF16)
    qx = _dot(a2, wxq_ref[...])
    hd = D // X_HEADS
    for h in range(X_HEADS):
        sl = slice(h * hd, (h + 1) * hd)
        s = _dot_nt(qx[:, sl].astype(BF16), kx_ref[0, :, sl]) * (hd ** -0.5)
        m = jnp.max(s, axis=-1, keepdims=True)
        p = jnp.exp(s - m)
        p = p / jnp.sum(p, axis=-1, keepdims=True)
        ox_ref[:, sl] = _dot(p.astype(BF16), vx_ref[0, :, sl]).astype(BF16)
    h2 = h1 + _dot(ox_ref[...], wxo_ref[...])
    h2_ref[...] = h2
    a3 = _rms(h2, gf_ref[...])
    a3t_ref[...] = a3.T.astype(BF16)
    qp = _dot(a3.astype(BF16), wpq_ref[...]).astype(BF16)
    half = PEER_DKEY // 2
    for hp in range(2 * PEER_HEADS):
        sc = _dot_nt(keys_ref[hp % 2], qp[:, hp * half:(hp + 1) * half])
        for tl in range(sc.shape[1] // LANES):
            st_ref[hp, tl] = sc[:, tl * LANES:(tl + 1) * LANES]


def _post_mix(x2, o_mla, o_ret, w_oa, w_ob, g_x, w_xq, kx, vx, w_xo, g_f, w_pq, keys, seq, tm):
    T, D = x2.shape
    nS = seq // tm
    tok = lambda w: pl.BlockSpec((tm, w), lambda i: (i, 0))
    memb = pl.BlockSpec((1,) + kx.shape[1:], lambda i: (i // nS, 0, 0))
    return pl.pallas_call(
        _post_mix_kernel,
        grid=(T // tm,),
        in_specs=[tok(D), tok(o_mla.shape[1]), tok(o_ret.shape[1]), _const_spec(w_oa.shape),
                  _const_spec(w_ob.shape), _const_spec(g_x.shape), _const_spec(w_xq.shape), memb, memb,
                  _const_spec(w_xo.shape), _const_spec(g_f.shape), _const_spec(w_pq.shape),
                  _const_spec(keys.shape)],
        out_specs=[tok(D), pl.BlockSpec((D, tm), lambda i: (0, i)),
                   pl.BlockSpec((2 * PEER_HEADS, tm // LANES, PEER_KEYS, LANES), lambda i: (0, i, 0, 0))],
        out_shape=[jax.ShapeDtypeStruct((T, D), F32), jax.ShapeDtypeStruct((D, T), BF16),
                   jax.ShapeDtypeStruct((2 * PEER_HEADS, T // LANES, PEER_KEYS, LANES), F32)],
        scratch_shapes=[pltpu.VMEM((tm, D), BF16)],
        compiler_params=_params(("parallel",)),
        name="post_mix",
    )(x2, o_mla, o_ret, w_oa, w_ob, g_x, w_xq, kx, vx, w_xo, g_f, w_pq, keys)


N_TOP = PEER_TOPK + 1
TOP_ROWS = 24
SUBLANES = 8


def _sort_network(n):
    def merge(lo, hi, r):
        step = r * 2
        if step < hi - lo:
            yield from merge(lo, hi, step)
            yield from merge(lo + r, hi, step)
            yield from [(i, i + r) for i in range(lo + r, hi - r, step)]
        else:
            yield (lo, lo + r)

    def sort(lo, hi):
        if hi - lo >= 1:
            mid = lo + (hi - lo) // 2
            yield from sort(lo, mid)
            yield from sort(mid + 1, hi)
            yield from merge(lo, hi, 1)

    return list(sort(0, n - 1))


def _top17(x):
    k = PEER_TOPK
    v = [x[SUBLANES * i:SUBLANES * (i + 1), :] for i in range(x.shape[0] // SUBLANES)]
    assert len(v) == k
    for i, j in _sort_network(k):
        v[i], v[j] = jnp.maximum(v[i], v[j]), jnp.minimum(v[i], v[j])
    dropped = jnp.full(v[0].shape, NEG, F32)
    for shift in (SUBLANES // 2, SUBLANES // 4, SUBLANES // 8):
        other = [pltpu.roll(a, SUBLANES - shift, axis=0) for a in v]
        lose = [jnp.minimum(v[i], other[k - 1 - i]) for i in range(k)]
        v = [jnp.maximum(v[i], other[k - 1 - i]) for i in range(k)]
        for a in lose + [pltpu.roll(dropped, SUBLANES - shift, axis=0)]:
            dropped = jnp.maximum(dropped, a)
        d = k // 2
        while d >= 1:
            for i in range(k):
                if i & d == 0:
                    v[i], v[i + d] = jnp.maximum(v[i], v[i + d]), jnp.minimum(v[i], v[i + d])
            d //= 2
    return [a[0:1, :] for a in v], dropped[0:1, :]


def _peer_route_kernel(st_ref, stats_ref, vs_ref):
    tl = st_ref.shape[-1]
    vs_ref[...] = jnp.full(vs_ref.shape, NEG, F32)
    for h in range(PEER_HEADS):
        for p in range(2):
            best, nxt = _top17(st_ref[2 * h + p, 0])
            for it, row in enumerate(best + [nxt]):
                vs_ref[p, it:it + 1, :] = row
        v1 = lambda i: vs_ref[0, i:i + 1, :]
        v2 = lambda i: vs_ref[1, i:i + 1, :]
        cands = [v1(0) + vs_ref[1, 8 * r:8 * r + 8, :] for r in range(TOP_ROWS // 8)]
        cands += [v1(i) + vs_ref[1, 0:8, :] for i in range(1, 8)]
        cands += [vs_ref[0, 8 * r:8 * r + 8, :] + v2(0) for r in range(1, TOP_ROWS // 8)]
        cands += [jnp.full((SUBLANES, tl), NEG, F32)] * (PEER_KEYS // SUBLANES - len(cands))
        best, c17 = _top17(jnp.concatenate(cands, axis=0))
        top = best[0]
        z = jnp.zeros((1, tl), F32)
        for m in best:
            z = z + jnp.exp(m - top)
        stats_ref[0, h:h + 1, :] = 0.5 * (best[-1] + c17)
        stats_ref[1, h:h + 1, :] = v1(0)
        stats_ref[2, h:h + 1, :] = v2(0)
        stats_ref[3, h:h + 1, :] = 1.0 / z


def _peer_route(st):
    HP, nb, K, tl = st.shape
    T = nb * tl
    return pl.pallas_call(
        _peer_route_kernel,
        grid=(nb,),
        in_specs=[pl.BlockSpec((HP, 1, K, tl), lambda i: (0, i, 0, 0))],
        out_specs=pl.BlockSpec((4, PEER_HEADS, tl), lambda i: (0, 0, i)),
        out_shape=jax.ShapeDtypeStruct((4, PEER_HEADS, T), F32),
        scratch_shapes=[pltpu.VMEM((2, TOP_ROWS, tl), F32)],
        compiler_params=_params(("parallel",)),
        name="peer_route",
    )(st)


def _peer_dense_kernel(a3t_ref, st_ref, stats_ref, u_ref, v_ref, h2_ref, g_ref, o_ref,
                       thr_ref, c1_ref, e2_ref, acc_ref, *, te, tm):
    j = pl.program_id(1)
    nk = PEER_KEYS
    n_al = te // nk
    nt = tm // LANES

    @pl.when(j == 0)
    def _():
        acc_ref[...] = jnp.zeros(acc_ref.shape, F32)
        for h in range(PEER_HEADS):
            for tl in range(nt):
                ls = slice(tl * LANES, (tl + 1) * LANES)
                s1 = st_ref[2 * h, tl]
                thr_ref[h, :, tl, :] = stats_ref[0, h:h + 1, ls] - s1
                c1_ref[h, :, tl, :] = jnp.exp(s1 - stats_ref[1, h:h + 1, ls]) * stats_ref[3, h:h + 1, ls]
                e2_ref[h, tl] = jnp.exp(st_ref[2 * h + 1, tl] - stats_ref[2, h:h + 1, ls])

    act = _dot(u_ref[...], a3t_ref[...])
    p_rows = []
    for al in range(n_al):
        a = j * n_al + al
        rs = slice(al * nk, (al + 1) * nk)
        tiles = []
        for tl in range(nt):
            w = jnp.zeros((nk, LANES), F32)
            for h in range(PEER_HEADS):
                thr = thr_ref[h, a, tl:tl + 1, :]
                c1 = c1_ref[h, a, tl:tl + 1, :]
                w = w + jnp.where(st_ref[2 * h + 1, tl] >= thr, e2_ref[h, tl] * c1, 0.0)
            tiles.append((_gelu(act[rs, tl * LANES:(tl + 1) * LANES]) * w).astype(BF16))
        p_rows.append(jnp.concatenate(tiles, axis=1))
    p = jnp.concatenate(p_rows, axis=0)
    acc_ref[...] += lax.dot_general(v_ref[...], p, (((0,), (0,)), ((), ())), preferred_element_type=F32)

    @pl.when(j == pl.num_programs(1) - 1)
    def _():
        o_ref[...] = _rms(h2_ref[...] + acc_ref[...].T, g_ref[...])


def _peer_dense(a3t, st, stats, u_bf, v_bf, h2, g, tm, te):
    D, T = a3t.shape
    nt = tm // LANES
    return pl.pallas_call(
        functools.partial(_peer_dense_kernel, te=te, tm=tm),
        grid=(T // tm, u_bf.shape[0] // te),
        in_specs=[pl.BlockSpec((D, tm), lambda i, j: (0, i)),
                  pl.BlockSpec((st.shape[0], nt) + st.shape[2:], lambda i, j: (0, i, 0, 0)),
                  pl.BlockSpec(stats.shape[:2] + (tm,), lambda i, j: (0, 0, i)),
                  pl.BlockSpec((te, D), lambda i, j: (j, 0)),
                  pl.BlockSpec((te, D), lambda i, j: (j, 0)),
                  pl.BlockSpec((tm, D), lambda i, j: (i, 0)), _const_spec(g.shape)],
        out_specs=pl.BlockSpec((tm, D), lambda i, j: (i, 0)),
        out_shape=jax.ShapeDtypeStruct((T, D), F32),
        scratch_shapes=[pltpu.VMEM((PEER_HEADS, PEER_KEYS, nt, LANES), F32)] * 2
        + [pltpu.VMEM((PEER_HEADS, nt, PEER_KEYS, LANES), F32), pltpu.VMEM((D, tm), F32)],
        compiler_params=_params(("parallel", "arbitrary")),
        name="peer_dense",
    )(a3t, st, stats, u_bf, v_bf, h2, g)


def _rope_tables(seq):
    pos = jnp.arange(seq, dtype=F32)

    def tab(dim):
        inv = 1.0 / (ROPE_BASE ** (jnp.arange(0, dim, 2, dtype=F32) / dim))
        ang = pos[:, None] * inv[None, :]
        return jnp.cos(ang), jnp.sin(ang)

    cm, sm = tab(MLA_ROPE)
    pad = LANES - MLA_NOPE - MLA_ROPE
    cosm = jnp.concatenate([jnp.ones((seq, MLA_NOPE), F32), cm, cm, jnp.zeros((seq, pad), F32)], axis=1)
    sinm = jnp.concatenate([jnp.zeros((seq, MLA_NOPE), F32), -sm, sm, jnp.zeros((seq, pad), F32)], axis=1)
    cr, sr = tab(RET_DK)
    cosr = jnp.concatenate([cr, cr], axis=1)
    sinr = jnp.concatenate([-sr, sr], axis=1)
    return cosm, sinm, cosr, sinr


def _retention_tables():
    C = RET_CHUNK
    log_g = jnp.log(1.0 - 2.0 ** (-5.0 - jnp.arange(RET_HEADS, dtype=F32)))
    idx = jnp.arange(C, dtype=F32)
    rel = idx[:, None] - idx[None, :]
    decay = jnp.where(rel[None] >= 0, jnp.exp(jnp.maximum(rel, 0.0)[None] * log_g[:, None, None]), 0.0)
    zeta = jnp.exp((C - 1 - idx)[None, :] * log_g[:, None])
    xi = jnp.exp((idx + 1)[None, :] * log_g[:, None])
    cd = jnp.exp(C * log_g)
    bc = lambda t: jnp.broadcast_to(t[:, :, None], (RET_HEADS, C, LANES))
    return decay, bc(zeta), bc(xi), jnp.broadcast_to(cd[:, None, None], (RET_HEADS, C, LANES))


def _layer(h, mem, g_mix, w_in, g_q_lora, w_uq, g_kv_lora, w_ukv, g_ret_gn, w_o, g_xattn, g_mem,
           w_xq, w_xkv, w_xo, g_ffn, w_pq, sub_keys, u_experts, v_experts, g_out):
    B, S, D = h.shape
    T = B * S
    row = lambda g: g.reshape(1, -1)

    o_pe = MLA_Q_RANK + MLA_KV_RANK
    pe_pad = jnp.zeros((D, LANES), F32).at[:, MLA_NOPE:MLA_NOPE + MLA_ROPE].set(w_in[:, o_pe:o_pe + MLA_ROPE])
    w_in_ext = jnp.concatenate([w_in[:, :o_pe], pe_pad, w_in[:, o_pe + MLA_ROPE:]], axis=1).astype(BF16)
    dq = MLA_NOPE + MLA_ROPE
    w_uq_pad = jnp.pad(w_uq.reshape(MLA_Q_RANK, MLA_HEADS, dq), ((0, 0), (0, 0), (0, LANES - dq)))
    w_uq_pad = w_uq_pad.reshape(MLA_Q_RANK, MLA_HEADS * LANES).astype(BF16)
    w_ukv3 = w_ukv.reshape(MLA_KV_RANK, MLA_HEADS, MLA_NOPE + MLA_V)
    w_k = jnp.pad(w_ukv3[:, :, :MLA_NOPE], ((0, 0), (0, 0), (0, LANES - MLA_NOPE)))
    w_k = w_k.reshape(MLA_KV_RANK, MLA_HEADS * LANES).astype(BF16)
    w_v = jnp.pad(w_ukv3[:, :, MLA_NOPE:], ((0, 0), (0, 0), (0, LANES - MLA_V)))
    w_v = w_v.reshape(MLA_KV_RANK, MLA_HEADS * LANES).astype(BF16)

    cosm, sinm, cosr, sinr = _rope_tables(S)
    q, k, v, rq, rk, rv, rg = _in_proj(h.reshape(T, D), row(g_mix), w_in_ext, row(g_q_lora), w_uq_pad,
                                       row(g_kv_lora), w_k, w_v, cosm, sinm, cosr, sinr, S, min(256, S))
    r3 = lambda t: t.reshape(B, S, t.shape[-1])
    o_mla = _mla_attn(r3(q), r3(k), r3(v), min(512, S))
    o_ret = _retention(r3(rq), r3(rk), r3(rv), r3(rg), row(g_ret_gn), *_retention_tables(), min(1024, S))

    kx, vx = _mem_kv(mem, row(g_mem), w_xkv.astype(BF16))
    n_mla = MLA_HEADS * MLA_V
    w_o_bf = w_o.astype(BF16)
    h2, a3t, st = _post_mix(h.reshape(T, D), o_mla.reshape(T, -1), o_ret.reshape(T, -1), w_o_bf[:n_mla],
                            w_o_bf[n_mla:], row(g_xattn), w_xq.astype(BF16), kx, vx, w_xo.astype(BF16),
                            row(g_ffn), w_pq.astype(BF16), sub_keys.astype(BF16), S, min(256, S))
    stats = _peer_route(st)
    out = _peer_dense(a3t, st, stats, u_experts.astype(BF16), v_experts.astype(BF16), h2, row(g_out),
                      min(512, T), 1024)
    return out.reshape(B, S, D)


def kernel(x, mem, g_mix, w_in, g_q_lora, w_uq, g_kv_lora, w_ukv, g_ret_gn, w_o, g_xattn, g_mem, w_xq, w_xkv,
           w_xo, g_ffn, w_pq, sub_keys, u_experts, v_experts, g_final):
    depth = g_mix.shape[0]
    assert depth == 1, "the final norm is fused into the single layer's last kernel"
    l = 0
    return _layer(x, mem, g_mix[l], w_in[l], g_q_lora[l], w_uq[l], g_kv_lora[l], w_ukv[l], g_ret_gn[l], w_o[l],
                  g_xattn[l], g_mem[l], w_xq[l], w_xkv[l], w_xo[l], g_ffn[l], w_pq[l], sub_keys[l],
                  u_experts[l], v_experts[l], g_final)
```

```python
import functools
import math

import jax
import jax.numpy as jnp
from jax import lax
from jax.experimental import pallas as pl
from jax.experimental.pallas import tpu as pltpu

F32 = jnp.float32
BF16 = jnp.bfloat16

LANES = 128
EPS = 1e-6
NEG = -1e30
ROPE_BASE = 10000.0
VMEM_LIMIT = 56 * 1024 * 1024

MLA_HEADS, MLA_Q_RANK, MLA_KV_RANK = 8, 384, 256
MLA_NOPE, MLA_ROPE, MLA_V = 64, 32, 64
RET_HEADS, RET_DK, RET_DV, RET_CHUNK = 4, 128, 128, 128
X_HEADS = 4
PEER_KEYS, PEER_HEADS, PEER_TOPK, PEER_DKEY = 128, 8, 16, 256


def _params(sem, vmem=VMEM_LIMIT):
    return pltpu.CompilerParams(dimension_semantics=sem, vmem_limit_bytes=vmem)


def _rms(x, g):
    return x * lax.rsqrt(jnp.mean(x * x, axis=-1, keepdims=True) + EPS) * g


def _dot(a, b):
    return jnp.dot(a, b, preferred_element_type=F32)


def _dot_nt(a, b):
    return lax.dot_general(a, b, (((1,), (1,)), ((), ())), preferred_element_type=F32)


def _gelu(x):
    return 0.5 * x * (1.0 + lax.erf(x * (2.0 ** -0.5)))


def _const_spec(shape):
    nd = len(shape)
    return pl.BlockSpec(shape, lambda *_: (0,) * nd)


def _in_proj_kernel(x_ref, gmix_ref, win_ref, gq_ref, wuq_ref, gkv_ref, wk_ref, wv_ref,
                    cosm_ref, sinm_ref, cosr_ref, sinr_ref,
                    q_ref, k_ref, v_ref, rq_ref, rk_ref, rv_ref, rg_ref):
    x = x_ref[...]
    a = _rms(x, gmix_ref[...]).astype(BF16)
    proj = _dot(a, win_ref[...])
    o_cq, o_ckv, o_pe, o_rq = 0, MLA_Q_RANK, MLA_Q_RANK + MLA_KV_RANK, MLA_Q_RANK + MLA_KV_RANK + LANES
    rw = RET_HEADS * RET_DK
    cq = proj[:, o_cq:o_ckv]
    ckv = proj[:, o_ckv:o_pe]
    kpe = proj[:, o_pe:o_rq]
    cqn = _rms(cq, gq_ref[...]).astype(BF16)
    ckvn = _rms(ckv, gkv_ref[...]).astype(BF16)
    q = _dot(cqn, wuq_ref[...])
    kn = _dot(ckvn, wk_ref[...])
    vlane = lax.broadcasted_iota(jnp.int32, (x.shape[0], MLA_HEADS * LANES), 1) % LANES
    v_ref[...] = jnp.where(vlane == MLA_V, 1.0, _dot(ckvn, wv_ref[...])).astype(BF16)

    cosm, sinm = cosm_ref[...], sinm_ref[...]
    lane = lax.broadcasted_iota(jnp.int32, (x.shape[0], LANES), 1)
    first_half = lane < MLA_NOPE + MLA_ROPE // 2

    def rope_m(c):
        rot = jnp.where(first_half, pltpu.roll(c, LANES - MLA_ROPE // 2, axis=1),
                        pltpu.roll(c, MLA_ROPE // 2, axis=1))
        return c * cosm + rot * sinm

    qscale = (MLA_NOPE + MLA_ROPE) ** -0.5 * math.log2(math.e)
    kpe_r = rope_m(kpe)
    for h in range(MLA_HEADS):
        sl = slice(h * LANES, (h + 1) * LANES)
        q_ref[:, sl] = (rope_m(q[:, sl]) * qscale).astype(BF16)
        k_ref[:, sl] = (kn[:, sl] + kpe_r).astype(BF16)

    cosr, sinr = cosr_ref[...], sinr_ref[...]
    kscale = RET_DK ** -0.5
    for h in range(RET_HEADS):
        sl = slice(h * LANES, (h + 1) * LANES)
        c = proj[:, o_rq + h * LANES:o_rq + (h + 1) * LANES]
        rq_ref[:, sl] = (c * cosr + pltpu.roll(c, RET_DK // 2, axis=1) * sinr).astype(BF16)
        c = proj[:, o_rq + rw + h * LANES:o_rq + rw + (h + 1) * LANES]
        rk_ref[:, sl] = ((c * cosr + pltpu.roll(c, RET_DK // 2, axis=1) * sinr) * kscale).astype(BF16)
    rv_ref[...] = proj[:, o_rq + 2 * rw:o_rq + 3 * rw].astype(BF16)
    rg_ref[...] = proj[:, o_rq + 3 * rw:o_rq + 4 * rw]


def _in_proj(x2, g_mix, w_in_ext, g_q, w_uq_pad, g_kv, w_k, w_v, cosm, sinm, cosr, sinr, seq, tm):
    T, D = x2.shape
    nS = seq // tm
    tok = lambda w: pl.BlockSpec((tm, w), lambda i: (i, 0))
    pos = lambda w: pl.BlockSpec((tm, w), lambda i: (i % nS, 0))
    outs = [(T, MLA_HEADS * LANES, BF16), (T, MLA_HEADS * LANES, BF16), (T, MLA_HEADS * LANES, BF16),
            (T, RET_HEADS * RET_DK, BF16), (T, RET_HEADS * RET_DK, BF16), (T, RET_HEADS * RET_DV, BF16),
            (T, RET_HEADS * RET_DV, F32)]
    return pl.pallas_call(
        _in_proj_kernel,
        grid=(T // tm,),
        in_specs=[tok(D), _const_spec(g_mix.shape), _const_spec(w_in_ext.shape), _const_spec(g_q.shape),
                  _const_spec(w_uq_pad.shape), _const_spec(g_kv.shape), _const_spec(w_k.shape),
                  _const_spec(w_v.shape), pos(LANES), pos(LANES), pos(LANES), pos(LANES)],
        out_specs=[tok(w) for (_, w, _) in outs],
        out_shape=[jax.ShapeDtypeStruct((t, w), d) for (t, w, d) in outs],
        compiler_params=_params(("parallel",)),
        name="in_proj",
    )(x2, g_mix, w_in_ext, g_q, w_uq_pad, g_kv, w_k, w_v, cosm, sinm, cosr, sinr)


def _mla_attn_kernel(q_ref, k_ref, v_ref, o_ref, *, t):
    i = pl.program_id(2)
    causal = (lax.broadcasted_iota(jnp.int32, (t, t), 1) <= lax.broadcasted_iota(jnp.int32, (t, t), 0))

    def step(j, carry, masked):
        k0 = pl.multiple_of(j * t, t)
        new = []
        for hh in range(2):
            m, acc = carry[hh]
            hs = slice(hh * LANES, (hh + 1) * LANES)
            s = _dot_nt(q_ref[0, :, hs], k_ref[0, pl.ds(k0, t), hs])
            if masked:
                s = jnp.where(causal, s, NEG)
            m_new = jnp.maximum(m, jnp.max(s, axis=-1, keepdims=True))
            p = jnp.exp2(s - m_new).astype(BF16)
            acc = jnp.exp2(m - m_new) * acc + _dot(p, v_ref[0, pl.ds(k0, t), hs])
            new.append((m_new, acc))
        return tuple(new)

    init = tuple((jnp.full((t, 1), NEG, F32), jnp.zeros((t, LANES), F32)) for _ in range(2))

    def pair(pi, carry):
        return step(2 * pi + 1, step(2 * pi, carry, False), False)

    carry = lax.fori_loop(0, i // 2, pair, init)
    carry = lax.cond(i % 2 == 1, lambda c: step(i - 1, c, False), lambda c: c, carry)
    carry = step(i, carry, True)
    outs = [acc[:, :MLA_V] / acc[:, MLA_V:MLA_V + 1] for (_, acc) in carry]
    o_ref[0] = jnp.concatenate(outs, axis=-1).astype(o_ref.dtype)


def _mla_attn(q, k, v, t):
    B, S, _ = q.shape
    return pl.pallas_call(
        functools.partial(_mla_attn_kernel, t=t),
        grid=(B, MLA_HEADS // 2, S // t),
        in_specs=[pl.BlockSpec((1, t, 2 * LANES), lambda b, h, i: (b, i, h)),
                  pl.BlockSpec((1, S, 2 * LANES), lambda b, h, i: (b, 0, h)),
                  pl.BlockSpec((1, S, 2 * LANES), lambda b, h, i: (b, 0, h))],
        out_specs=pl.BlockSpec((1, t, 2 * MLA_V), lambda b, h, i: (b, i, h)),
        out_shape=jax.ShapeDtypeStruct((B, S, MLA_HEADS * MLA_V), BF16),
        compiler_params=_params(("parallel", "parallel", "arbitrary")),
        name="mla_attn",
    )(q, k, v)


def _retention_kernel(rq_ref, rk_ref, rv_ref, rg_ref, gn_ref, decay_ref, zeta_ref, xi_ref, cd_ref,
                      o_ref, r_ref, *, n_chunks):
    C = RET_CHUNK

    @pl.when(pl.program_id(1) == 0)
    def _():
        r_ref[...] = jnp.zeros(r_ref.shape, F32)

    def chunk(n, carry):
        r0 = pl.multiple_of(n * C, C)
        for h in range(RET_HEADS):
            hs = slice(h * LANES, (h + 1) * LANES)
            qc = rq_ref[0, pl.ds(r0, C), hs]
            kc = rk_ref[0, pl.ds(r0, C), hs]
            vc = rv_ref[0, pl.ds(r0, C), hs]
            state = r_ref[h]
            sc = _dot_nt(qc, kc) * decay_ref[h]
            inner = _dot(sc.astype(BF16), vc)
            cross = _dot(qc, state.astype(BF16)) * xi_ref[h]
            o = inner + cross
            kz = (kc.astype(F32) * zeta_ref[h]).T.astype(BF16)
            r_ref[h] = state * cd_ref[h] + _dot(kz, vc)
            mu = jnp.mean(o, axis=-1, keepdims=True)
            d = o - mu
            var = jnp.mean(d * d, axis=-1, keepdims=True)
            on = d * lax.rsqrt(var + EPS) * gn_ref[:, hs]
            g = rg_ref[0, pl.ds(r0, C), hs]
            o_ref[0, pl.ds(r0, C), hs] = (g * jax.nn.sigmoid(g) * on).astype(o_ref.dtype)
        return carry

    lax.fori_loop(0, n_chunks, chunk, 0)


def _retention(rq, rk, rv, rg, g_gn, decay, zeta, xi, cd, ts):
    B, S, W = rq.shape
    tok = pl.BlockSpec((1, ts, W), lambda b, i: (b, i, 0))
    return pl.pallas_call(
        functools.partial(_retention_kernel, n_chunks=ts // RET_CHUNK),
        grid=(B, S // ts),
        in_specs=[tok, tok, tok, tok, _const_spec(g_gn.shape), _const_spec(decay.shape),
                  _const_spec(zeta.shape), _const_spec(xi.shape), _const_spec(cd.shape)],
        out_specs=tok,
        out_shape=jax.ShapeDtypeStruct((B, S, W), BF16),
        scratch_shapes=[pltpu.VMEM((RET_HEADS, RET_DK, RET_DV), F32)],
        compiler_params=_params(("parallel", "arbitrary")),
        name="retention",
    )(rq, rk, rv, rg, g_gn, decay, zeta, xi, cd)


def _mem_kv_kernel(mem_ref, g_ref, w_ref, k_ref, v_ref):
    D = mem_ref.shape[-1]
    mn = _rms(mem_ref[0], g_ref[...]).astype(BF16)
    kv = _dot(mn, w_ref[...])
    k_ref[0] = kv[:, :D].astype(BF16)
    v_ref[0] = kv[:, D:].astype(BF16)


def _mem_kv(mem, g_mem, w_xkv):
    B, M, D = mem.shape
    blk = pl.BlockSpec((1, M, D), lambda b: (b, 0, 0))
    return pl.pallas_call(
        _mem_kv_kernel,
        grid=(B,),
        in_specs=[blk, _const_spec(g_mem.shape), _const_spec(w_xkv.shape)],
        out_specs=[blk, blk],
        out_shape=[jax.ShapeDtypeStruct((B, M, D), BF16)] * 2,
        compiler_params=_params(("parallel",)),
        name="mem_kv",
    )(mem, g_mem, w_xkv)


def _post_mix_kernel(x_ref, omla_ref, oret_ref, woa_ref, wob_ref, gx_ref, wxq_ref, kx_ref, vx_ref, wxo_ref,
                     gf_ref, wpq_ref, keys_ref, h2_ref, a3t_ref, st_ref, ox_ref):
    D = x_ref.shape[-1]
    h1 = x_ref[...] + _dot(omla_ref[...], woa_ref[...]) + _dot(oret_ref[...], wob_ref[...])
    a2 = _rms(h1, gx_ref[...]).astype(BF16)
    qx = _dot(a2, wxq_ref[...])
    hd = D // X_HEADS
    for h in range(X_HEADS):
        sl = slice(h * hd, (h + 1) * hd)
        s = _dot_nt(qx[:, sl].astype(BF16), kx_ref[0, :, sl]) * (hd ** -0.5)
        m = jnp.max(s, axis=-1, keepdims=True)
        p = jnp.exp(s - m)
        p = p / jnp.sum(p, axis=-1, keepdims=True)
        ox_ref[:, sl] = _dot(p.astype(BF16), vx_ref[0, :, sl]).astype(BF16)
    h2 = h1 + _dot(ox_ref[...], wxo_ref[...])
    h2_ref[...] = h2
    a3 = _rms(h2, gf_ref[...])
    a3t_ref[...] = a3.T.astype(BF16)
    qp = _dot(a3.astype(BF16), wpq_ref[...]).astype(BF16)
    half = PEER_DKEY // 2
    for hp in range(2 * PEER_HEADS):
        sc = _dot_nt(keys_ref[hp % 2], qp[:, hp * half:(hp + 1) * half])
        for tl in range(sc.shape[1] // LANES):
            st_ref[hp, tl] = sc[:, tl * LANES:(tl + 1) * LANES]


def _post_mix(x2, o_mla, o_ret, w_oa, w_ob, g_x, w_xq, kx, vx, w_xo, g_f, w_pq, keys, seq, tm):
    T, D = x2.shape
    nS = seq // tm
    tok = lambda w: pl.BlockSpec((tm, w), lambda i: (i, 0))
    memb = pl.BlockSpec((1,) + kx.shape[1:], lambda i: (i // nS, 0, 0))
    return pl.pallas_call(
        _post_mix_kernel,
        grid=(T // tm,),
        in_specs=[tok(D), tok(o_mla.shape[1]), tok(o_ret.shape[1]), _const_spec(w_oa.shape),
                  _const_spec(w_ob.shape), _const_spec(g_x.shape), _const_spec(w_xq.shape), memb, memb,
                  _const_spec(w_xo.shape), _const_spec(g_f.shape), _const_spec(w_pq.shape),
                  _const_spec(keys.shape)],
        out_specs=[tok(D), pl.BlockSpec((D, tm), lambda i: (0, i)),
                   pl.BlockSpec((2 * PEER_HEADS, tm // LANES, PEER_KEYS, LANES), lambda i: (0, i, 0, 0))],
        out_shape=[jax.ShapeDtypeStruct((T, D), F32), jax.ShapeDtypeStruct((D, T), BF16),
                   jax.ShapeDtypeStruct((2 * PEER_HEADS, T // LANES, PEER_KEYS, LANES), F32)],
        scratch_shapes=[pltpu.VMEM((tm, D), BF16)],
        compiler_params=_params(("parallel",)),
        name="post_mix",
    )(x2, o_mla, o_ret, w_oa, w_ob, g_x, w_xq, kx, vx, w_xo, g_f, w_pq, keys)


N_TOP = PEER_TOPK + 1
TOP_ROWS = 24
SUBLANES = 8


def _sort_network(n):
    def merge(lo, hi, r):
        step = r * 2
        if step < hi - lo:
            yield from merge(lo, hi, step)
            yield from merge(lo + r, hi, step)
            yield from [(i, i + r) for i in range(lo + r, hi - r, step)]
        else:
            yield (lo, lo + r)

    def sort(lo, hi):
        if hi - lo >= 1:
            mid = lo + (hi - lo) // 2
            yield from sort(lo, mid)
            yield from sort(mid + 1, hi)
            yield from merge(lo, hi, 1)

    return list(sort(0, n - 1))


def _top17(x):
    k = PEER_TOPK
    v = [x[SUBLANES * i:SUBLANES * (i + 1), :] for i in range(x.shape[0] // SUBLANES)]
    assert len(v) == k
    for i, j in _sort_network(k):
        v[i], v[j] = jnp.maximum(v[i], v[j]), jnp.minimum(v[i], v[j])
    dropped = jnp.full(v[0].shape, NEG, F32)
    for shift in (SUBLANES // 2, SUBLANES // 4, SUBLANES // 8):
        other = [pltpu.roll(a, SUBLANES - shift, axis=0) for a in v]
        lose = [jnp.minimum(v[i], other[k - 1 - i]) for i in range(k)]
        v = [jnp.maximum(v[i], other[k - 1 - i]) for i in range(k)]
        for a in lose + [pltpu.roll(dropped, SUBLANES - shift, axis=0)]:
            dropped = jnp.maximum(dropped, a)
        d = k // 2
        while d >= 1:
            for i in range(k):
                if i & d == 0:
                    v[i], v[i + d] = jnp.maximum(v[i], v[i + d]), jnp.minimum(v[i], v[i + d])
            d //= 2
    return [a[0:1, :] for a in v], dropped[0:1, :]


def _peer_route_kernel(st_ref, stats_ref, vs_ref):
    tl = st_ref.shape[-1]
    vs_ref[...] = jnp.full(vs_ref.shape, NEG, F32)
    for h in range(PEER_HEADS):
        for p in range(2):
            best, nxt = _top17(st_ref[2 * h + p, 0])
            for it, row in enumerate(best + [nxt]):
                vs_ref[p, it:it + 1, :] = row
        v1 = lambda i: vs_ref[0, i:i + 1, :]
        v2 = lambda i: vs_ref[1, i:i + 1, :]
        cands = [v1(0) + vs_ref[1, 8 * r:8 * r + 8, :] for r in range(TOP_ROWS // 8)]
        cands += [v1(i) + vs_ref[1, 0:8, :] for i in range(1, 8)]
        cands += [vs_ref[0, 8 * r:8 * r + 8, :] + v2(0) for r in range(1, TOP_ROWS // 8)]
        cands += [jnp.full((SUBLANES, tl), NEG, F32)] * (PEER_KEYS // SUBLANES - len(cands))
        best, c17 = _top17(jnp.concatenate(cands, axis=0))
        top = best[0]
        z = jnp.zeros((1, tl), F32)
        for m in best:
            z = z + jnp.exp(m - top)
        stats_ref[0, h:h + 1, :] = 0.5 * (best[-1] + c17)
        stats_ref[1, h:h + 1, :] = v1(0)
        stats_ref[2, h:h + 1, :] = v2(0)
        stats_ref[3, h:h + 1, :] = 1.0 / z


def _peer_route(st):
    HP, nb, K, tl = st.shape
    T = nb * tl
    return pl.pallas_call(
        _peer_route_kernel,
        grid=(nb,),
        in_specs=[pl.BlockSpec((HP, 1, K, tl), lambda i: (0, i, 0, 0))],
        out_specs=pl.BlockSpec((4, PEER_HEADS, tl), lambda i: (0, 0, i)),
        out_shape=jax.ShapeDtypeStruct((4, PEER_HEADS, T), F32),
        scratch_shapes=[pltpu.VMEM((2, TOP_ROWS, tl), F32)],
        compiler_params=_params(("parallel",)),
        name="peer_route",
    )(st)


def _peer_dense_kernel(a3t_ref, st_ref, stats_ref, u_ref, v_ref, h2_ref, g_ref, o_ref,
                       thr_ref, c1_ref, e2_ref, acc_ref, *, te, tm):
    j = pl.program_id(1)
    nk = PEER_KEYS
    n_al = te // nk
    nt = tm // LANES

    @pl.when(j == 0)
    def _():
        acc_ref[...] = jnp.zeros(acc_ref.shape, F32)
        for h in range(PEER_HEADS):
            for tl in range(nt):
                ls = slice(tl * LANES, (tl + 1) * LANES)
                s1 = st_ref[2 * h, tl]
                thr_ref[h, :, tl, :] = stats_ref[0, h:h + 1, ls] - s1
                c1_ref[h, :, tl, :] = jnp.exp(s1 - stats_ref[1, h:h + 1, ls]) * stats_ref[3, h:h + 1, ls]
                e2_ref[h, tl] = jnp.exp(st_ref[2 * h + 1, tl] - stats_ref[2, h:h + 1, ls])

    act = _dot(u_ref[...], a3t_ref[...])
    p_rows = []
    for al in range(n_al):
        a = j * n_al + al
        rs = slice(al * nk, (al + 1) * nk)
        tiles = []
        for tl in range(nt):
            w = jnp.zeros((nk, LANES), F32)
            for h in range(PEER_HEADS):
                thr = thr_ref[h, a, tl:tl + 1, :]
                c1 = c1_ref[h, a, tl:tl + 1, :]
                w = w + jnp.where(st_ref[2 * h + 1, tl] >= thr, e2_ref[h, tl] * c1, 0.0)
            tiles.append((_gelu(act[rs, tl * LANES:(tl + 1) * LANES]) * w).astype(BF16))
        p_rows.append(jnp.concatenate(tiles, axis=1))
    p = jnp.concatenate(p_rows, axis=0)
    acc_ref[...] += lax.dot_general(v_ref[...], p, (((0,), (0,)), ((), ())), preferred_element_type=F32)

    @pl.when(j == pl.num_programs(1) - 1)
    def _():
        o_ref[...] = _rms(h2_ref[...] + acc_ref[...].T, g_ref[...])


def _peer_dense(a3t, st, stats, u_bf, v_bf, h2, g, tm, te):
    D, T = a3t.shape
    nt = tm // LANES
    return pl.pallas_call(
        functools.partial(_peer_dense_kernel, te=te, tm=tm),
        grid=(T // tm, u_bf.shape[0] // te),
        in_specs=[pl.BlockSpec((D, tm), lambda i, j: (0, i)),
                  pl.BlockSpec((st.shape[0], nt) + st.shape[2:], lambda i, j: (0, i, 0, 0)),
                  pl.BlockSpec(stats.shape[:2] + (tm,), lambda i, j: (0, 0, i)),
                  pl.BlockSpec((te, D), lambda i, j: (j, 0)),
                  pl.BlockSpec((te, D), lambda i, j: (j, 0)),
                  pl.BlockSpec((tm, D), lambda i, j: (i, 0)), _const_spec(g.shape)],
        out_specs=pl.BlockSpec((tm, D), lambda i, j: (i, 0)),
        out_shape=jax.ShapeDtypeStruct((T, D), F32),
        scratch_shapes=[pltpu.VMEM((PEER_HEADS, PEER_KEYS, nt, LANES), F32)] * 2
        + [pltpu.VMEM((PEER_HEADS, nt, PEER_KEYS, LANES), F32), pltpu.VMEM((D, tm), F32)],
        compiler_params=_params(("parallel", "arbitrary")),
        name="peer_dense",
    )(a3t, st, stats, u_bf, v_bf, h2, g)


def _rope_tables(seq):
    pos = jnp.arange(seq, dtype=F32)

    def tab(dim):
        inv = 1.0 / (ROPE_BASE ** (jnp.arange(0, dim, 2, dtype=F32) / dim))
        ang = pos[:, None] * inv[None, :]
        return jnp.cos(ang), jnp.sin(ang)

    cm, sm = tab(MLA_ROPE)
    pad = LANES - MLA_NOPE - MLA_ROPE
    cosm = jnp.concatenate([jnp.ones((seq, MLA_NOPE), F32), cm, cm, jnp.zeros((seq, pad), F32)], axis=1)
    sinm = jnp.concatenate([jnp.zeros((seq, MLA_NOPE), F32), -sm, sm, jnp.zeros((seq, pad), F32)], axis=1)
    cr, sr = tab(RET_DK)
    cosr = jnp.concatenate([cr, cr], axis=1)
    sinr = jnp.concatenate([-sr, sr], axis=1)
    return cosm, sinm, cosr, sinr


def _retention_tables():
    C = RET_CHUNK
    log_g = jnp.log(1.0 - 2.0 ** (-5.0 - jnp.arange(RET_HEADS, dtype=F32)))
    idx = jnp.arange(C, dtype=F32)
    rel = idx[:, None] - idx[None, :]
    decay = jnp.where(rel[None] >= 0, jnp.exp(jnp.maximum(rel, 0.0)[None] * log_g[:, None, None]), 0.0)
    zeta = jnp.exp((C - 1 - idx)[None, :] * log_g[:, None])
    xi = jnp.exp((idx + 1)[None, :] * log_g[:, None])
    cd = jnp.exp(C * log_g)
    bc = lambda t: jnp.broadcast_to(t[:, :, None], (RET_HEADS, C, LANES))
    return decay, bc(zeta), bc(xi), jnp.broadcast_to(cd[:, None, None], (RET_HEADS, C, LANES))


def _layer(h, mem, g_mix, w_in, g_q_lora, w_uq, g_kv_lora, w_ukv, g_ret_gn, w_o, g_xattn, g_mem,
           w_xq, w_xkv, w_xo, g_ffn, w_pq, sub_keys, u_experts, v_experts, g_out):
    B, S, D = h.shape
    T = B * S
    row = lambda g: g.reshape(1, -1)

    o_pe = MLA_Q_RANK + MLA_KV_RANK
    pe_pad = jnp.zeros((D, LANES), F32).at[:, MLA_NOPE:MLA_NOPE + MLA_ROPE].set(w_in[:, o_pe:o_pe + MLA_ROPE])
    w_in_ext = jnp.concatenate([w_in[:, :o_pe], pe_pad, w_in[:, o_pe + MLA_ROPE:]], axis=1).astype(BF16)
    dq = MLA_NOPE + MLA_ROPE
    w_uq_pad = jnp.pad(w_uq.reshape(MLA_Q_RANK, MLA_HEADS, dq), ((0, 0), (0, 0), (0, LANES - dq)))
    w_uq_pad = w_uq_pad.reshape(MLA_Q_RANK, MLA_HEADS * LANES).astype(BF16)
    w_ukv3 = w_ukv.reshape(MLA_KV_RANK, MLA_HEADS, MLA_NOPE + MLA_V)
    w_k = jnp.pad(w_ukv3[:, :, :MLA_NOPE], ((0, 0), (0, 0), (0, LANES - MLA_NOPE)))
    w_k = w_k.reshape(MLA_KV_RANK, MLA_HEADS * LANES).astype(BF16)
    w_v = jnp.pad(w_ukv3[:, :, MLA_NOPE:], ((0, 0), (0, 0), (0, LANES - MLA_V)))
    w_v = w_v.reshape(MLA_KV_RANK, MLA_HEADS * LANES).astype(BF16)

    cosm, sinm, cosr, sinr = _rope_tables(S)
    q, k, v, rq, rk, rv, rg = _in_proj(h.reshape(T, D), row(g_mix), w_in_ext, row(g_q_lora), w_uq_pad,
                                       row(g_kv_lora), w_k, w_v, cosm, sinm, cosr, sinr, S, min(512, S))
    r3 = lambda t: t.reshape(B, S, t.shape[-1])
    o_mla = _mla_attn(r3(q), r3(k), r3(v), min(512, S))
    o_ret = _retention(r3(rq), r3(rk), r3(rv), r3(rg), row(g_ret_gn), *_retention_tables(), min(1024, S))

    kx, vx = _mem_kv(mem, row(g_mem), w_xkv.astype(BF16))
    n_mla = MLA_HEADS * MLA_V
    w_o_bf = w_o.astype(BF16)
    h2, a3t, st = _post_mix(h.reshape(T, D), o_mla.reshape(T, -1), o_ret.reshape(T, -1), w_o_bf[:n_mla],
                            w_o_bf[n_mla:], row(g_xattn), w_xq.astype(BF16), kx, vx, w_xo.astype(BF16),
                            row(g_ffn), w_pq.astype(BF16), sub_keys.astype(BF16), S, min(512, S))
    stats = _peer_route(st)
    out = _peer_dense(a3t, st, stats, u_experts.astype(BF16), v_experts.astype(BF16), h2, row(g_out),
                      min(512, T), 1024)
    return out.reshape(B, S, D)


def kernel(x, mem, g_mix, w_in, g_q_lora, w_uq, g_kv_lora, w_ukv, g_ret_gn, w_o, g_xattn, g_mem, w_xq, w_xkv,
           w_xo, g_ffn, w_pq, sub_keys, u_experts, v_experts, g_final):
    depth = g_mix.shape[0]
    assert depth == 1, "the final norm is fused into the single layer's last kernel"
    l = 0
    return _layer(x, mem, g_mix[l], w_in[l], g_q_lora[l], w_uq[l], g_kv_lora[l], w_ukv[l], g_ret_gn[l], w_o[l],
                  g_xattn[l], g_mem[l], w_xq[l], w_xkv[l], w_xo[l], g_ffn[l], w_pq[l], sub_keys[l],
                  u_experts[l], v_experts[l], g_final)
```

```python
import functools
import math

import jax
import jax.numpy as jnp
from jax import lax
from jax.experimental import pallas as pl
from jax.experimental.pallas import tpu as pltpu

F32 = jnp.float32
BF16 = jnp.bfloat16

LANES = 128
EPS = 1e-6
NEG = -1e30
ROPE_BASE = 10000.0
VMEM_LIMIT = 56 * 1024 * 1024

MLA_HEADS, MLA_Q_RANK, MLA_KV_RANK = 8, 384, 256
MLA_NOPE, MLA_ROPE, MLA_V = 64, 32, 64
RET_HEADS, RET_DK, RET_DV, RET_CHUNK = 4, 128, 128, 128
X_HEADS = 4
PEER_KEYS, PEER_HEADS, PEER_TOPK, PEER_DKEY = 128, 8, 16, 256


def _params(sem, vmem=VMEM_LIMIT):
    return pltpu.CompilerParams(dimension_semantics=sem, vmem_limit_bytes=vmem)


def _rms(x, g):
    return x * lax.rsqrt(jnp.mean(x * x, axis=-1, keepdims=True) + EPS) * g


def _dot(a, b):
    return jnp.dot(a, b, preferred_element_type=F32)


def _dot_nt(a, b):
    return lax.dot_general(a, b, (((1,), (1,)), ((), ())), preferred_element_type=F32)


def _gelu(x):
    return 0.5 * x * (1.0 + lax.erf(x * (2.0 ** -0.5)))


def _const_spec(shape):
    nd = len(shape)
    return pl.BlockSpec(shape, lambda *_: (0,) * nd)


def _in_proj_kernel(x_ref, gmix_ref, win_ref, gq_ref, wuq_ref, gkv_ref, wk_ref, wv_ref,
                    cosm_ref, sinm_ref, cosr_ref, sinr_ref,
                    q_ref, k_ref, v_ref, rq_ref, rk_ref, rv_ref, rg_ref):
    x = x_ref[...]
    a = _rms(x, gmix_ref[...]).astype(BF16)
    proj = _dot(a, win_ref[...])
    o_cq, o_ckv, o_pe, o_rq = 0, MLA_Q_RANK, MLA_Q_RANK + MLA_KV_RANK, MLA_Q_RANK + MLA_KV_RANK + LANES
    rw = RET_HEADS * RET_DK
    cq = proj[:, o_cq:o_ckv]
    ckv = proj[:, o_ckv:o_pe]
    kpe = proj[:, o_pe:o_rq]
    cqn = _rms(cq, gq_ref[...]).astype(BF16)
    ckvn = _rms(ckv, gkv_ref[...]).astype(BF16)
    q = _dot(cqn, wuq_ref[...])
    kn = _dot(ckvn, wk_ref[...])
    vlane = lax.broadcasted_iota(jnp.int32, (x.shape[0], MLA_HEADS * LANES), 1) % LANES
    v_ref[...] = jnp.where(vlane == MLA_V, 1.0, _dot(ckvn, wv_ref[...])).astype(BF16)

    cosm, sinm = cosm_ref[...], sinm_ref[...]
    lane = lax.broadcasted_iota(jnp.int32, (x.shape[0], LANES), 1)
    first_half = lane < MLA_NOPE + MLA_ROPE // 2

    def rope_m(c):
        rot = jnp.where(first_half, pltpu.roll(c, LANES - MLA_ROPE // 2, axis=1),
                        pltpu.roll(c, MLA_ROPE // 2, axis=1))
        return c * cosm + rot * sinm

    qscale = (MLA_NOPE + MLA_ROPE) ** -0.5 * math.log2(math.e)
    kpe_r = rope_m(kpe)
    for h in range(MLA_HEADS):
        sl = slice(h * LANES, (h + 1) * LANES)
        q_ref[:, sl] = (rope_m(q[:, sl]) * qscale).astype(BF16)
        k_ref[:, sl] = (kn[:, sl] + kpe_r).astype(BF16)

    cosr, sinr = cosr_ref[...], sinr_ref[...]
    kscale = RET_DK ** -0.5
    for h in range(RET_HEADS):
        sl = slice(h * LANES, (h + 1) * LANES)
        c = proj[:, o_rq + h * LANES:o_rq + (h + 1) * LANES]
        rq_ref[:, sl] = (c * cosr + pltpu.roll(c, RET_DK // 2, axis=1) * sinr).astype(BF16)
        c = proj[:, o_rq + rw + h * LANES:o_rq + rw + (h + 1) * LANES]
        rk_ref[:, sl] = ((c * cosr + pltpu.roll(c, RET_DK // 2, axis=1) * sinr) * kscale).astype(BF16)
    rv_ref[...] = proj[:, o_rq + 2 * rw:o_rq + 3 * rw].astype(BF16)
    rg_ref[...] = proj[:, o_rq + 3 * rw:o_rq + 4 * rw]


def _in_proj(x2, g_mix, w_in_ext, g_q, w_uq_pad, g_kv, w_k, w_v, cosm, sinm, cosr, sinr, seq, tm):
    T, D = x2.shape
    nS = seq // tm
    tok = lambda w: pl.BlockSpec((tm, w), lambda i: (i, 0))
    pos = lambda w: pl.BlockSpec((tm, w), lambda i: (i % nS, 0))
    outs = [(T, MLA_HEADS * LANES, BF16), (T, MLA_HEADS * LANES, BF16), (T, MLA_HEADS * LANES, BF16),
            (T, RET_HEADS * RET_DK, BF16), (T, RET_HEADS * RET_DK, BF16), (T, RET_HEADS * RET_DV, BF16),
            (T, RET_HEADS * RET_DV, F32)]
    return pl.pallas_call(
        _in_proj_kernel,
        grid=(T // tm,),
        in_specs=[tok(D), _const_spec(g_mix.shape), _const_spec(w_in_ext.shape), _const_spec(g_q.shape),
                  _const_spec(w_uq_pad.shape), _const_spec(g_kv.shape), _const_spec(w_k.shape),
                  _const_spec(w_v.shape), pos(LANES), pos(LANES), pos(LANES), pos(LANES)],
        out_specs=[tok(w) for (_, w, _) in outs],
        out_shape=[jax.ShapeDtypeStruct((t, w), d) for (t, w, d) in outs],
        compiler_params=_params(("parallel",)),
        name="in_proj",
    )(x2, g_mix, w_in_ext, g_q, w_uq_pad, g_kv, w_k, w_v, cosm, sinm, cosr, sinr)


def _mla_attn_kernel(q_ref, k_ref, v_ref, o_ref, *, t):
    i = pl.program_id(2)
    causal = (lax.broadcasted_iota(jnp.int32, (t, t), 1) <= lax.broadcasted_iota(jnp.int32, (t, t), 0))

    def step(j, carry, masked):
        k0 = pl.multiple_of(j * t, t)
        new = []
        for hh in range(2):
            m, acc = carry[hh]
            hs = slice(hh * LANES, (hh + 1) * LANES)
            s = _dot_nt(q_ref[0, :, hs], k_ref[0, pl.ds(k0, t), hs])
            if masked:
                s = jnp.where(causal, s, NEG)
            m_new = jnp.maximum(m, jnp.max(s, axis=-1, keepdims=True))
            p = jnp.exp2(s - m_new).astype(BF16)
            acc = jnp.exp2(m - m_new) * acc + _dot(p, v_ref[0, pl.ds(k0, t), hs])
            new.append((m_new, acc))
        return tuple(new)

    init = tuple((jnp.full((t, 1), NEG, F32), jnp.zeros((t, LANES), F32)) for _ in range(2))

    def pair(pi, carry):
        return step(2 * pi + 1, step(2 * pi, carry, False), False)

    carry = lax.fori_loop(0, i // 2, pair, init)
    carry = lax.cond(i % 2 == 1, lambda c: step(i - 1, c, False), lambda c: c, carry)
    carry = step(i, carry, True)
    outs = [acc[:, :MLA_V] / acc[:, MLA_V:MLA_V + 1] for (_, acc) in carry]
    o_ref[0] = jnp.concatenate(outs, axis=-1).astype(o_ref.dtype)


def _mla_attn(q, k, v, t):
    B, S, _ = q.shape
    return pl.pallas_call(
        functools.partial(_mla_attn_kernel, t=t),
        grid=(B, MLA_HEADS // 2, S // t),
        in_specs=[pl.BlockSpec((1, t, 2 * LANES), lambda b, h, i: (b, i, h)),
                  pl.BlockSpec((1, S, 2 * LANES), lambda b, h, i: (b, 0, h)),
                  pl.BlockSpec((1, S, 2 * LANES), lambda b, h, i: (b, 0, h))],
        out_specs=pl.BlockSpec((1, t, 2 * MLA_V), lambda b, h, i: (b, i, h)),
        out_shape=jax.ShapeDtypeStruct((B, S, MLA_HEADS * MLA_V), BF16),
        compiler_params=_params(("parallel", "parallel", "arbitrary")),
        name="mla_attn",
    )(q, k, v)


def _retention_kernel(rq_ref, rk_ref, rv_ref, rg_ref, gn_ref, decay_ref, zeta_ref, xi_ref, cd_ref,
                      o_ref, r_ref, *, n_chunks):
    C = RET_CHUNK

    @pl.when(pl.program_id(1) == 0)
    def _():
        r_ref[...] = jnp.zeros(r_ref.shape, F32)

    def chunk(n, carry):
        r0 = pl.multiple_of(n * C, C)
        for h in range(RET_HEADS):
            hs = slice(h * LANES, (h + 1) * LANES)
            qc = rq_ref[0, pl.ds(r0, C), hs]
            kc = rk_ref[0, pl.ds(r0, C), hs]
            vc = rv_ref[0, pl.ds(r0, C), hs]
            state = r_ref[h]
            sc = _dot_nt(qc, kc) * decay_ref[h]
            inner = _dot(sc.astype(BF16), vc)
            cross = _dot(qc, state.astype(BF16)) * xi_ref[h]
            o = inner + cross
            kz = (kc.astype(F32) * zeta_ref[h]).T.astype(BF16)
            r_ref[h] = state * cd_ref[h] + _dot(kz, vc)
            mu = jnp.mean(o, axis=-1, keepdims=True)
            d = o - mu
            var = jnp.mean(d * d, axis=-1, keepdims=True)
            on = d * lax.rsqrt(var + EPS) * gn_ref[:, hs]
            g = rg_ref[0, pl.ds(r0, C), hs]
            o_ref[0, pl.ds(r0, C), hs] = (g * jax.nn.sigmoid(g) * on).astype(o_ref.dtype)
        return carry

    lax.fori_loop(0, n_chunks, chunk, 0)


def _retention(rq, rk, rv, rg, g_gn, decay, zeta, xi, cd, ts):
    B, S, W = rq.shape
    tok = pl.BlockSpec((1, ts, W), lambda b, i: (b, i, 0))
    return pl.pallas_call(
        functools.partial(_retention_kernel, n_chunks=ts // RET_CHUNK),
        grid=(B, S // ts),
        in_specs=[tok, tok, tok, tok, _const_spec(g_gn.shape), _const_spec(decay.shape),
                  _const_spec(zeta.shape), _const_spec(xi.shape), _const_spec(cd.shape)],
        out_specs=tok,
        out_shape=jax.ShapeDtypeStruct((B, S, W), BF16),
        scratch_shapes=[pltpu.VMEM((RET_HEADS, RET_DK, RET_DV), F32)],
        compiler_params=_params(("parallel", "arbitrary")),
        name="retention",
    )(rq, rk, rv, rg, g_gn, decay, zeta, xi, cd)


def _mem_kv_kernel(mem_ref, g_ref, w_ref, k_ref, v_ref):
    D = mem_ref.shape[-1]
    mn = _rms(mem_ref[0], g_ref[...]).astype(BF16)
    kv = _dot(mn, w_ref[...])
    k_ref[0] = kv[:, :D].astype(BF16)
    v_ref[0] = kv[:, D:].astype(BF16)


def _mem_kv(mem, g_mem, w_xkv):
    B, M, D = mem.shape
    blk = pl.BlockSpec((1, M, D), lambda b: (b, 0, 0))
    return pl.pallas_call(
        _mem_kv_kernel,
        grid=(B,),
        in_specs=[blk, _const_spec(g_mem.shape), _const_spec(w_xkv.shape)],
        out_specs=[blk, blk],
        out_shape=[jax.ShapeDtypeStruct((B, M, D), BF16)] * 2,
        compiler_params=_params(("parallel",)),
        name="mem_kv",
    )(mem, g_mem, w_xkv)


def _post_mix_kernel(x_ref, omla_ref, oret_ref, woa_ref, wob_ref, gx_ref, wxq_ref, kx_ref, vx_ref, wxo_ref,
                     gf_ref, wpq_ref, keys_ref, h2_ref, a3t_ref, st_ref, ox_ref):
    D = x_ref.shape[-1]
    h1 = x_ref[...] + _dot(omla_ref[...], woa_ref[...]) + _dot(oret_ref[...], wob_ref[...])
    a2 = _rms(h1, gx_ref[...]).astype(BF16)
    qx = _dot(a2, wxq_ref[...])
    hd = D // X_HEADS
    for h in range(X_HEADS):
        sl = slice(h * hd, (h + 1) * hd)
        s = _dot_nt(qx[:, sl].astype(BF16), kx_ref[0, :, sl]) * (hd ** -0.5)
        m = jnp.max(s, axis=-1, keepdims=True)
        p = jnp.exp(s - m)
        p = p / jnp.sum(p, axis=-1, keepdims=True)
        ox_ref[:, sl] = _dot(p.astype(BF16), vx_ref[0, :, sl]).astype(BF16)
    h2 = h1 + _dot(ox_ref[...], wxo_ref[...])
    h2_ref[...] = h2
    a3 = _rms(h2, gf_ref[...])
    a3t_ref[...] = a3.T.astype(BF16)
    qp = _dot(a3.astype(BF16), wpq_ref[...]).astype(BF16)
    half = PEER_DKEY // 2
    for hp in range(2 * PEER_HEADS):
        sc = _dot_nt(keys_ref[hp % 2], qp[:, hp * half:(hp + 1) * half])
        for tl in range(sc.shape[1] // LANES):
            st_ref[hp, tl] = sc[:, tl * LANES:(tl + 1) * LANES]


def _post_mix(x2, o_mla, o_ret, w_oa, w_ob, g_x, w_xq, kx, vx, w_xo, g_f, w_pq, keys, seq, tm):
    T, D = x2.shape
    nS = seq // tm
    tok = lambda w: pl.BlockSpec((tm, w), lambda i: (i, 0))
    memb = pl.BlockSpec((1,) + kx.shape[1:], lambda i: (i // nS, 0, 0))
    return pl.pallas_call(
        _post_mix_kernel,
        grid=(T // tm,),
        in_specs=[tok(D), tok(o_mla.shape[1]), tok(o_ret.shape[1]), _const_spec(w_oa.shape),
                  _const_spec(w_ob.shape), _const_spec(g_x.shape), _const_spec(w_xq.shape), memb, memb,
                  _const_spec(w_xo.shape), _const_spec(g_f.shape), _const_spec(w_pq.shape),
                  _const_spec(keys.shape)],
        out_specs=[tok(D), pl.BlockSpec((D, tm), lambda i: (0, i)),
                   pl.BlockSpec((2 * PEER_HEADS, tm // LANES, PEER_KEYS, LANES), lambda i: (0, i, 0, 0))],
        out_shape=[jax.ShapeDtypeStruct((T, D), F32), jax.ShapeDtypeStruct((D, T), BF16),
                   jax.ShapeDtypeStruct((2 * PEER_HEADS, T // LANES, PEER_KEYS, LANES), F32)],
        scratch_shapes=[pltpu.VMEM((tm, D), BF16)],
        compiler_params=_params(("parallel",)),
        name="post_mix",
    )(x2, o_mla, o_ret, w_oa, w_ob, g_x, w_xq, kx, vx, w_xo, g_f, w_pq, keys)


N_TOP = PEER_TOPK + 1
TOP_ROWS = 24
SUBLANES = 8


def _sort_network(n):
    def merge(lo, hi, r):
        step = r * 2
        if step < hi - lo:
            yield from merge(lo, hi, step)
            yield from merge(lo + r, hi, step)
            yield from [(i, i + r) for i in range(lo + r, hi - r, step)]
        else:
            yield (lo, lo + r)

    def sort(lo, hi):
        if hi - lo >= 1:
            mid = lo + (hi - lo) // 2
            yield from sort(lo, mid)
            yield from sort(mid + 1, hi)
            yield from merge(lo, hi, 1)

    return list(sort(0, n - 1))


def _top17(x):
    k = PEER_TOPK
    v = [x[SUBLANES * i:SUBLANES * (i + 1), :] for i in range(x.shape[0] // SUBLANES)]
    assert len(v) == k
    for i, j in _sort_network(k):
        v[i], v[j] = jnp.maximum(v[i], v[j]), jnp.minimum(v[i], v[j])
    dropped = jnp.full(v[0].shape, NEG, F32)
    for shift in (SUBLANES // 2, SUBLANES // 4, SUBLANES // 8):
        other = [pltpu.roll(a, SUBLANES - shift, axis=0) for a in v]
        lose = [jnp.minimum(v[i], other[k - 1 - i]) for i in range(k)]
        v = [jnp.maximum(v[i], other[k - 1 - i]) for i in range(k)]
        for a in lose + [pltpu.roll(dropped, SUBLANES - shift, axis=0)]:
            dropped = jnp.maximum(dropped, a)
        d = k // 2
        while d >= 1:
            for i in range(k):
                if i & d == 0:
                    v[i], v[i + d] = jnp.maximum(v[i], v[i + d]), jnp.minimum(v[i], v[i + d])
            d //= 2
    return [a[0:1, :] for a in v], dropped[0:1, :]


def _peer_route_kernel(st_ref, stats_ref, vs_ref):
    tl = st_ref.shape[-1]
    vs_ref[...] = jnp.full(vs_ref.shape, NEG, F32)
    for h in range(PEER_HEADS):
        for p in range(2):
            best, nxt = _top17(st_ref[2 * h + p, 0])
            for it, row in enumerate(best + [nxt]):
                vs_ref[p, it:it + 1, :] = row
        v1 = lambda i: vs_ref[0, i:i + 1, :]
        v2 = lambda i: vs_ref[1, i:i + 1, :]
        cands = [v1(0) + vs_ref[1, 8 * r:8 * r + 8, :] for r in range(TOP_ROWS // 8)]
        cands += [v1(i) + vs_ref[1, 0:8, :] for i in range(1, 8)]
        cands += [vs_ref[0, 8 * r:8 * r + 8, :] + v2(0) for r in range(1, TOP_ROWS // 8)]
        cands += [jnp.full((SUBLANES, tl), NEG, F32)] * (PEER_KEYS // SUBLANES - len(cands))
        best, c17 = _top17(jnp.concatenate(cands, axis=0))
        top = best[0]
        z = jnp.zeros((1, tl), F32)
        for m in best:
            z = z + jnp.exp(m - top)
        stats_ref[0, h:h + 1, :] = 0.5 * (best[-1] + c17)
        stats_ref[1, h:h + 1, :] = v1(0)
        stats_ref[2, h:h + 1, :] = v2(0)
        stats_ref[3, h:h + 1, :] = 1.0 / z


def _peer_route(st):
    HP, nb, K, tl = st.shape
    T = nb * tl
    return pl.pallas_call(
        _peer_route_kernel,
        grid=(nb,),
        in_specs=[pl.BlockSpec((HP, 1, K, tl), lambda i: (0, i, 0, 0))],
        out_specs=pl.BlockSpec((4, PEER_HEADS, tl), lambda i: (0, 0, i)),
        out_shape=jax.ShapeDtypeStruct((4, PEER_HEADS, T), F32),
        scratch_shapes=[pltpu.VMEM((2, TOP_ROWS, tl), F32)],
        compiler_params=_params(("parallel",)),
        name="peer_route",
    )(st)


def _peer_dense_kernel(a3t_ref, st_ref, stats_ref, u_ref, v_ref, h2_ref, g_ref, o_ref,
                       thr_ref, c1_ref, e2_ref, acc_ref, *, te, tm):
    j = pl.program_id(1)
    nk = PEER_KEYS
    n_al = te // nk
    nt = tm // LANES

    @pl.when(j == 0)
    def _():
        acc_ref[...] = jnp.zeros(acc_ref.shape, F32)
        for h in range(PEER_HEADS):
            for tl in range(nt):
                ls = slice(tl * LANES, (tl + 1) * LANES)
                s1 = st_ref[2 * h, tl]
                thr_ref[h, :, tl, :] = stats_ref[0, h:h + 1, ls] - s1
                c1_ref[h, :, tl, :] = jnp.exp(s1 - stats_ref[1, h:h + 1, ls]) * stats_ref[3, h:h + 1, ls]
                e2_ref[h, tl] = jnp.exp(st_ref[2 * h + 1, tl] - stats_ref[2, h:h + 1, ls])

    act = _dot(u_ref[...], a3t_ref[...])
    p_rows = []
    for al in range(n_al):
        a = j * n_al + al
        rs = slice(al * nk, (al + 1) * nk)
        tiles = []
        for tl in range(nt):
            w = jnp.zeros((nk, LANES), F32)
            for h in range(PEER_HEADS):
                thr = thr_ref[h, a, tl:tl + 1, :]
                c1 = c1_ref[h, a, tl:tl + 1, :]
                w = w + jnp.where(st_ref[2 * h + 1, tl] >= thr, e2_ref[h, tl] * c1, 0.0)
            tiles.append((_gelu(act[rs, tl * LANES:(tl + 1) * LANES]) * w).astype(BF16))
        p_rows.append(jnp.concatenate(tiles, axis=1))
    p = jnp.concatenate(p_rows, axis=0)
    acc_ref[...] += lax.dot_general(v_ref[...], p, (((0,), (0,)), ((), ())), preferred_element_type=F32)

    @pl.when(j == pl.num_programs(1) - 1)
    def _():
        o_ref[...] = _rms(h2_ref[...] + acc_ref[...].T, g_ref[...])


def _peer_dense(a3t, st, stats, u_bf, v_bf, h2, g, tm, te):
    D, T = a3t.shape
    nt = tm // LANES
    return pl.pallas_call(
        functools.partial(_peer_dense_kernel, te=te, tm=tm),
        grid=(T // tm, u_bf.shape[0] // te),
        in_specs=[pl.BlockSpec((D, tm), lambda i, j: (0, i)),
                  pl.BlockSpec((st.shape[0], nt) + st.shape[2:], lambda i, j: (0, i, 0, 0)),
                  pl.BlockSpec(stats.shape[:2] + (tm,), lambda i, j: (0, 0, i)),
                  pl.BlockSpec((te, D), lambda i, j: (j, 0)),
                  pl.BlockSpec((te, D), lambda i, j: (j, 0)),
                  pl.BlockSpec((tm, D), lambda i, j: (i, 0)), _const_spec(g.shape)],
        out_specs=pl.BlockSpec((tm, D), lambda i, j: (i, 0)),
        out_shape=jax.ShapeDtypeStruct((T, D), F32),
        scratch_shapes=[pltpu.VMEM((PEER_HEADS, PEER_KEYS, nt, LANES), F32)] * 2
        + [pltpu.VMEM((PEER_HEADS, nt, PEER_KEYS, LANES), F32), pltpu.VMEM((D, tm), F32)],
        compiler_params=_params(("parallel", "arbitrary")),
        name="peer_dense",
    )(a3t, st, stats, u_bf, v_bf, h2, g)


def _rope_tables(seq):
    pos = jnp.arange(seq, dtype=F32)

    def tab(dim):
        inv = 1.0 / (ROPE_BASE ** (jnp.arange(0, dim, 2, dtype=F32) / dim))
        ang = pos[:, None] * inv[None, :]
        return jnp.cos(ang), jnp.sin(ang)

    cm, sm = tab(MLA_ROPE)
    pad = LANES - MLA_NOPE - MLA_ROPE
    cosm = jnp.concatenate([jnp.ones((seq, MLA_NOPE), F32), cm, cm, jnp.zeros((seq, pad), F32)], axis=1)
    sinm = jnp.concatenate([jnp.zeros((seq, MLA_NOPE), F32), -sm, sm, jnp.zeros((seq, pad), F32)], axis=1)
    cr, sr = tab(RET_DK)
    cosr = jnp.concatenate([cr, cr], axis=1)
    sinr = jnp.concatenate([-sr, sr], axis=1)
    return cosm, sinm, cosr, sinr


def _retention_tables():
    C = RET_CHUNK
    log_g = jnp.log(1.0 - 2.0 ** (-5.0 - jnp.arange(RET_HEADS, dtype=F32)))
    idx = jnp.arange(C, dtype=F32)
    rel = idx[:, None] - idx[None, :]
    decay = jnp.where(rel[None] >= 0, jnp.exp(jnp.maximum(rel, 0.0)[None] * log_g[:, None, None]), 0.0)
    zeta = jnp.exp((C - 1 - idx)[None, :] * log_g[:, None])
    xi = jnp.exp((idx + 1)[None, :] * log_g[:, None])
    cd = jnp.exp(C * log_g)
    bc = lambda t: jnp.broadcast_to(t[:, :, None], (RET_HEADS, C, LANES))
    return decay, bc(zeta), bc(xi), jnp.broadcast_to(cd[:, None, None], (RET_HEADS, C, LANES))


def _layer(h, mem, g_mix, w_in, g_q_lora, w_uq, g_kv_lora, w_ukv, g_ret_gn, w_o, g_xattn, g_mem,
           w_xq, w_xkv, w_xo, g_ffn, w_pq, sub_keys, u_experts, v_experts, g_out):
    B, S, D = h.shape
    T = B * S
    row = lambda g: g.reshape(1, -1)

    o_pe = MLA_Q_RANK + MLA_KV_RANK
    pe_pad = jnp.zeros((D, LANES), F32).at[:, MLA_NOPE:MLA_NOPE + MLA_ROPE].set(w_in[:, o_pe:o_pe + MLA_ROPE])
    w_in_ext = jnp.concatenate([w_in[:, :o_pe], pe_pad, w_in[:, o_pe + MLA_ROPE:]], axis=1).astype(BF16)
    dq = MLA_NOPE + MLA_ROPE
    w_uq_pad = jnp.pad(w_uq.reshape(MLA_Q_RANK, MLA_HEADS, dq), ((0, 0), (0, 0), (0, LANES - dq)))
    w_uq_pad = w_uq_pad.reshape(MLA_Q_RANK, MLA_HEADS * LANES).astype(BF16)
    w_ukv3 = w_ukv.reshape(MLA_KV_RANK, MLA_HEADS, MLA_NOPE + MLA_V)
    w_k = jnp.pad(w_ukv3[:, :, :MLA_NOPE], ((0, 0), (0, 0), (0, LANES - MLA_NOPE)))
    w_k = w_k.reshape(MLA_KV_RANK, MLA_HEADS * LANES).astype(BF16)
    w_v = jnp.pad(w_ukv3[:, :, MLA_NOPE:], ((0, 0), (0, 0), (0, LANES - MLA_V)))
    w_v = w_v.reshape(MLA_KV_RANK, MLA_HEADS * LANES).astype(BF16)

    cosm, sinm, cosr, sinr = _rope_tables(S)
    q, k, v, rq, rk, rv, rg = _in_proj(h.reshape(T, D), row(g_mix), w_in_ext, row(g_q_lora), w_uq_pad,
                                       row(g_kv_lora), w_k, w_v, cosm, sinm, cosr, sinr, S, min(512, S))
    r3 = lambda t: t.reshape(B, S, t.shape[-1])
    o_mla = _mla_attn(r3(q), r3(k), r3(v), min(512, S))
    o_ret = _retention(r3(rq), r3(rk), r3(rv), r3(rg), row(g_ret_gn), *_retention_tables(), min(1024, S))

    kx, vx = _mem_kv(mem, row(g_mem), w_xkv.astype(BF16))
    n_mla = MLA_HEADS * MLA_V
    w_o_bf = w_o.astype(BF16)
    h2, a3t, st = _post_mix(h.reshape(T, D), o_mla.reshape(T, -1), o_ret.reshape(T, -1), w_o_bf[:n_mla],
                            w_o_bf[n_mla:], row(g_xattn), w_xq.astype(BF16), kx, vx, w_xo.astype(BF16),
                            row(g_ffn), w_pq.astype(BF16), sub_keys.astype(BF16), S, min(512, S))
    stats = _peer_route(st)
    out = _peer_dense(a3t, st, stats, u_experts.astype(BF16), v_experts.astype(BF16), h2, row(g_out),
                      min(512, T), 2048)
    return out.reshape(B, S, D)


def kernel(x, mem, g_mix, w_in, g_q_lora, w_uq, g_kv_lora, w_ukv, g_ret_gn, w_o, g_xattn, g_mem, w_xq, w_xkv,
           w_xo, g_ffn, w_pq, sub_keys, u_experts, v_experts, g_final):
    depth = g_mix.shape[0]
    assert depth == 1, "the final norm is fused into the single layer's last kernel"
    l = 0
    return _layer(x, mem, g_mix[l], w_in[l], g_q_lora[l], w_uq[l], g_kv_lora[l], w_ukv[l], g_ret_gn[l], w_o[l],
                  g_xattn[l], g_mem[l], w_xq[l], w_xkv[l], w_xo[l], g_ffn[l], w_pq[l], sub_keys[l],
                  u_experts[l], v_experts[l], g_final)
```

```python
import functools
import math

import jax
import jax.numpy as jnp
from jax import lax
from jax.experimental import pallas as pl
from jax.experimental.pallas import tpu as pltpu

F32 = jnp.float32
BF16 = jnp.bfloat16

LANES = 128
EPS = 1e-6
NEG = -1e30
ROPE_BASE = 10000.0
VMEM_LIMIT = 56 * 1024 * 1024

MLA_HEADS, MLA_Q_RANK, MLA_KV_RANK = 8, 384, 256
MLA_NOPE, MLA_ROPE, MLA_V = 64, 32, 64
RET_HEADS, RET_DK, RET_DV, RET_CHUNK = 4, 128, 128, 128
X_HEADS = 4
PEER_KEYS, PEER_HEADS, PEER_TOPK, PEER_DKEY = 128, 8, 16, 256


def _params(sem, vmem=VMEM_LIMIT):
    return pltpu.CompilerParams(dimension_semantics=sem, vmem_limit_bytes=vmem)


def _rms(x, g):
    return x * lax.rsqrt(jnp.mean(x * x, axis=-1, keepdims=True) + EPS) * g


def _dot(a, b):
    return jnp.dot(a, b, preferred_element_type=F32)


def _dot_nt(a, b):
    return lax.dot_general(a, b, (((1,), (1,)), ((), ())), preferred_element_type=F32)


def _gelu(x):
    return 0.5 * x * (1.0 + lax.erf(x * (2.0 ** -0.5)))


def _const_spec(shape):
    nd = len(shape)
    return pl.BlockSpec(shape, lambda *_: (0,) * nd)


def _in_proj_kernel(x_ref, gmix_ref, win_ref, gq_ref, wuq_ref, gkv_ref, wk_ref, wv_ref,
                    cosm_ref, sinm_ref, cosr_ref, sinr_ref,
                    q_ref, k_ref, v_ref, rq_ref, rk_ref, rv_ref, rg_ref):
    x = x_ref[...]
    a = _rms(x, gmix_ref[...]).astype(BF16)
    proj = _dot(a, win_ref[...])
    o_cq, o_ckv, o_pe, o_rq = 0, MLA_Q_RANK, MLA_Q_RANK + MLA_KV_RANK, MLA_Q_RANK + MLA_KV_RANK + LANES
    rw = RET_HEADS * RET_DK
    cq = proj[:, o_cq:o_ckv]
    ckv = proj[:, o_ckv:o_pe]
    kpe = proj[:, o_pe:o_rq]
    cqn = _rms(cq, gq_ref[...]).astype(BF16)
    ckvn = _rms(ckv, gkv_ref[...]).astype(BF16)
    q = _dot(cqn, wuq_ref[...])
    kn = _dot(ckvn, wk_ref[...])
    vlane = lax.broadcasted_iota(jnp.int32, (x.shape[0], MLA_HEADS * LANES), 1) % LANES
    v_ref[...] = jnp.where(vlane == MLA_V, 1.0, _dot(ckvn, wv_ref[...])).astype(BF16)

    cosm, sinm = cosm_ref[...], sinm_ref[...]
    lane = lax.broadcasted_iota(jnp.int32, (x.shape[0], LANES), 1)
    first_half = lane < MLA_NOPE + MLA_ROPE // 2

    def rope_m(c):
        rot = jnp.where(first_half, pltpu.roll(c, LANES - MLA_ROPE // 2, axis=1),
                        pltpu.roll(c, MLA_ROPE // 2, axis=1))
        return c * cosm + rot * sinm

    qscale = (MLA_NOPE + MLA_ROPE) ** -0.5 * math.log2(math.e)
    kpe_r = rope_m(kpe)
    for h in range(MLA_HEADS):
        sl = slice(h * LANES, (h + 1) * LANES)
        q_ref[:, sl] = (rope_m(q[:, sl]) * qscale).astype(BF16)
        k_ref[:, sl] = (kn[:, sl] + kpe_r).astype(BF16)

    cosr, sinr = cosr_ref[...], sinr_ref[...]
    kscale = RET_DK ** -0.5
    for h in range(RET_HEADS):
        sl = slice(h * LANES, (h + 1) * LANES)
        c = proj[:, o_rq + h * LANES:o_rq + (h + 1) * LANES]
        rq_ref[:, sl] = (c * cosr + pltpu.roll(c, RET_DK // 2, axis=1) * sinr).astype(BF16)
        c = proj[:, o_rq + rw + h * LANES:o_rq + rw + (h + 1) * LANES]
        rk_ref[:, sl] = ((c * cosr + pltpu.roll(c, RET_DK // 2, axis=1) * sinr) * kscale).astype(BF16)
    rv_ref[...] = proj[:, o_rq + 2 * rw:o_rq + 3 * rw].astype(BF16)
    rg_ref[...] = proj[:, o_rq + 3 * rw:o_rq + 4 * rw]


def _in_proj(x2, g_mix, w_in_ext, g_q, w_uq_pad, g_kv, w_k, w_v, cosm, sinm, cosr, sinr, seq, tm):
    T, D = x2.shape
    nS = seq // tm
    tok = lambda w: pl.BlockSpec((tm, w), lambda i: (i, 0))
    pos = lambda w: pl.BlockSpec((tm, w), lambda i: (i % nS, 0))
    outs = [(T, MLA_HEADS * LANES, BF16), (T, MLA_HEADS * LANES, BF16), (T, MLA_HEADS * LANES, BF16),
            (T, RET_HEADS * RET_DK, BF16), (T, RET_HEADS * RET_DK, BF16), (T, RET_HEADS * RET_DV, BF16),
            (T, RET_HEADS * RET_DV, F32)]
    return pl.pallas_call(
        _in_proj_kernel,
        grid=(T // tm,),
        in_specs=[tok(D), _const_spec(g_mix.shape), _const_spec(w_in_ext.shape), _const_spec(g_q.shape),
                  _const_spec(w_uq_pad.shape), _const_spec(g_kv.shape), _const_spec(w_k.shape),
                  _const_spec(w_v.shape), pos(LANES), pos(LANES), pos(LANES), pos(LANES)],
        out_specs=[tok(w) for (_, w, _) in outs],
        out_shape=[jax.ShapeDtypeStruct((t, w), d) for (t, w, d) in outs],
        compiler_params=_params(("parallel",)),
        name="in_proj",
    )(x2, g_mix, w_in_ext, g_q, w_uq_pad, g_kv, w_k, w_v, cosm, sinm, cosr, sinr)


def _mla_attn_kernel(q_ref, k_ref, v_ref, o_ref, *, t):
    i = pl.program_id(2)
    causal = (lax.broadcasted_iota(jnp.int32, (t, t), 1) <= lax.broadcasted_iota(jnp.int32, (t, t), 0))

    def step(j, carry, masked):
        k0 = pl.multiple_of(j * t, t)
        new = []
        for hh in range(2):
            m, acc = carry[hh]
            hs = slice(hh * LANES, (hh + 1) * LANES)
            s = _dot_nt(q_ref[0, :, hs], k_ref[0, pl.ds(k0, t), hs])
            if masked:
                s = jnp.where(causal, s, NEG)
            m_new = jnp.maximum(m, jnp.max(s, axis=-1, keepdims=True))
            p = jnp.exp2(s - m_new).astype(BF16)
            acc = jnp.exp2(m - m_new) * acc + _dot(p, v_ref[0, pl.ds(k0, t), hs])
            new.append((m_new, acc))
        return tuple(new)

    init = tuple((jnp.full((t, 1), NEG, F32), jnp.zeros((t, LANES), F32)) for _ in range(2))
    carry = lax.fori_loop(0, i, functools.partial(step, masked=False), init)
    carry = step(i, carry, True)
    outs = [acc[:, :MLA_V] / acc[:, MLA_V:MLA_V + 1] for (_, acc) in carry]
    o_ref[0] = jnp.concatenate(outs, axis=-1).astype(o_ref.dtype)


def _mla_attn(q, k, v, t):
    B, S, _ = q.shape
    return pl.pallas_call(
        functools.partial(_mla_attn_kernel, t=t),
        grid=(B, MLA_HEADS // 2, S // t),
        in_specs=[pl.BlockSpec((1, t, 2 * LANES), lambda b, h, i: (b, i, h)),
                  pl.BlockSpec((1, S, 2 * LANES), lambda b, h, i: (b, 0, h)),
                  pl.BlockSpec((1, S, 2 * LANES), lambda b, h, i: (b, 0, h))],
        out_specs=pl.BlockSpec((1, t, 2 * MLA_V), lambda b, h, i: (b, i, h)),
        out_shape=jax.ShapeDtypeStruct((B, S, MLA_HEADS * MLA_V), BF16),
        compiler_params=_params(("parallel", "parallel", "arbitrary")),
        name="mla_attn",
    )(q, k, v)


def _retention_kernel(rq_ref, rk_ref, rv_ref, rg_ref, gn_ref, decay_ref, zeta_ref, xi_ref, cd_ref,
                      o_ref, r_ref, *, n_chunks):
    C = RET_CHUNK

    @pl.when(pl.program_id(1) == 0)
    def _():
        r_ref[...] = jnp.zeros(r_ref.shape, F32)

    def chunk(n, carry):
        r0 = pl.multiple_of(n * C, C)
        for h in range(RET_HEADS):
            hs = slice(h * LANES, (h + 1) * LANES)
            qc = rq_ref[0, pl.ds(r0, C), hs]
            kc = rk_ref[0, pl.ds(r0, C), hs]
            vc = rv_ref[0, pl.ds(r0, C), hs]
            state = r_ref[h]
            sc = _dot_nt(qc, kc) * decay_ref[h]
            inner = _dot(sc.astype(BF16), vc)
            cross = _dot(qc, state.astype(BF16)) * xi_ref[h]
            o = inner + cross
            kz = (kc.astype(F32) * zeta_ref[h]).T.astype(BF16)
            r_ref[h] = state * cd_ref[h] + _dot(kz, vc)
            mu = jnp.mean(o, axis=-1, keepdims=True)
            d = o - mu
            var = jnp.mean(d * d, axis=-1, keepdims=True)
            on = d * lax.rsqrt(var + EPS) * gn_ref[:, hs]
            g = rg_ref[0, pl.ds(r0, C), hs]
            o_ref[0, pl.ds(r0, C), hs] = (g * jax.nn.sigmoid(g) * on).astype(o_ref.dtype)
        return carry

    lax.fori_loop(0, n_chunks, chunk, 0)


def _retention(rq, rk, rv, rg, g_gn, decay, zeta, xi, cd, ts):
    B, S, W = rq.shape
    tok = pl.BlockSpec((1, ts, W), lambda b, i: (b, i, 0))
    return pl.pallas_call(
        functools.partial(_retention_kernel, n_chunks=ts // RET_CHUNK),
        grid=(B, S // ts),
        in_specs=[tok, tok, tok, tok, _const_spec(g_gn.shape), _const_spec(decay.shape),
                  _const_spec(zeta.shape), _const_spec(xi.shape), _const_spec(cd.shape)],
        out_specs=tok,
        out_shape=jax.ShapeDtypeStruct((B, S, W), BF16),
        scratch_shapes=[pltpu.VMEM((RET_HEADS, RET_DK, RET_DV), F32)],
        compiler_params=_params(("parallel", "arbitrary")),
        name="retention",
    )(rq, rk, rv, rg, g_gn, decay, zeta, xi, cd)


def _mem_kv_kernel(mem_ref, g_ref, w_ref, k_ref, v_ref):
    D = mem_ref.shape[-1]
    mn = _rms(mem_ref[0], g_ref[...]).astype(BF16)
    kv = _dot(mn, w_ref[...])
    k_ref[0] = kv[:, :D].astype(BF16)
    v_ref[0] = kv[:, D:].astype(BF16)


def _mem_kv(mem, g_mem, w_xkv):
    B, M, D = mem.shape
    blk = pl.BlockSpec((1, M, D), lambda b: (b, 0, 0))
    return pl.pallas_call(
        _mem_kv_kernel,
        grid=(B,),
        in_specs=[blk, _const_spec(g_mem.shape), _const_spec(w_xkv.shape)],
        out_specs=[blk, blk],
        out_shape=[jax.ShapeDtypeStruct((B, M, D), BF16)] * 2,
        compiler_params=_params(("parallel",)),
        name="mem_kv",
    )(mem, g_mem, w_xkv)


def _post_mix_kernel(x_ref, omla_ref, oret_ref, woa_ref, wob_ref, gx_ref, wxq_ref, kx_ref, vx_ref, wxo_ref,
                     gf_ref, wpq_ref, keys_ref, h2_ref, a3t_ref, st_ref, ox_ref):
    D = x_ref.shape[-1]
    h1 = x_ref[...] + _dot(omla_ref[...], woa_ref[...]) + _dot(oret_ref[...], wob_ref[...])
    a2 = _rms(h1, gx_ref[...]).astype(BF16)
    qx = _dot(a2, wxq_ref[...])
    hd = D // X_HEADS
    for h in range(X_HEADS):
        sl = slice(h * hd, (h + 1) * hd)
        s = _dot_nt(qx[:, sl].astype(BF16), kx_ref[0, :, sl]) * (hd ** -0.5)
        m = jnp.max(s, axis=-1, keepdims=True)
        p = jnp.exp(s - m)
        p = p / jnp.sum(p, axis=-1, keepdims=True)
        ox_ref[:, sl] = _dot(p.astype(BF16), vx_ref[0, :, sl]).astype(BF16)
    h2 = h1 + _dot(ox_ref[...], wxo_ref[...])
    h2_ref[...] = h2
    a3 = _rms(h2, gf_ref[...])
    a3t_ref[...] = a3.T.astype(BF16)
    qp = _dot(a3.astype(BF16), wpq_ref[...]).astype(BF16)
    half = PEER_DKEY // 2
    for hp in range(2 * PEER_HEADS):
        sc = _dot_nt(keys_ref[hp % 2], qp[:, hp * half:(hp + 1) * half])
        for tl in range(sc.shape[1] // LANES):
            st_ref[hp, tl] = sc[:, tl * LANES:(tl + 1) * LANES]


def _post_mix(x2, o_mla, o_ret, w_oa, w_ob, g_x, w_xq, kx, vx, w_xo, g_f, w_pq, keys, seq, tm):
    T, D = x2.shape
    nS = seq // tm
    tok = lambda w: pl.BlockSpec((tm, w), lambda i: (i, 0))
    memb = pl.BlockSpec((1,) + kx.shape[1:], lambda i: (i // nS, 0, 0))
    return pl.pallas_call(
        _post_mix_kernel,
        grid=(T // tm,),
        in_specs=[tok(D), tok(o_mla.shape[1]), tok(o_ret.shape[1]), _const_spec(w_oa.shape),
                  _const_spec(w_ob.shape), _const_spec(g_x.shape), _const_spec(w_xq.shape), memb, memb,
                  _const_spec(w_xo.shape), _const_spec(g_f.shape), _const_spec(w_pq.shape),
                  _const_spec(keys.shape)],
        out_specs=[tok(D), pl.BlockSpec((D, tm), lambda i: (0, i)),
                   pl.BlockSpec((2 * PEER_HEADS, tm // LANES, PEER_KEYS, LANES), lambda i: (0, i, 0, 0))],
        out_shape=[jax.ShapeDtypeStruct((T, D), F32), jax.ShapeDtypeStruct((D, T), BF16),
                   jax.ShapeDtypeStruct((2 * PEER_HEADS, T // LANES, PEER_KEYS, LANES), F32)],
        scratch_shapes=[pltpu.VMEM((tm, D), BF16)],
        compiler_params=_params(("parallel",)),
        name="post_mix",
    )(x2, o_mla, o_ret, w_oa, w_ob, g_x, w_xq, kx, vx, w_xo, g_f, w_pq, keys)


N_TOP = PEER_TOPK + 1
TOP_ROWS = 24
SUBLANES = 8


def _sort_network(n):
    def merge(lo, hi, r):
        step = r * 2
        if step < hi - lo:
            yield from merge(lo, hi, step)
            yield from merge(lo + r, hi, step)
            yield from [(i, i + r) for i in range(lo + r, hi - r, step)]
        else:
            yield (lo, lo + r)

    def sort(lo, hi):
        if hi - lo >= 1:
            mid = lo + (hi - lo) // 2
            yield from sort(lo, mid)
            yield from sort(mid + 1, hi)
            yield from merge(lo, hi, 1)

    return list(sort(0, n - 1))


def _top17(x):
    k = PEER_TOPK
    v = [x[SUBLANES * i:SUBLANES * (i + 1), :] for i in range(x.shape[0] // SUBLANES)]
    assert len(v) == k
    for i, j in _sort_network(k):
        v[i], v[j] = jnp.maximum(v[i], v[j]), jnp.minimum(v[i], v[j])
    dropped = jnp.full(v[0].shape, NEG, F32)
    for shift in (SUBLANES // 2, SUBLANES // 4, SUBLANES // 8):
        other = [pltpu.roll(a, SUBLANES - shift, axis=0) for a in v]
        lose = [jnp.minimum(v[i], other[k - 1 - i]) for i in range(k)]
        v = [jnp.maximum(v[i], other[k - 1 - i]) for i in range(k)]
        for a in lose + [pltpu.roll(dropped, SUBLANES - shift, axis=0)]:
            dropped = jnp.maximum(dropped, a)
        d = k // 2
        while d >= 1:
            for i in range(k):
                if i & d == 0:
                    v[i], v[i + d] = jnp.maximum(v[i], v[i + d]), jnp.minimum(v[i], v[i + d])
            d //= 2
    return [a[0:1, :] for a in v], dropped[0:1, :]


def _peer_route_kernel(st_ref, stats_ref, vs_ref):
    tl = st_ref.shape[-1]
    vs_ref[...] = jnp.full(vs_ref.shape, NEG, F32)
    for h in range(PEER_HEADS):
        for p in range(2):
            best, nxt = _top17(st_ref[2 * h + p, 0])
            for it, row in enumerate(best + [nxt]):
                vs_ref[p, it:it + 1, :] = row
        v1 = lambda i: vs_ref[0, i:i + 1, :]
        v2 = lambda i: vs_ref[1, i:i + 1, :]
        cands = [v1(0) + vs_ref[1, 8 * r:8 * r + 8, :] for r in range(TOP_ROWS // 8)]
        cands += [v1(i) + vs_ref[1, 0:8, :] for i in range(1, 8)]
        cands += [vs_ref[0, 8 * r:8 * r + 8, :] + v2(0) for r in range(1, TOP_ROWS // 8)]
        cands += [jnp.full((SUBLANES, tl), NEG, F32)] * (PEER_KEYS // SUBLANES - len(cands))
        best, c17 = _top17(jnp.concatenate(cands, axis=0))
        top = best[0]
        z = jnp.zeros((1, tl), F32)
        for m in best:
            z = z + jnp.exp(m - top)
        stats_ref[0, h:h + 1, :] = 0.5 * (best[-1] + c17)
        stats_ref[1, h:h + 1, :] = v1(0)
        stats_ref[2, h:h + 1, :] = v2(0)
        stats_ref[3, h:h + 1, :] = 1.0 / z


def _peer_route(st):
    HP, nb, K, tl = st.shape
    T = nb * tl
    return pl.pallas_call(
        _peer_route_kernel,
        grid=(nb,),
        in_specs=[pl.BlockSpec((HP, 1, K, tl), lambda i: (0, i, 0, 0))],
        out_specs=pl.BlockSpec((4, PEER_HEADS, tl), lambda i: (0, 0, i)),
        out_shape=jax.ShapeDtypeStruct((4, PEER_HEADS, T), F32),
        scratch_shapes=[pltpu.VMEM((2, TOP_ROWS, tl), F32)],
        compiler_params=_params(("parallel",)),
        name="peer_route",
    )(st)


def _peer_dense_kernel(a3t_ref, st_ref, stats_ref, u_ref, v_ref, h2_ref, g_ref, o_ref,
                       thr_ref, c1_ref, e2_ref, acc_ref, *, te, tm):
    j = pl.program_id(1)
    nk = PEER_KEYS
    n_al = te // nk
    nt = tm // LANES

    @pl.when(j == 0)
    def _():
        acc_ref[...] = jnp.zeros(acc_ref.shape, F32)
        for h in range(PEER_HEADS):
            for tl in range(nt):
                ls = slice(tl * LANES, (tl + 1) * LANES)
                s1 = st_ref[2 * h, tl]
                thr_ref[h, :, tl, :] = stats_ref[0, h:h + 1, ls] - s1
                c1_ref[h, :, tl, :] = jnp.exp(s1 - stats_ref[1, h:h + 1, ls]) * stats_ref[3, h:h + 1, ls]
                e2_ref[h, tl] = jnp.exp(st_ref[2 * h + 1, tl] - stats_ref[2, h:h + 1, ls])

    act = _dot(u_ref[...], a3t_ref[...])
    p_rows = []
    for al in range(n_al):
        a = j * n_al + al
        rs = slice(al * nk, (al + 1) * nk)
        tiles = []
        for tl in range(nt):
            w = jnp.zeros((nk, LANES), F32)
            for h in range(PEER_HEADS):
                thr = thr_ref[h, a, tl:tl + 1, :]
                c1 = c1_ref[h, a, tl:tl + 1, :]
                w = w + jnp.where(st_ref[2 * h + 1, tl] >= thr, e2_ref[h, tl] * c1, 0.0)
            tiles.append((_gelu(act[rs, tl * LANES:(tl + 1) * LANES]) * w).astype(BF16))
        p_rows.append(jnp.concatenate(tiles, axis=1))
    p = jnp.concatenate(p_rows, axis=0)
    acc_ref[...] += lax.dot_general(v_ref[...], p, (((0,), (0,)), ((), ())), preferred_element_type=F32)

    @pl.when(j == pl.num_programs(1) - 1)
    def _():
        o_ref[...] = _rms(h2_ref[...] + acc_ref[...].T, g_ref[...])


def _peer_dense(a3t, st, stats, u_bf, v_bf, h2, g, tm, te):
    D, T = a3t.shape
    nt = tm // LANES
    return pl.pallas_call(
        functools.partial(_peer_dense_kernel, te=te, tm=tm),
        grid=(T // tm, u_bf.shape[0] // te),
        in_specs=[pl.BlockSpec((D, tm), lambda i, j: (0, i)),
                  pl.BlockSpec((st.shape[0], nt) + st.shape[2:], lambda i, j: (0, i, 0, 0)),
                  pl.BlockSpec(stats.shape[:2] + (tm,), lambda i, j: (0, 0, i)),
                  pl.BlockSpec((te, D), lambda i, j: (j, 0)),
                  pl.BlockSpec((te, D), lambda i, j: (j, 0)),
                  pl.BlockSpec((tm, D), lambda i, j: (i, 0)), _const_spec(g.shape)],
        out_specs=pl.BlockSpec((tm, D), lambda i, j: (i, 0)),
        out_shape=jax.ShapeDtypeStruct((T, D), F32),
        scratch_shapes=[pltpu.VMEM((PEER_HEADS, PEER_KEYS, nt, LANES), F32)] * 2
        + [pltpu.VMEM((PEER_HEADS, nt, PEER_KEYS, LANES), F32), pltpu.VMEM((D, tm), F32)],
        compiler_params=_params(("parallel", "arbitrary")),
        name="peer_dense",
    )(a3t, st, stats, u_bf, v_bf, h2, g)


def _rope_tables(seq):
    pos = jnp.arange(seq, dtype=F32)

    def tab(dim):
        inv = 1.0 / (ROPE_BASE ** (jnp.arange(0, dim, 2, dtype=F32) / dim))
        ang = pos[:, None] * inv[None, :]
        return jnp.cos(ang), jnp.sin(ang)

    cm, sm = tab(MLA_ROPE)
    pad = LANES - MLA_NOPE - MLA_ROPE
    cosm = jnp.concatenate([jnp.ones((seq, MLA_NOPE), F32), cm, cm, jnp.zeros((seq, pad), F32)], axis=1)
    sinm = jnp.concatenate([jnp.zeros((seq, MLA_NOPE), F32), -sm, sm, jnp.zeros((seq, pad), F32)], axis=1)
    cr, sr = tab(RET_DK)
    cosr = jnp.concatenate([cr, cr], axis=1)
    sinr = jnp.concatenate([-sr, sr], axis=1)
    return cosm, sinm, cosr, sinr


def _retention_tables():
    C = RET_CHUNK
    log_g = jnp.log(1.0 - 2.0 ** (-5.0 - jnp.arange(RET_HEADS, dtype=F32)))
    idx = jnp.arange(C, dtype=F32)
    rel = idx[:, None] - idx[None, :]
    decay = jnp.where(rel[None] >= 0, jnp.exp(jnp.maximum(rel, 0.0)[None] * log_g[:, None, None]), 0.0)
    zeta = jnp.exp((C - 1 - idx)[None, :] * log_g[:, None])
    xi = jnp.exp((idx + 1)[None, :] * log_g[:, None])
    cd = jnp.exp(C * log_g)
    bc = lambda t: jnp.broadcast_to(t[:, :, None], (RET_HEADS, C, LANES))
    return decay, bc(zeta), bc(xi), jnp.broadcast_to(cd[:, None, None], (RET_HEADS, C, LANES))


def _layer(h, mem, g_mix, w_in, g_q_lora, w_uq, g_kv_lora, w_ukv, g_ret_gn, w_o, g_xattn, g_mem,
           w_xq, w_xkv, w_xo, g_ffn, w_pq, sub_keys, u_experts, v_experts, g_out):
    B, S, D = h.shape
    T = B * S
    row = lambda g: g.reshape(1, -1)

    o_pe = MLA_Q_RANK + MLA_KV_RANK
    pe_pad = jnp.zeros((D, LANES), F32).at[:, MLA_NOPE:MLA_NOPE + MLA_ROPE].set(w_in[:, o_pe:o_pe + MLA_ROPE])
    w_in_ext = jnp.concatenate([w_in[:, :o_pe], pe_pad, w_in[:, o_pe + MLA_ROPE:]], axis=1).astype(BF16)
    dq = MLA_NOPE + MLA_ROPE
    w_uq_pad = jnp.pad(w_uq.reshape(MLA_Q_RANK, MLA_HEADS, dq), ((0, 0), (0, 0), (0, LANES - dq)))
    w_uq_pad = w_uq_pad.reshape(MLA_Q_RANK, MLA_HEADS * LANES).astype(BF16)
    w_ukv3 = w_ukv.reshape(MLA_KV_RANK, MLA_HEADS, MLA_NOPE + MLA_V)
    w_k = jnp.pad(w_ukv3[:, :, :MLA_NOPE], ((0, 0), (0, 0), (0, LANES - MLA_NOPE)))
    w_k = w_k.reshape(MLA_KV_RANK, MLA_HEADS * LANES).astype(BF16)
    w_v = jnp.pad(w_ukv3[:, :, MLA_NOPE:], ((0, 0), (0, 0), (0, LANES - MLA_V)))
    w_v = w_v.reshape(MLA_KV_RANK, MLA_HEADS * LANES).astype(BF16)

    cosm, sinm, cosr, sinr = _rope_tables(S)
    q, k, v, rq, rk, rv, rg = _in_proj(h.reshape(T, D), row(g_mix), w_in_ext, row(g_q_lora), w_uq_pad,
                                       row(g_kv_lora), w_k, w_v, cosm, sinm, cosr, sinr, S, min(512, S))
    r3 = lambda t: t.reshape(B, S, t.shape[-1])
    o_mla = _mla_attn(r3(q), r3(k), r3(v), min(1024, S))
    o_ret = _retention(r3(rq), r3(rk), r3(rv), r3(rg), row(g_ret_gn), *_retention_tables(), min(1024, S))

    kx, vx = _mem_kv(mem, row(g_mem), w_xkv.astype(BF16))
    n_mla = MLA_HEADS * MLA_V
    w_o_bf = w_o.astype(BF16)
    h2, a3t, st = _post_mix(h.reshape(T, D), o_mla.reshape(T, -1), o_ret.reshape(T, -1), w_o_bf[:n_mla],
                            w_o_bf[n_mla:], row(g_xattn), w_xq.astype(BF16), kx, vx, w_xo.astype(BF16),
                            row(g_ffn), w_pq.astype(BF16), sub_keys.astype(BF16), S, min(512, S))
    stats = _peer_route(st)
    out = _peer_dense(a3t, st, stats, u_experts.astype(BF16), v_experts.astype(BF16), h2, row(g_out),
                      min(512, T), 2048)
    return out.reshape(B, S, D)


def kernel(x, mem, g_mix, w_in, g_q_lora, w_uq, g_kv_lora, w_ukv, g_ret_gn, w_o, g_xattn, g_mem, w_xq, w_xkv,
           w_xo, g_ffn, w_pq, sub_keys, u_experts, v_experts, g_final):
    depth = g_mix.shape[0]
    assert depth == 1, "the final norm is fused into the single layer's last kernel"
    l = 0
    return _layer(x, mem, g_mix[l], w_in[l], g_q_lora[l], w_uq[l], g_kv_lora[l], w_ukv[l], g_ret_gn[l], w_o[l],
                  g_xattn[l], g_mem[l], w_xq[l], w_xkv[l], w_xo[l], g_ffn[l], w_pq[l], sub_keys[l],
                  u_experts[l], v_experts[l], g_final)
```

```python
import functools
import math

import jax
import jax.numpy as jnp
from jax import lax
from jax.experimental import pallas as pl
from jax.experimental.pallas import tpu as pltpu

F32 = jnp.float32
BF16 = jnp.bfloat16

LANES = 128
EPS = 1e-6
NEG = -1e30
ROPE_BASE = 10000.0
VMEM_LIMIT = 56 * 1024 * 1024

MLA_HEADS, MLA_Q_RANK, MLA_KV_RANK = 8, 384, 256
MLA_NOPE, MLA_ROPE, MLA_V = 64, 32, 64
RET_HEADS, RET_DK, RET_DV, RET_CHUNK = 4, 128, 128, 128
X_HEADS = 4
PEER_KEYS, PEER_HEADS, PEER_TOPK, PEER_DKEY = 128, 8, 16, 256


def _params(sem, vmem=VMEM_LIMIT):
    return pltpu.CompilerParams(dimension_semantics=sem, vmem_limit_bytes=vmem)


def _rms(x, g):
    return x * lax.rsqrt(jnp.mean(x * x, axis=-1, keepdims=True) + EPS) * g


def _dot(a, b):
    return jnp.dot(a, b, preferred_element_type=F32)


def _dot_nt(a, b):
    return lax.dot_general(a, b, (((1,), (1,)), ((), ())), preferred_element_type=F32)


def _gelu_x2(x):
    return x * (1.0 + lax.erf(x * (2.0 ** -0.5)))


def _const_spec(shape):
    nd = len(shape)
    return pl.BlockSpec(shape, lambda *_: (0,) * nd)


def _in_proj_kernel(x_ref, gmix_ref, win_ref, gq_ref, wuq_ref, gkv_ref, wk_ref, wv_ref,
                    cosm_ref, sinm_ref, cosr_ref, sinr_ref,
                    q_ref, k_ref, v_ref, rq_ref, rk_ref, rv_ref, rg_ref):
    x = x_ref[...]
    a = _rms(x, gmix_ref[...]).astype(BF16)
    proj = _dot(a, win_ref[...])
    o_cq, o_ckv, o_pe, o_rq = 0, MLA_Q_RANK, MLA_Q_RANK + MLA_KV_RANK, MLA_Q_RANK + MLA_KV_RANK + LANES
    rw = RET_HEADS * RET_DK
    cq = proj[:, o_cq:o_ckv]
    ckv = proj[:, o_ckv:o_pe]
    kpe = proj[:, o_pe:o_rq]
    cqn = _rms(cq, gq_ref[...]).astype(BF16)
    ckvn = _rms(ckv, gkv_ref[...]).astype(BF16)
    q = _dot(cqn, wuq_ref[...])
    kn = _dot(ckvn, wk_ref[...])
    vlane = lax.broadcasted_iota(jnp.int32, (x.shape[0], MLA_HEADS * LANES), 1) % LANES
    v_ref[...] = jnp.where(vlane == MLA_V, 1.0, _dot(ckvn, wv_ref[...])).astype(BF16)

    cosm, sinm = cosm_ref[...], sinm_ref[...]
    lane = lax.broadcasted_iota(jnp.int32, (x.shape[0], LANES), 1)
    first_half = lane < MLA_NOPE + MLA_ROPE // 2

    def rope_m(c):
        rot = jnp.where(first_half, pltpu.roll(c, LANES - MLA_ROPE // 2, axis=1),
                        pltpu.roll(c, MLA_ROPE // 2, axis=1))
        return c * cosm + rot * sinm

    qscale = (MLA_NOPE + MLA_ROPE) ** -0.5 * math.log2(math.e)
    kpe_r = rope_m(kpe)
    for h in range(MLA_HEADS):
        sl = slice(h * LANES, (h + 1) * LANES)
        q_ref[:, sl] = (rope_m(q[:, sl]) * qscale).astype(BF16)
        k_ref[:, sl] = (kn[:, sl] + kpe_r).astype(BF16)

    cosr, sinr = cosr_ref[...], sinr_ref[...]
    kscale = RET_DK ** -0.5
    for h in range(RET_HEADS):
        sl = slice(h * LANES, (h + 1) * LANES)
        c = proj[:, o_rq + h * LANES:o_rq + (h + 1) * LANES]
        rq_ref[:, sl] = (c * cosr + pltpu.roll(c, RET_DK // 2, axis=1) * sinr).astype(BF16)
        c = proj[:, o_rq + rw + h * LANES:o_rq + rw + (h + 1) * LANES]
        rk_ref[:, sl] = ((c * cosr + pltpu.roll(c, RET_DK // 2, axis=1) * sinr) * kscale).astype(BF16)
    rv_ref[...] = proj[:, o_rq + 2 * rw:o_rq + 3 * rw].astype(BF16)
    rg_ref[...] = proj[:, o_rq + 3 * rw:o_rq + 4 * rw]


def _in_proj(x2, g_mix, w_in_ext, g_q, w_uq_pad, g_kv, w_k, w_v, cosm, sinm, cosr, sinr, seq, tm):
    T, D = x2.shape
    nS = seq // tm
    tok = lambda w: pl.BlockSpec((tm, w), lambda i: (i, 0))
    pos = lambda w: pl.BlockSpec((tm, w), lambda i: (i % nS, 0))
    outs = [(T, MLA_HEADS * LANES, BF16), (T, MLA_HEADS * LANES, BF16), (T, MLA_HEADS * LANES, BF16),
            (T, RET_HEADS * RET_DK, BF16), (T, RET_HEADS * RET_DK, BF16), (T, RET_HEADS * RET_DV, BF16),
            (T, RET_HEADS * RET_DV, F32)]
    return pl.pallas_call(
        _in_proj_kernel,
        grid=(T // tm,),
        in_specs=[tok(D), _const_spec(g_mix.shape), _const_spec(w_in_ext.shape), _const_spec(g_q.shape),
                  _const_spec(w_uq_pad.shape), _const_spec(g_kv.shape), _const_spec(w_k.shape),
                  _const_spec(w_v.shape), pos(LANES), pos(LANES), pos(LANES), pos(LANES)],
        out_specs=[tok(w) for (_, w, _) in outs],
        out_shape=[jax.ShapeDtypeStruct((t, w), d) for (t, w, d) in outs],
        compiler_params=_params(("parallel",)),
        name="in_proj",
    )(x2, g_mix, w_in_ext, g_q, w_uq_pad, g_kv, w_k, w_v, cosm, sinm, cosr, sinr)


ATTN_HEADS = 4


def _mla_attn_kernel(q_ref, k_ref, v_ref, o_ref, *, t):
    i = pl.program_id(2)
    causal = (lax.broadcasted_iota(jnp.int32, (t, t), 1) <= lax.broadcasted_iota(jnp.int32, (t, t), 0))

    def step(j, carry, masked):
        k0 = pl.multiple_of(j * t, t)
        new = []
        for hh in range(ATTN_HEADS):
            m, acc = carry[hh]
            hs = slice(hh * LANES, (hh + 1) * LANES)
            s = _dot_nt(q_ref[0, :, hs], k_ref[0, pl.ds(k0, t), hs])
            if masked:
                s = jnp.where(causal, s, NEG)
            m_new = jnp.maximum(m, jnp.max(s, axis=-1, keepdims=True))
            p = jnp.exp2(s - m_new).astype(BF16)
            acc = jnp.exp2(m - m_new) * acc + _dot(p, v_ref[0, pl.ds(k0, t), hs])
            new.append((m_new, acc))
        return tuple(new)

    init = tuple((jnp.full((t, 1), NEG, F32), jnp.zeros((t, LANES), F32)) for _ in range(ATTN_HEADS))
    carry = lax.fori_loop(0, i, functools.partial(step, masked=False), init)
    carry = step(i, carry, True)
    outs = [acc[:, :MLA_V] / acc[:, MLA_V:MLA_V + 1] for (_, acc) in carry]
    o_ref[0] = jnp.concatenate(outs, axis=-1).astype(o_ref.dtype)


def _mla_attn(q, k, v, t):
    B, S, _ = q.shape
    return pl.pallas_call(
        functools.partial(_mla_attn_kernel, t=t),
        grid=(B, MLA_HEADS // ATTN_HEADS, S // t),
        in_specs=[pl.BlockSpec((1, t, ATTN_HEADS * LANES), lambda b, h, i: (b, i, h)),
                  pl.BlockSpec((1, S, ATTN_HEADS * LANES), lambda b, h, i: (b, 0, h)),
                  pl.BlockSpec((1, S, ATTN_HEADS * LANES), lambda b, h, i: (b, 0, h))],
        out_specs=pl.BlockSpec((1, t, ATTN_HEADS * MLA_V), lambda b, h, i: (b, i, h)),
        out_shape=jax.ShapeDtypeStruct((B, S, MLA_HEADS * MLA_V), BF16),
        compiler_params=_params(("parallel", "parallel", "arbitrary")),
        name="mla_attn",
    )(q, k, v)


def _retention_kernel(rq_ref, rk_ref, rv_ref, rg_ref, gn_ref, decay_ref, zeta_ref, xi_ref, cd_ref,
                      o_ref, r_ref, *, n_chunks):
    C = RET_CHUNK

    @pl.when(pl.program_id(1) == 0)
    def _():
        r_ref[...] = jnp.zeros(r_ref.shape, F32)

    def chunk(n, carry):
        r0 = pl.multiple_of(n * C, C)
        for h in range(RET_HEADS):
            hs = slice(h * LANES, (h + 1) * LANES)
            qc = rq_ref[0, pl.ds(r0, C), hs]
            kc = rk_ref[0, pl.ds(r0, C), hs]
            vc = rv_ref[0, pl.ds(r0, C), hs]
            state = r_ref[h]
            sc = _dot_nt(qc, kc) * decay_ref[h]
            inner = _dot(sc.astype(BF16), vc)
            cross = _dot(qc, state.astype(BF16)) * xi_ref[h]
            o = inner + cross
            kz = (kc.astype(F32) * zeta_ref[h]).T.astype(BF16)
            r_ref[h] = state * cd_ref[h] + _dot(kz, vc)
            mu = jnp.mean(o, axis=-1, keepdims=True)
            d = o - mu
            var = jnp.mean(d * d, axis=-1, keepdims=True)
            on = d * lax.rsqrt(var + EPS) * gn_ref[:, hs]
            g = rg_ref[0, pl.ds(r0, C), hs]
            o_ref[0, pl.ds(r0, C), hs] = (g * jax.nn.sigmoid(g) * on).astype(o_ref.dtype)
        return carry

    lax.fori_loop(0, n_chunks, chunk, 0)


def _retention(rq, rk, rv, rg, g_gn, decay, zeta, xi, cd, ts):
    B, S, W = rq.shape
    tok = pl.BlockSpec((1, ts, W), lambda b, i: (b, i, 0))
    return pl.pallas_call(
        functools.partial(_retention_kernel, n_chunks=ts // RET_CHUNK),
        grid=(B, S // ts),
        in_specs=[tok, tok, tok, tok, _const_spec(g_gn.shape), _const_spec(decay.shape),
                  _const_spec(zeta.shape), _const_spec(xi.shape), _const_spec(cd.shape)],
        out_specs=tok,
        out_shape=jax.ShapeDtypeStruct((B, S, W), BF16),
        scratch_shapes=[pltpu.VMEM((RET_HEADS, RET_DK, RET_DV), F32)],
        compiler_params=_params(("parallel", "arbitrary")),
        name="retention",
    )(rq, rk, rv, rg, g_gn, decay, zeta, xi, cd)


def _mem_kv_kernel(mem_ref, g_ref, w_ref, k_ref, v_ref):
    D = mem_ref.shape[-1]
    mn = _rms(mem_ref[0], g_ref[...]).astype(BF16)
    kv = _dot(mn, w_ref[...])
    k_ref[0] = kv[:, :D].astype(BF16)
    v_ref[0] = kv[:, D:].astype(BF16)


def _mem_kv(mem, g_mem, w_xkv):
    B, M, D = mem.shape
    blk = pl.BlockSpec((1, M, D), lambda b: (b, 0, 0))
    return pl.pallas_call(
        _mem_kv_kernel,
        grid=(B,),
        in_specs=[blk, _const_spec(g_mem.shape), _const_spec(w_xkv.shape)],
        out_specs=[blk, blk],
        out_shape=[jax.ShapeDtypeStruct((B, M, D), BF16)] * 2,
        compiler_params=_params(("parallel",)),
        name="mem_kv",
    )(mem, g_mem, w_xkv)


def _post_mix_kernel(x_ref, omla_ref, oret_ref, woa_ref, wob_ref, gx_ref, wxq_ref, kx_ref, vx_ref, wxo_ref,
                     gf_ref, wpq_ref, keys_ref, h2_ref, a3t_ref, st_ref, ox_ref):
    D = x_ref.shape[-1]
    h1 = x_ref[...] + _dot(omla_ref[...], woa_ref[...]) + _dot(oret_ref[...], wob_ref[...])
    a2 = _rms(h1, gx_ref[...]).astype(BF16)
    qx = _dot(a2, wxq_ref[...])
    hd = D // X_HEADS
    for h in range(X_HEADS):
        sl = slice(h * hd, (h + 1) * hd)
        s = _dot_nt(qx[:, sl].astype(BF16), kx_ref[0, :, sl]) * (hd ** -0.5)
        m = jnp.max(s, axis=-1, keepdims=True)
        p = jnp.exp(s - m)
        p = p / jnp.sum(p, axis=-1, keepdims=True)
        ox_ref[:, sl] = _dot(p.astype(BF16), vx_ref[0, :, sl]).astype(BF16)
    h2 = h1 + _dot(ox_ref[...], wxo_ref[...])
    h2_ref[...] = h2
    a3 = _rms(h2, gf_ref[...])
    a3t_ref[...] = a3.T.astype(BF16)
    qp = _dot(a3.astype(BF16), wpq_ref[...]).astype(BF16)
    half = PEER_DKEY // 2
    for hp in range(2 * PEER_HEADS):
        sc = _dot_nt(keys_ref[hp % 2], qp[:, hp * half:(hp + 1) * half])
        for tl in range(sc.shape[1] // LANES):
            st_ref[hp, tl] = sc[:, tl * LANES:(tl + 1) * LANES]


def _post_mix(x2, o_mla, o_ret, w_oa, w_ob, g_x, w_xq, kx, vx, w_xo, g_f, w_pq, keys, seq, tm):
    T, D = x2.shape
    nS = seq // tm
    tok = lambda w: pl.BlockSpec((tm, w), lambda i: (i, 0))
    memb = pl.BlockSpec((1,) + kx.shape[1:], lambda i: (i // nS, 0, 0))
    return pl.pallas_call(
        _post_mix_kernel,
        grid=(T // tm,),
        in_specs=[tok(D), tok(o_mla.shape[1]), tok(o_ret.shape[1]), _const_spec(w_oa.shape),
                  _const_spec(w_ob.shape), _const_spec(g_x.shape), _const_spec(w_xq.shape), memb, memb,
                  _const_spec(w_xo.shape), _const_spec(g_f.shape), _const_spec(w_pq.shape),
                  _const_spec(keys.shape)],
        out_specs=[tok(D), pl.BlockSpec((D, tm), lambda i: (0, i)),
                   pl.BlockSpec((2 * PEER_HEADS, tm // LANES, PEER_KEYS, LANES), lambda i: (0, i, 0, 0))],
        out_shape=[jax.ShapeDtypeStruct((T, D), F32), jax.ShapeDtypeStruct((D, T), BF16),
                   jax.ShapeDtypeStruct((2 * PEER_HEADS, T // LANES, PEER_KEYS, LANES), F32)],
        scratch_shapes=[pltpu.VMEM((tm, D), BF16)],
        compiler_params=_params(("parallel",)),
        name="post_mix",
    )(x2, o_mla, o_ret, w_oa, w_ob, g_x, w_xq, kx, vx, w_xo, g_f, w_pq, keys)


N_TOP = PEER_TOPK + 1
TOP_ROWS = 24
SUBLANES = 8


def _sort_network(n):
    def merge(lo, hi, r):
        step = r * 2
        if step < hi - lo:
            yield from merge(lo, hi, step)
            yield from merge(lo + r, hi, step)
            yield from [(i, i + r) for i in range(lo + r, hi - r, step)]
        else:
            yield (lo, lo + r)

    def sort(lo, hi):
        if hi - lo >= 1:
            mid = lo + (hi - lo) // 2
            yield from sort(lo, mid)
            yield from sort(mid + 1, hi)
            yield from merge(lo, hi, 1)

    return list(sort(0, n - 1))


def _top17(x):
    k = PEER_TOPK
    v = [x[SUBLANES * i:SUBLANES * (i + 1), :] for i in range(x.shape[0] // SUBLANES)]
    assert len(v) == k
    for i, j in _sort_network(k):
        v[i], v[j] = jnp.maximum(v[i], v[j]), jnp.minimum(v[i], v[j])
    dropped = jnp.full(v[0].shape, NEG, F32)
    for shift in (SUBLANES // 2, SUBLANES // 4, SUBLANES // 8):
        other = [pltpu.roll(a, SUBLANES - shift, axis=0) for a in v]
        lose = [jnp.minimum(v[i], other[k - 1 - i]) for i in range(k)]
        v = [jnp.maximum(v[i], other[k - 1 - i]) for i in range(k)]
        for a in lose + [pltpu.roll(dropped, SUBLANES - shift, axis=0)]:
            dropped = jnp.maximum(dropped, a)
        d = k // 2
        while d >= 1:
            for i in range(k):
                if i & d == 0:
                    v[i], v[i + d] = jnp.maximum(v[i], v[i + d]), jnp.minimum(v[i], v[i + d])
            d //= 2
    return [a[0:1, :] for a in v], dropped[0:1, :]


def _peer_route_kernel(st_ref, stats_ref, vs_ref):
    tl = st_ref.shape[-1]
    vs_ref[...] = jnp.full(vs_ref.shape, NEG, F32)
    for h in range(PEER_HEADS):
        for p in range(2):
            best, nxt = _top17(st_ref[2 * h + p, 0])
            for it, row in enumerate(best + [nxt]):
                vs_ref[p, it:it + 1, :] = row
        v1 = lambda i: vs_ref[0, i:i + 1, :]
        v2 = lambda i: vs_ref[1, i:i + 1, :]
        cands = [v1(0) + vs_ref[1, 8 * r:8 * r + 8, :] for r in range(TOP_ROWS // 8)]
        cands += [v1(i) + vs_ref[1, 0:8, :] for i in range(1, 8)]
        cands += [vs_ref[0, 8 * r:8 * r + 8, :] + v2(0) for r in range(1, TOP_ROWS // 8)]
        cands += [jnp.full((SUBLANES, tl), NEG, F32)] * (PEER_KEYS // SUBLANES - len(cands))
        best, c17 = _top17(jnp.concatenate(cands, axis=0))
        top = best[0]
        z = jnp.zeros((1, tl), F32)
        for m in best:
            z = z + jnp.exp(m - top)
        stats_ref[0, h:h + 1, :] = 0.5 * (best[-1] + c17)
        stats_ref[1, h:h + 1, :] = v1(0)
        stats_ref[2, h:h + 1, :] = v2(0)
        stats_ref[3, h:h + 1, :] = 1.0 / z


def _peer_route(st):
    HP, nb, K, tl = st.shape
    T = nb * tl
    return pl.pallas_call(
        _peer_route_kernel,
        grid=(nb,),
        in_specs=[pl.BlockSpec((HP, 1, K, tl), lambda i: (0, i, 0, 0))],
        out_specs=pl.BlockSpec((4, PEER_HEADS, tl), lambda i: (0, 0, i)),
        out_shape=jax.ShapeDtypeStruct((4, PEER_HEADS, T), F32),
        scratch_shapes=[pltpu.VMEM((2, TOP_ROWS, tl), F32)],
        compiler_params=_params(("parallel",)),
        name="peer_route",
    )(st)


def _peer_dense_kernel(a3t_ref, st_ref, stats_ref, u_ref, v_ref, h2_ref, g_ref, o_ref,
                       thr_ref, c1_ref, e2_ref, acc_ref, *, te, tm):
    j = pl.program_id(1)
    nk = PEER_KEYS
    n_al = te // nk
    nt = tm // LANES

    @pl.when(j == 0)
    def _():
        acc_ref[...] = jnp.zeros(acc_ref.shape, F32)
        for h in range(PEER_HEADS):
            for tl in range(nt):
                ls = slice(tl * LANES, (tl + 1) * LANES)
                s1 = st_ref[2 * h, tl]
                thr_ref[h, :, tl, :] = stats_ref[0, h:h + 1, ls] - s1
                c1_ref[h, :, tl, :] = jnp.exp(s1 - stats_ref[1, h:h + 1, ls]) * (0.5 * stats_ref[3, h:h + 1, ls])
                e2_ref[h, tl] = jnp.exp(st_ref[2 * h + 1, tl] - stats_ref[2, h:h + 1, ls])

    act = _dot(u_ref[...], a3t_ref[...])
    p_rows = []
    for al in range(n_al):
        a = j * n_al + al
        rs = slice(al * nk, (al + 1) * nk)
        tiles = []
        for tl in range(nt):
            w = None
            for h in range(PEER_HEADS):
                thr = thr_ref[h, a, tl:tl + 1, :]
                c1 = c1_ref[h, a, tl:tl + 1, :]
                term = jnp.where(st_ref[2 * h + 1, tl] >= thr, e2_ref[h, tl] * c1, 0.0)
                w = term if w is None else w + term
            tiles.append((_gelu_x2(act[rs, tl * LANES:(tl + 1) * LANES]) * w).astype(BF16))
        p_rows.append(jnp.concatenate(tiles, axis=1))
    p = jnp.concatenate(p_rows, axis=0)
    acc_ref[...] += lax.dot_general(v_ref[...], p, (((0,), (0,)), ((), ())), preferred_element_type=F32)

    @pl.when(j == pl.num_programs(1) - 1)
    def _():
        o_ref[...] = _rms(h2_ref[...] + acc_ref[...].T, g_ref[...])


def _peer_dense(a3t, st, stats, u_bf, v_bf, h2, g, tm, te):
    D, T = a3t.shape
    nt = tm // LANES
    return pl.pallas_call(
        functools.partial(_peer_dense_kernel, te=te, tm=tm),
        grid=(T // tm, u_bf.shape[0] // te),
        in_specs=[pl.BlockSpec((D, tm), lambda i, j: (0, i)),
                  pl.BlockSpec((st.shape[0], nt) + st.shape[2:], lambda i, j: (0, i, 0, 0)),
                  pl.BlockSpec(stats.shape[:2] + (tm,), lambda i, j: (0, 0, i)),
                  pl.BlockSpec((te, D), lambda i, j: (j, 0)),
                  pl.BlockSpec((te, D), lambda i, j: (j, 0)),
                  pl.BlockSpec((tm, D), lambda i, j: (i, 0)), _const_spec(g.shape)],
        out_specs=pl.BlockSpec((tm, D), lambda i, j: (i, 0)),
        out_shape=jax.ShapeDtypeStruct((T, D), F32),
        scratch_shapes=[pltpu.VMEM((PEER_HEADS, PEER_KEYS, nt, LANES), F32)] * 2
        + [pltpu.VMEM((PEER_HEADS, nt, PEER_KEYS, LANES), F32), pltpu.VMEM((D, tm), F32)],
        compiler_params=_params(("parallel", "arbitrary")),
        name="peer_dense",
    )(a3t, st, stats, u_bf, v_bf, h2, g)


def _rope_tables(seq):
    pos = jnp.arange(seq, dtype=F32)

    def tab(dim):
        inv = 1.0 / (ROPE_BASE ** (jnp.arange(0, dim, 2, dtype=F32) / dim))
        ang = pos[:, None] * inv[None, :]
        return jnp.cos(ang), jnp.sin(ang)

    cm, sm = tab(MLA_ROPE)
    pad = LANES - MLA_NOPE - MLA_ROPE
    cosm = jnp.concatenate([jnp.ones((seq, MLA_NOPE), F32), cm, cm, jnp.zeros((seq, pad), F32)], axis=1)
    sinm = jnp.concatenate([jnp.zeros((seq, MLA_NOPE), F32), -sm, sm, jnp.zeros((seq, pad), F32)], axis=1)
    cr, sr = tab(RET_DK)
    cosr = jnp.concatenate([cr, cr], axis=1)
    sinr = jnp.concatenate([-sr, sr], axis=1)
    return cosm, sinm, cosr, sinr


def _retention_tables():
    C = RET_CHUNK
    log_g = jnp.log(1.0 - 2.0 ** (-5.0 - jnp.arange(RET_HEADS, dtype=F32)))
    idx = jnp.arange(C, dtype=F32)
    rel = idx[:, None] - idx[None, :]
    decay = jnp.where(rel[None] >= 0, jnp.exp(jnp.maximum(rel, 0.0)[None] * log_g[:, None, None]), 0.0)
    zeta = jnp.exp((C - 1 - idx)[None, :] * log_g[:, None])
    xi = jnp.exp((idx + 1)[None, :] * log_g[:, None])
    cd = jnp.exp(C * log_g)
    bc = lambda t: jnp.broadcast_to(t[:, :, None], (RET_HEADS, C, LANES))
    return decay, bc(zeta), bc(xi), jnp.broadcast_to(cd[:, None, None], (RET_HEADS, C, LANES))


def _layer(h, mem, g_mix, w_in, g_q_lora, w_uq, g_kv_lora, w_ukv, g_ret_gn, w_o, g_xattn, g_mem,
           w_xq, w_xkv, w_xo, g_ffn, w_pq, sub_keys, u_experts, v_experts, g_out):
    B, S, D = h.shape
    T = B * S
    row = lambda g: g.reshape(1, -1)

    o_pe = MLA_Q_RANK + MLA_KV_RANK
    pe_pad = jnp.zeros((D, LANES), F32).at[:, MLA_NOPE:MLA_NOPE + MLA_ROPE].set(w_in[:, o_pe:o_pe + MLA_ROPE])
    w_in_ext = jnp.concatenate([w_in[:, :o_pe], pe_pad, w_in[:, o_pe + MLA_ROPE:]], axis=1).astype(BF16)
    dq = MLA_NOPE + MLA_ROPE
    w_uq_pad = jnp.pad(w_uq.reshape(MLA_Q_RANK, MLA_HEADS, dq), ((0, 0), (0, 0), (0, LANES - dq)))
    w_uq_pad = w_uq_pad.reshape(MLA_Q_RANK, MLA_HEADS * LANES).astype(BF16)
    w_ukv3 = w_ukv.reshape(MLA_KV_RANK, MLA_HEADS, MLA_NOPE + MLA_V)
    w_k = jnp.pad(w_ukv3[:, :, :MLA_NOPE], ((0, 0), (0, 0), (0, LANES - MLA_NOPE)))
    w_k = w_k.reshape(MLA_KV_RANK, MLA_HEADS * LANES).astype(BF16)
    w_v = jnp.pad(w_ukv3[:, :, MLA_NOPE:], ((0, 0), (0, 0), (0, LANES - MLA_V)))
    w_v = w_v.reshape(MLA_KV_RANK, MLA_HEADS * LANES).astype(BF16)

    cosm, sinm, cosr, sinr = _rope_tables(S)
    q, k, v, rq, rk, rv, rg = _in_proj(h.reshape(T, D), row(g_mix), w_in_ext, row(g_q_lora), w_uq_pad,
                                       row(g_kv_lora), w_k, w_v, cosm, sinm, cosr, sinr, S, min(512, S))
    r3 = lambda t: t.reshape(B, S, t.shape[-1])
    o_mla = _mla_attn(r3(q), r3(k), r3(v), min(1024, S))
    o_ret = _retention(r3(rq), r3(rk), r3(rv), r3(rg), row(g_ret_gn), *_retention_tables(), min(1024, S))

    kx, vx = _mem_kv(mem, row(g_mem), w_xkv.astype(BF16))
    n_mla = MLA_HEADS * MLA_V
    w_o_bf = w_o.astype(BF16)
    h2, a3t, st = _post_mix(h.reshape(T, D), o_mla.reshape(T, -1), o_ret.reshape(T, -1), w_o_bf[:n_mla],
                            w_o_bf[n_mla:], row(g_xattn), w_xq.astype(BF16), kx, vx, w_xo.astype(BF16),
                            row(g_ffn), w_pq.astype(BF16), sub_keys.astype(BF16), S, min(512, S))
    stats = _peer_route(st)
    out = _peer_dense(a3t, st, stats, u_experts.astype(BF16), v_experts.astype(BF16), h2, row(g_out),
                      min(512, T), 2048)
    return out.reshape(B, S, D)


def kernel(x, mem, g_mix, w_in, g_q_lora, w_uq, g_kv_lora, w_ukv, g_ret_gn, w_o, g_xattn, g_mem, w_xq, w_xkv,
           w_xo, g_ffn, w_pq, sub_keys, u_experts, v_experts, g_final):
    depth = g_mix.shape[0]
    assert depth == 1, "the final norm is fused into the single layer's last kernel"
    l = 0
    return _layer(x, mem, g_mix[l], w_in[l], g_q_lora[l], w_uq[l], g_kv_lora[l], w_ukv[l], g_ret_gn[l], w_o[l],
                  g_xattn[l], g_mem[l], w_xq[l], w_xkv[l], w_xo[l], g_ffn[l], w_pq[l], sub_keys[l],
                  u_experts[l], v_experts[l], g_final)
```

```python
import functools
import math

import jax
import jax.numpy as jnp
from jax import lax
from jax.experimental import pallas as pl
from jax.experimental.pallas import tpu as pltpu

F32 = jnp.float32
BF16 = jnp.bfloat16

LANES = 128
EPS = 1e-6
NEG = -1e30
ROPE_BASE = 10000.0
VMEM_LIMIT = 56 * 1024 * 1024

MLA_HEADS, MLA_Q_RANK, MLA_KV_RANK = 8, 384, 256
MLA_NOPE, MLA_ROPE, MLA_V = 64, 32, 64
RET_HEADS, RET_DK, RET_DV, RET_CHUNK = 4, 128, 128, 128
X_HEADS = 4
PEER_KEYS, PEER_HEADS, PEER_TOPK, PEER_DKEY = 128, 8, 16, 256


def _params(sem, vmem=VMEM_LIMIT):
    return pltpu.CompilerParams(dimension_semantics=sem, vmem_limit_bytes=vmem)


def _rms(x, g):
    return x * lax.rsqrt(jnp.mean(x * x, axis=-1, keepdims=True) + EPS) * g


def _dot(a, b):
    return jnp.dot(a, b, preferred_element_type=F32)


def _dot_nt(a, b):
    return lax.dot_general(a, b, (((1,), (1,)), ((), ())), preferred_element_type=F32)


def _gelu_x2(x):
    return x * (1.0 + lax.erf(x * (2.0 ** -0.5)))


def _const_spec(shape):
    nd = len(shape)
    return pl.BlockSpec(shape, lambda *_: (0,) * nd)


def _in_proj_kernel(x_ref, gmix_ref, win_ref, gq_ref, wuq_ref, gkv_ref, wk_ref, wv_ref,
                    cosm_ref, sinm_ref, cosr_ref, sinr_ref,
                    q_ref, k_ref, v_ref, rq_ref, rk_ref, rv_ref, rg_ref):
    x = x_ref[...]
    a = _rms(x, gmix_ref[...]).astype(BF16)
    proj = _dot(a, win_ref[...])
    o_cq, o_ckv, o_pe, o_rq = 0, MLA_Q_RANK, MLA_Q_RANK + MLA_KV_RANK, MLA_Q_RANK + MLA_KV_RANK + LANES
    rw = RET_HEADS * RET_DK
    cq = proj[:, o_cq:o_ckv]
    ckv = proj[:, o_ckv:o_pe]
    kpe = proj[:, o_pe:o_rq]
    cqn = _rms(cq, gq_ref[...]).astype(BF16)
    ckvn = _rms(ckv, gkv_ref[...]).astype(BF16)
    q = _dot(cqn, wuq_ref[...])
    kn = _dot(ckvn, wk_ref[...])
    vlane = lax.broadcasted_iota(jnp.int32, (x.shape[0], MLA_HEADS * LANES), 1) % LANES
    v_ref[...] = jnp.where(vlane == MLA_V, 1.0, _dot(ckvn, wv_ref[...])).astype(BF16)

    cosm, sinm = cosm_ref[...], sinm_ref[...]
    lane = lax.broadcasted_iota(jnp.int32, (x.shape[0], LANES), 1)
    first_half = lane < MLA_NOPE + MLA_ROPE // 2

    def rope_m(c):
        rot = jnp.where(first_half, pltpu.roll(c, LANES - MLA_ROPE // 2, axis=1),
                        pltpu.roll(c, MLA_ROPE // 2, axis=1))
        return c * cosm + rot * sinm

    qscale = (MLA_NOPE + MLA_ROPE) ** -0.5 * math.log2(math.e)
    kpe_r = rope_m(kpe)
    for h in range(MLA_HEADS):
        sl = slice(h * LANES, (h + 1) * LANES)
        q_ref[:, sl] = (rope_m(q[:, sl]) * qscale).astype(BF16)
        k_ref[:, sl] = (kn[:, sl] + kpe_r).astype(BF16)

    cosr, sinr = cosr_ref[...], sinr_ref[...]
    kscale = RET_DK ** -0.5
    for h in range(RET_HEADS):
        sl = slice(h * LANES, (h + 1) * LANES)
        c = proj[:, o_rq + h * LANES:o_rq + (h + 1) * LANES]
        rq_ref[:, sl] = (c * cosr + pltpu.roll(c, RET_DK // 2, axis=1) * sinr).astype(BF16)
        c = proj[:, o_rq + rw + h * LANES:o_rq + rw + (h + 1) * LANES]
        rk_ref[:, sl] = ((c * cosr + pltpu.roll(c, RET_DK // 2, axis=1) * sinr) * kscale).astype(BF16)
    rv_ref[...] = proj[:, o_rq + 2 * rw:o_rq + 3 * rw].astype(BF16)
    rg_ref[...] = proj[:, o_rq + 3 * rw:o_rq + 4 * rw]


def _in_proj(x2, g_mix, w_in_ext, g_q, w_uq_pad, g_kv, w_k, w_v, cosm, sinm, cosr, sinr, seq, tm):
    T, D = x2.shape
    nS = seq // tm
    tok = lambda w: pl.BlockSpec((tm, w), lambda i: (i, 0))
    pos = lambda w: pl.BlockSpec((tm, w), lambda i: (i % nS, 0))
    outs = [(T, MLA_HEADS * LANES, BF16), (T, MLA_HEADS * LANES, BF16), (T, MLA_HEADS * LANES, BF16),
            (T, RET_HEADS * RET_DK, BF16), (T, RET_HEADS * RET_DK, BF16), (T, RET_HEADS * RET_DV, BF16),
            (T, RET_HEADS * RET_DV, F32)]
    return pl.pallas_call(
        _in_proj_kernel,
        grid=(T // tm,),
        in_specs=[tok(D), _const_spec(g_mix.shape), _const_spec(w_in_ext.shape), _const_spec(g_q.shape),
                  _const_spec(w_uq_pad.shape), _const_spec(g_kv.shape), _const_spec(w_k.shape),
                  _const_spec(w_v.shape), pos(LANES), pos(LANES), pos(LANES), pos(LANES)],
        out_specs=[tok(w) for (_, w, _) in outs],
        out_shape=[jax.ShapeDtypeStruct((t, w), d) for (t, w, d) in outs],
        compiler_params=_params(("parallel",)),
        name="in_proj",
    )(x2, g_mix, w_in_ext, g_q, w_uq_pad, g_kv, w_k, w_v, cosm, sinm, cosr, sinr)


ATTN_HEADS = 4


def _mla_attn_kernel(q_ref, k_ref, v_ref, o_ref, *, t):
    i = pl.program_id(2)
    causal = (lax.broadcasted_iota(jnp.int32, (t, t), 1) <= lax.broadcasted_iota(jnp.int32, (t, t), 0))

    def step(j, carry, masked):
        k0 = pl.multiple_of(j * t, t)
        new = []
        for hh in range(ATTN_HEADS):
            m, acc = carry[hh]
            hs = slice(hh * LANES, (hh + 1) * LANES)
            s = _dot_nt(q_ref[0, :, hs], k_ref[0, pl.ds(k0, t), hs])
            if masked:
                s = jnp.where(causal, s, NEG)
            m_new = jnp.maximum(m, jnp.max(s, axis=-1, keepdims=True))
            p = jnp.exp2(s - m_new).astype(BF16)
            acc = jnp.exp2(m - m_new) * acc + _dot(p, v_ref[0, pl.ds(k0, t), hs])
            new.append((m_new, acc))
        return tuple(new)

    init = tuple((jnp.full((t, 1), NEG, F32), jnp.zeros((t, LANES), F32)) for _ in range(ATTN_HEADS))
    carry = lax.fori_loop(0, i, functools.partial(step, masked=False), init)
    carry = step(i, carry, True)
    outs = [acc[:, :MLA_V] / acc[:, MLA_V:MLA_V + 1] for (_, acc) in carry]
    o_ref[0] = jnp.concatenate(outs, axis=-1).astype(o_ref.dtype)


def _mla_attn(q, k, v, t):
    B, S, _ = q.shape
    return pl.pallas_call(
        functools.partial(_mla_attn_kernel, t=t),
        grid=(B, MLA_HEADS // ATTN_HEADS, S // t),
        in_specs=[pl.BlockSpec((1, t, ATTN_HEADS * LANES), lambda b, h, i: (b, i, h)),
                  pl.BlockSpec((1, S, ATTN_HEADS * LANES), lambda b, h, i: (b, 0, h)),
                  pl.BlockSpec((1, S, ATTN_HEADS * LANES), lambda b, h, i: (b, 0, h))],
        out_specs=pl.BlockSpec((1, t, ATTN_HEADS * MLA_V), lambda b, h, i: (b, i, h)),
        out_shape=jax.ShapeDtypeStruct((B, S, MLA_HEADS * MLA_V), BF16),
        compiler_params=_params(("parallel", "parallel", "arbitrary")),
        name="mla_attn",
    )(q, k, v)


def _retention_kernel(rq_ref, rk_ref, rv_ref, rg_ref, gn_ref, decay_ref, zeta_ref, xi_ref, cd_ref,
                      o_ref, r_ref, *, n_chunks):
    C = RET_CHUNK

    @pl.when(pl.program_id(1) == 0)
    def _():
        r_ref[...] = jnp.zeros(r_ref.shape, F32)

    def chunk(n, carry):
        r0 = pl.multiple_of(n * C, C)
        for h in range(RET_HEADS):
            hs = slice(h * LANES, (h + 1) * LANES)
            qc = rq_ref[0, pl.ds(r0, C), hs]
            kc = rk_ref[0, pl.ds(r0, C), hs]
            vc = rv_ref[0, pl.ds(r0, C), hs]
            state = r_ref[h]
            sc = _dot_nt(qc, kc) * decay_ref[h]
            inner = _dot(sc.astype(BF16), vc)
            cross = _dot(qc, state.astype(BF16)) * xi_ref[h]
            o = inner + cross
            kz = (kc.astype(F32) * zeta_ref[h]).T.astype(BF16)
            r_ref[h] = state * cd_ref[h] + _dot(kz, vc)
            mu = jnp.mean(o, axis=-1, keepdims=True)
            d = o - mu
            var = jnp.mean(d * d, axis=-1, keepdims=True)
            on = d * lax.rsqrt(var + EPS) * gn_ref[:, hs]
            g = rg_ref[0, pl.ds(r0, C), hs]
            o_ref[0, pl.ds(r0, C), hs] = (g * jax.nn.sigmoid(g) * on).astype(o_ref.dtype)
        return carry

    lax.fori_loop(0, n_chunks, chunk, 0, unroll=True)


def _retention(rq, rk, rv, rg, g_gn, decay, zeta, xi, cd, ts):
    B, S, W = rq.shape
    tok = pl.BlockSpec((1, ts, W), lambda b, i: (b, i, 0))
    return pl.pallas_call(
        functools.partial(_retention_kernel, n_chunks=ts // RET_CHUNK),
        grid=(B, S // ts),
        in_specs=[tok, tok, tok, tok, _const_spec(g_gn.shape), _const_spec(decay.shape),
                  _const_spec(zeta.shape), _const_spec(xi.shape), _const_spec(cd.shape)],
        out_specs=tok,
        out_shape=jax.ShapeDtypeStruct((B, S, W), BF16),
        scratch_shapes=[pltpu.VMEM((RET_HEADS, RET_DK, RET_DV), F32)],
        compiler_params=_params(("parallel", "arbitrary")),
        name="retention",
    )(rq, rk, rv, rg, g_gn, decay, zeta, xi, cd)


def _mem_kv_kernel(mem_ref, g_ref, w_ref, k_ref, v_ref):
    D = mem_ref.shape[-1]
    mn = _rms(mem_ref[0], g_ref[...]).astype(BF16)
    kv = _dot(mn, w_ref[...])
    k_ref[0] = kv[:, :D].astype(BF16)
    v_ref[0] = kv[:, D:].astype(BF16)


def _mem_kv(mem, g_mem, w_xkv):
    B, M, D = mem.shape
    blk = pl.BlockSpec((1, M, D), lambda b: (b, 0, 0))
    return pl.pallas_call(
        _mem_kv_kernel,
        grid=(B,),
        in_specs=[blk, _const_spec(g_mem.shape), _const_spec(w_xkv.shape)],
        out_specs=[blk, blk],
        out_shape=[jax.ShapeDtypeStruct((B, M, D), BF16)] * 2,
        compiler_params=_params(("parallel",)),
        name="mem_kv",
    )(mem, g_mem, w_xkv)


def _post_mix_kernel(x_ref, omla_ref, oret_ref, woa_ref, wob_ref, gx_ref, wxq_ref, kx_ref, vx_ref, wxo_ref,
                     gf_ref, wpq_ref, keys_ref, h2_ref, a3t_ref, st_ref, ox_ref):
    D = x_ref.shape[-1]
    h1 = x_ref[...] + _dot(omla_ref[...], woa_ref[...]) + _dot(oret_ref[...], wob_ref[...])
    a2 = _rms(h1, gx_ref[...]).astype(BF16)
    qx = _dot(a2, wxq_ref[...])
    hd = D // X_HEADS
    for h in range(X_HEADS):
        sl = slice(h * hd, (h + 1) * hd)
        s = _dot_nt(qx[:, sl].astype(BF16), kx_ref[0, :, sl]) * (hd ** -0.5)
        m = jnp.max(s, axis=-1, keepdims=True)
        p = jnp.exp(s - m)
        p = p / jnp.sum(p, axis=-1, keepdims=True)
        ox_ref[:, sl] = _dot(p.astype(BF16), vx_ref[0, :, sl]).astype(BF16)
    h2 = h1 + _dot(ox_ref[...], wxo_ref[...])
    h2_ref[...] = h2
    a3 = _rms(h2, gf_ref[...])
    a3t_ref[...] = a3.T.astype(BF16)
    qp = _dot(a3.astype(BF16), wpq_ref[...]).astype(BF16)
    half = PEER_DKEY // 2
    for hp in range(2 * PEER_HEADS):
        sc = _dot_nt(keys_ref[hp % 2], qp[:, hp * half:(hp + 1) * half])
        for tl in range(sc.shape[1] // LANES):
            st_ref[hp, tl] = sc[:, tl * LANES:(tl + 1) * LANES]


def _post_mix(x2, o_mla, o_ret, w_oa, w_ob, g_x, w_xq, kx, vx, w_xo, g_f, w_pq, keys, seq, tm):
    T, D = x2.shape
    nS = seq // tm
    tok = lambda w: pl.BlockSpec((tm, w), lambda i: (i, 0))
    memb = pl.BlockSpec((1,) + kx.shape[1:], lambda i: (i // nS, 0, 0))
    return pl.pallas_call(
        _post_mix_kernel,
        grid=(T // tm,),
        in_specs=[tok(D), tok(o_mla.shape[1]), tok(o_ret.shape[1]), _const_spec(w_oa.shape),
                  _const_spec(w_ob.shape), _const_spec(g_x.shape), _const_spec(w_xq.shape), memb, memb,
                  _const_spec(w_xo.shape), _const_spec(g_f.shape), _const_spec(w_pq.shape),
                  _const_spec(keys.shape)],
        out_specs=[tok(D), pl.BlockSpec((D, tm), lambda i: (0, i)),
                   pl.BlockSpec((2 * PEER_HEADS, tm // LANES, PEER_KEYS, LANES), lambda i: (0, i, 0, 0))],
        out_shape=[jax.ShapeDtypeStruct((T, D), F32), jax.ShapeDtypeStruct((D, T), BF16),
                   jax.ShapeDtypeStruct((2 * PEER_HEADS, T // LANES, PEER_KEYS, LANES), F32)],
        scratch_shapes=[pltpu.VMEM((tm, D), BF16)],
        compiler_params=_params(("parallel",)),
        name="post_mix",
    )(x2, o_mla, o_ret, w_oa, w_ob, g_x, w_xq, kx, vx, w_xo, g_f, w_pq, keys)


N_TOP = PEER_TOPK + 1
TOP_ROWS = 24
SUBLANES = 8


def _sort_network(n):
    def merge(lo, hi, r):
        step = r * 2
        if step < hi - lo:
            yield from merge(lo, hi, step)
            yield from merge(lo + r, hi, step)
            yield from [(i, i + r) for i in range(lo + r, hi - r, step)]
        else:
            yield (lo, lo + r)

    def sort(lo, hi):
        if hi - lo >= 1:
            mid = lo + (hi - lo) // 2
            yield from sort(lo, mid)
            yield from sort(mid + 1, hi)
            yield from merge(lo, hi, 1)

    return list(sort(0, n - 1))


def _top17(x):
    k = PEER_TOPK
    v = [x[SUBLANES * i:SUBLANES * (i + 1), :] for i in range(x.shape[0] // SUBLANES)]
    assert len(v) == k
    for i, j in _sort_network(k):
        v[i], v[j] = jnp.maximum(v[i], v[j]), jnp.minimum(v[i], v[j])
    dropped = jnp.full(v[0].shape, NEG, F32)
    for shift in (SUBLANES // 2, SUBLANES // 4, SUBLANES // 8):
        other = [pltpu.roll(a, SUBLANES - shift, axis=0) for a in v]
        lose = [jnp.minimum(v[i], other[k - 1 - i]) for i in range(k)]
        v = [jnp.maximum(v[i], other[k - 1 - i]) for i in range(k)]
        for a in lose + [pltpu.roll(dropped, SUBLANES - shift, axis=0)]:
            dropped = jnp.maximum(dropped, a)
        d = k // 2
        while d >= 1:
            for i in range(k):
                if i & d == 0:
                    v[i], v[i + d] = jnp.maximum(v[i], v[i + d]), jnp.minimum(v[i], v[i + d])
            d //= 2
    return [a[0:1, :] for a in v], dropped[0:1, :]


def _peer_route_kernel(st_ref, stats_ref, vs_ref):
    tl = st_ref.shape[-1]
    vs_ref[...] = jnp.full(vs_ref.shape, NEG, F32)
    for h in range(PEER_HEADS):
        for p in range(2):
            best, nxt = _top17(st_ref[2 * h + p, 0])
            for it, row in enumerate(best + [nxt]):
                vs_ref[p, it:it + 1, :] = row
        v1 = lambda i: vs_ref[0, i:i + 1, :]
        v2 = lambda i: vs_ref[1, i:i + 1, :]
        cands = [v1(0) + vs_ref[1, 8 * r:8 * r + 8, :] for r in range(TOP_ROWS // 8)]
        cands += [v1(i) + vs_ref[1, 0:8, :] for i in range(1, 8)]
        cands += [vs_ref[0, 8 * r:8 * r + 8, :] + v2(0) for r in range(1, TOP_ROWS // 8)]
        cands += [jnp.full((SUBLANES, tl), NEG, F32)] * (PEER_KEYS // SUBLANES - len(cands))
        best, c17 = _top17(jnp.concatenate(cands, axis=0))
        top = best[0]
        z = jnp.zeros((1, tl), F32)
        for m in best:
            z = z + jnp.exp(m - top)
        stats_ref[0, h:h + 1, :] = 0.5 * (best[-1] + c17)
        stats_ref[1, h:h + 1, :] = v1(0)
        stats_ref[2, h:h + 1, :] = v2(0)
        stats_ref[3, h:h + 1, :] = 1.0 / z


def _peer_route(st):
    HP, nb, K, tl = st.shape
    T = nb * tl
    return pl.pallas_call(
        _peer_route_kernel,
        grid=(nb,),
        in_specs=[pl.BlockSpec((HP, 1, K, tl), lambda i: (0, i, 0, 0))],
        out_specs=pl.BlockSpec((4, PEER_HEADS, tl), lambda i: (0, 0, i)),
        out_shape=jax.ShapeDtypeStruct((4, PEER_HEADS, T), F32),
        scratch_shapes=[pltpu.VMEM((2, TOP_ROWS, tl), F32)],
        compiler_params=_params(("parallel",)),
        name="peer_route",
    )(st)


def _peer_dense_kernel(a3t_ref, st_ref, stats_ref, u_ref, v_ref, h2_ref, g_ref, o_ref,
                       thr_ref, c1_ref, e2_ref, acc_ref, *, te, tm):
    j = pl.program_id(1)
    nk = PEER_KEYS
    n_al = te // nk
    nt = tm // LANES

    @pl.when(j == 0)
    def _():
        acc_ref[...] = jnp.zeros(acc_ref.shape, F32)
        for h in range(PEER_HEADS):
            for tl in range(nt):
                ls = slice(tl * LANES, (tl + 1) * LANES)
                s1 = st_ref[2 * h, tl]
                thr_ref[h, :, tl, :] = stats_ref[0, h:h + 1, ls] - s1
                c1_ref[h, :, tl, :] = jnp.exp(s1 - stats_ref[1, h:h + 1, ls]) * (0.5 * stats_ref[3, h:h + 1, ls])
                e2_ref[h, tl] = jnp.exp(st_ref[2 * h + 1, tl] - stats_ref[2, h:h + 1, ls])

    act = _dot(u_ref[...], a3t_ref[...])
    p_rows = []
    for al in range(n_al):
        a = j * n_al + al
        rs = slice(al * nk, (al + 1) * nk)
        tiles = []
        for tl in range(nt):
            w = None
            for h in range(PEER_HEADS):
                thr = thr_ref[h, a, tl:tl + 1, :]
                c1 = c1_ref[h, a, tl:tl + 1, :]
                term = jnp.where(st_ref[2 * h + 1, tl] >= thr, e2_ref[h, tl] * c1, 0.0)
                w = term if w is None else w + term
            tiles.append((_gelu_x2(act[rs, tl * LANES:(tl + 1) * LANES]) * w).astype(BF16))
        p_rows.append(jnp.concatenate(tiles, axis=1))
    p = jnp.concatenate(p_rows, axis=0)
    acc_ref[...] += lax.dot_general(v_ref[...], p, (((0,), (0,)), ((), ())), preferred_element_type=F32)

    @pl.when(j == pl.num_programs(1) - 1)
    def _():
        o_ref[...] = _rms(h2_ref[...] + acc_ref[...].T, g_ref[...])


def _peer_dense(a3t, st, stats, u_bf, v_bf, h2, g, tm, te):
    D, T = a3t.shape
    nt = tm // LANES
    return pl.pallas_call(
        functools.partial(_peer_dense_kernel, te=te, tm=tm),
        grid=(T // tm, u_bf.shape[0] // te),
        in_specs=[pl.BlockSpec((D, tm), lambda i, j: (0, i)),
                  pl.BlockSpec((st.shape[0], nt) + st.shape[2:], lambda i, j: (0, i, 0, 0)),
                  pl.BlockSpec(stats.shape[:2] + (tm,), lambda i, j: (0, 0, i)),
                  pl.BlockSpec((te, D), lambda i, j: (j, 0)),
                  pl.BlockSpec((te, D), lambda i, j: (j, 0)),
                  pl.BlockSpec((tm, D), lambda i, j: (i, 0)), _const_spec(g.shape)],
        out_specs=pl.BlockSpec((tm, D), lambda i, j: (i, 0)),
        out_shape=jax.ShapeDtypeStruct((T, D), F32),
        scratch_shapes=[pltpu.VMEM((PEER_HEADS, PEER_KEYS, nt, LANES), F32)] * 2
        + [pltpu.VMEM((PEER_HEADS, nt, PEER_KEYS, LANES), F32), pltpu.VMEM((D, tm), F32)],
        compiler_params=_params(("parallel", "arbitrary")),
        name="peer_dense",
    )(a3t, st, stats, u_bf, v_bf, h2, g)


def _rope_tables(seq):
    pos = jnp.arange(seq, dtype=F32)

    def tab(dim):
        inv = 1.0 / (ROPE_BASE ** (jnp.arange(0, dim, 2, dtype=F32) / dim))
        ang = pos[:, None] * inv[None, :]
        return jnp.cos(ang), jnp.sin(ang)

    cm, sm = tab(MLA_ROPE)
    pad = LANES - MLA_NOPE - MLA_ROPE
    cosm = jnp.concatenate([jnp.ones((seq, MLA_NOPE), F32), cm, cm, jnp.zeros((seq, pad), F32)], axis=1)
    sinm = jnp.concatenate([jnp.zeros((seq, MLA_NOPE), F32), -sm, sm, jnp.zeros((seq, pad), F32)], axis=1)
    cr, sr = tab(RET_DK)
    cosr = jnp.concatenate([cr, cr], axis=1)
    sinr = jnp.concatenate([-sr, sr], axis=1)
    return cosm, sinm, cosr, sinr


def _retention_tables():
    C = RET_CHUNK
    log_g = jnp.log(1.0 - 2.0 ** (-5.0 - jnp.arange(RET_HEADS, dtype=F32)))
    idx = jnp.arange(C, dtype=F32)
    rel = idx[:, None] - idx[None, :]
    decay = jnp.where(rel[None] >= 0, jnp.exp(jnp.maximum(rel, 0.0)[None] * log_g[:, None, None]), 0.0)
    zeta = jnp.exp((C - 1 - idx)[None, :] * log_g[:, None])
    xi = jnp.exp((idx + 1)[None, :] * log_g[:, None])
    cd = jnp.exp(C * log_g)
    bc = lambda t: jnp.broadcast_to(t[:, :, None], (RET_HEADS, C, LANES))
    return decay, bc(zeta), bc(xi), jnp.broadcast_to(cd[:, None, None], (RET_HEADS, C, LANES))


def _layer(h, mem, g_mix, w_in, g_q_lora, w_uq, g_kv_lora, w_ukv, g_ret_gn, w_o, g_xattn, g_mem,
           w_xq, w_xkv, w_xo, g_ffn, w_pq, sub_keys, u_experts, v_experts, g_out):
    B, S, D = h.shape
    T = B * S
    row = lambda g: g.reshape(1, -1)

    o_pe = MLA_Q_RANK + MLA_KV_RANK
    pe_pad = jnp.zeros((D, LANES), F32).at[:, MLA_NOPE:MLA_NOPE + MLA_ROPE].set(w_in[:, o_pe:o_pe + MLA_ROPE])
    w_in_ext = jnp.concatenate([w_in[:, :o_pe], pe_pad, w_in[:, o_pe + MLA_ROPE:]], axis=1).astype(BF16)
    dq = MLA_NOPE + MLA_ROPE
    w_uq_pad = jnp.pad(w_uq.reshape(MLA_Q_RANK, MLA_HEADS, dq), ((0, 0), (0, 0), (0, LANES - dq)))
    w_uq_pad = w_uq_pad.reshape(MLA_Q_RANK, MLA_HEADS * LANES).astype(BF16)
    w_ukv3 = w_ukv.reshape(MLA_KV_RANK, MLA_HEADS, MLA_NOPE + MLA_V)
    w_k = jnp.pad(w_ukv3[:, :, :MLA_NOPE], ((0, 0), (0, 0), (0, LANES - MLA_NOPE)))
    w_k = w_k.reshape(MLA_KV_RANK, MLA_HEADS * LANES).astype(BF16)
    w_v = jnp.pad(w_ukv3[:, :, MLA_NOPE:], ((0, 0), (0, 0), (0, LANES - MLA_V)))
    w_v = w_v.reshape(MLA_KV_RANK, MLA_HEADS * LANES).astype(BF16)

    cosm, sinm, cosr, sinr = _rope_tables(S)
    q, k, v, rq, rk, rv, rg = _in_proj(h.reshape(T, D), row(g_mix), w_in_ext, row(g_q_lora), w_uq_pad,
                                       row(g_kv_lora), w_k, w_v, cosm, sinm, cosr, sinr, S, min(1024, S))
    r3 = lambda t: t.reshape(B, S, t.shape[-1])
    o_mla = _mla_attn(r3(q), r3(k), r3(v), min(1024, S))
    o_ret = _retention(r3(rq), r3(rk), r3(rv), r3(rg), row(g_ret_gn), *_retention_tables(), min(1024, S))

    kx, vx = _mem_kv(mem, row(g_mem), w_xkv.astype(BF16))
    n_mla = MLA_HEADS * MLA_V
    w_o_bf = w_o.astype(BF16)
    h2, a3t, st = _post_mix(h.reshape(T, D), o_mla.reshape(T, -1), o_ret.reshape(T, -1), w_o_bf[:n_mla],
                            w_o_bf[n_mla:], row(g_xattn), w_xq.astype(BF16), kx, vx, w_xo.astype(BF16),
                            row(g_ffn), w_pq.astype(BF16), sub_keys.astype(BF16), S, min(512, S))
    stats = _peer_route(st)
    out = _peer_dense(a3t, st, stats, u_experts.astype(BF16), v_experts.astype(BF16), h2, row(g_out),
                      min(512, T), 2048)
    return out.reshape(B, S, D)


def kernel(x, mem, g_mix, w_in, g_q_lora, w_uq, g_kv_lora, w_ukv, g_ret_gn, w_o, g_xattn, g_mem, w_xq, w_xkv,
           w_xo, g_ffn, w_pq, sub_keys, u_experts, v_experts, g_final):
    depth = g_mix.shape[0]
    assert depth == 1, "the final norm is fused into the single layer's last kernel"
    l = 0
    return _layer(x, mem, g_mix[l], w_in[l], g_q_lora[l], w_uq[l], g_kv_lora[l], w_ukv[l], g_ret_gn[l], w_o[l],
                  g_xattn[l], g_mem[l], w_xq[l], w_xkv[l], w_xo[l], g_ffn[l], w_pq[l], sub_keys[l],
                  u_experts[l], v_experts[l], g_final)
```

```python
import functools
import math

import jax
import jax.numpy as jnp
from jax import lax
from jax.experimental import pallas as pl
from jax.experimental.pallas import tpu as pltpu

F32 = jnp.float32
BF16 = jnp.bfloat16

LANES = 128
EPS = 1e-6
NEG = -1e30
ROPE_BASE = 10000.0
VMEM_LIMIT = 56 * 1024 * 1024

MLA_HEADS, MLA_Q_RANK, MLA_KV_RANK = 8, 384, 256
MLA_NOPE, MLA_ROPE, MLA_V = 64, 32, 64
RET_HEADS, RET_DK, RET_DV, RET_CHUNK = 4, 128, 128, 128
X_HEADS = 4
PEER_KEYS, PEER_HEADS, PEER_TOPK, PEER_DKEY = 128, 8, 16, 256


def _params(sem, vmem=VMEM_LIMIT):
    return pltpu.CompilerParams(dimension_semantics=sem, vmem_limit_bytes=vmem)


def _rms(x, g):
    return x * lax.rsqrt(jnp.mean(x * x, axis=-1, keepdims=True) + EPS) * g


def _dot(a, b):
    return jnp.dot(a, b, preferred_element_type=F32)


def _dot_nt(a, b):
    return lax.dot_general(a, b, (((1,), (1,)), ((), ())), preferred_element_type=F32)


def _gelu_x2(x):
    return x * (1.0 + lax.erf(x * (2.0 ** -0.5)))


def _const_spec(shape):
    nd = len(shape)
    return pl.BlockSpec(shape, lambda *_: (0,) * nd)


def _in_proj_kernel(x_ref, gmix_ref, win_ref, gq_ref, wuq_ref, gkv_ref, wk_ref, wv_ref,
                    cosm_ref, sinm_ref, cosr_ref, sinr_ref,
                    q_ref, k_ref, v_ref, rq_ref, rk_ref, rv_ref, rg_ref):
    x = x_ref[...]
    a = _rms(x, gmix_ref[...]).astype(BF16)
    proj = _dot(a, win_ref[...])
    o_cq, o_ckv, o_pe, o_rq = 0, MLA_Q_RANK, MLA_Q_RANK + MLA_KV_RANK, MLA_Q_RANK + MLA_KV_RANK + LANES
    rw = RET_HEADS * RET_DK
    cq = proj[:, o_cq:o_ckv]
    ckv = proj[:, o_ckv:o_pe]
    kpe = proj[:, o_pe:o_rq]
    cqn = _rms(cq, gq_ref[...]).astype(BF16)
    ckvn = _rms(ckv, gkv_ref[...]).astype(BF16)
    q = _dot(cqn, wuq_ref[...])
    kn = _dot(ckvn, wk_ref[...])
    vlane = lax.broadcasted_iota(jnp.int32, (x.shape[0], MLA_HEADS * LANES), 1) % LANES
    v_ref[...] = jnp.where(vlane == MLA_V, 1.0, _dot(ckvn, wv_ref[...])).astype(BF16)

    cosm, sinm = cosm_ref[...], sinm_ref[...]
    lane = lax.broadcasted_iota(jnp.int32, (x.shape[0], LANES), 1)
    first_half = lane < MLA_NOPE + MLA_ROPE // 2

    def rope_m(c):
        rot = jnp.where(first_half, pltpu.roll(c, LANES - MLA_ROPE // 2, axis=1),
                        pltpu.roll(c, MLA_ROPE // 2, axis=1))
        return c * cosm + rot * sinm

    qscale = (MLA_NOPE + MLA_ROPE) ** -0.5 * math.log2(math.e)
    kpe_r = rope_m(kpe)
    for h in range(MLA_HEADS):
        sl = slice(h * LANES, (h + 1) * LANES)
        q_ref[:, sl] = (rope_m(q[:, sl]) * qscale).astype(BF16)
        k_ref[:, sl] = (kn[:, sl] + kpe_r).astype(BF16)

    cosr, sinr = cosr_ref[...], sinr_ref[...]
    kscale = RET_DK ** -0.5
    for h in range(RET_HEADS):
        sl = slice(h * LANES, (h + 1) * LANES)
        c = proj[:, o_rq + h * LANES:o_rq + (h + 1) * LANES]
        rq_ref[:, sl] = (c * cosr + pltpu.roll(c, RET_DK // 2, axis=1) * sinr).astype(BF16)
        c = proj[:, o_rq + rw + h * LANES:o_rq + rw + (h + 1) * LANES]
        rk_ref[:, sl] = ((c * cosr + pltpu.roll(c, RET_DK // 2, axis=1) * sinr) * kscale).astype(BF16)
    rv_ref[...] = proj[:, o_rq + 2 * rw:o_rq + 3 * rw].astype(BF16)
    rg_ref[...] = proj[:, o_rq + 3 * rw:o_rq + 4 * rw]


def _in_proj(x2, g_mix, w_in_ext, g_q, w_uq_pad, g_kv, w_k, w_v, cosm, sinm, cosr, sinr, seq, tm):
    T, D = x2.shape
    nS = seq // tm
    tok = lambda w: pl.BlockSpec((tm, w), lambda i: (i, 0))
    pos = lambda w: pl.BlockSpec((tm, w), lambda i: (i % nS, 0))
    outs = [(T, MLA_HEADS * LANES, BF16), (T, MLA_HEADS * LANES, BF16), (T, MLA_HEADS * LANES, BF16),
            (T, RET_HEADS * RET_DK, BF16), (T, RET_HEADS * RET_DK, BF16), (T, RET_HEADS * RET_DV, BF16),
            (T, RET_HEADS * RET_DV, F32)]
    return pl.pallas_call(
        _in_proj_kernel,
        grid=(T // tm,),
        in_specs=[tok(D), _const_spec(g_mix.shape), _const_spec(w_in_ext.shape), _const_spec(g_q.shape),
                  _const_spec(w_uq_pad.shape), _const_spec(g_kv.shape), _const_spec(w_k.shape),
                  _const_spec(w_v.shape), pos(LANES), pos(LANES), pos(LANES), pos(LANES)],
        out_specs=[tok(w) for (_, w, _) in outs],
        out_shape=[jax.ShapeDtypeStruct((t, w), d) for (t, w, d) in outs],
        compiler_params=_params(("parallel",)),
        name="in_proj",
    )(x2, g_mix, w_in_ext, g_q, w_uq_pad, g_kv, w_k, w_v, cosm, sinm, cosr, sinr)


ATTN_HEADS = 4


def _mla_attn_kernel(q_ref, k_ref, v_ref, o_ref, *, t):
    i = pl.program_id(2)
    causal = (lax.broadcasted_iota(jnp.int32, (t, t), 1) <= lax.broadcasted_iota(jnp.int32, (t, t), 0))

    def step(j, carry, masked):
        k0 = pl.multiple_of(j * t, t)
        new = []
        for hh in range(ATTN_HEADS):
            m, acc = carry[hh]
            hs = slice(hh * LANES, (hh + 1) * LANES)
            s = _dot_nt(q_ref[0, :, hs], k_ref[0, pl.ds(k0, t), hs])
            if masked:
                s = jnp.where(causal, s, NEG)
            m_new = jnp.maximum(m, jnp.max(s, axis=-1, keepdims=True))
            p = jnp.exp2(s - m_new).astype(BF16)
            acc = jnp.exp2(m - m_new) * acc + _dot(p, v_ref[0, pl.ds(k0, t), hs])
            new.append((m_new, acc))
        return tuple(new)

    init = tuple((jnp.full((t, 1), NEG, F32), jnp.zeros((t, LANES), F32)) for _ in range(ATTN_HEADS))
    carry = lax.fori_loop(0, i, functools.partial(step, masked=False), init)
    carry = step(i, carry, True)
    outs = [acc[:, :MLA_V] / acc[:, MLA_V:MLA_V + 1] for (_, acc) in carry]
    o_ref[0] = jnp.concatenate(outs, axis=-1).astype(o_ref.dtype)


def _mla_attn(q, k, v, t):
    B, S, _ = q.shape
    return pl.pallas_call(
        functools.partial(_mla_attn_kernel, t=t),
        grid=(B, MLA_HEADS // ATTN_HEADS, S // t),
        in_specs=[pl.BlockSpec((1, t, ATTN_HEADS * LANES), lambda b, h, i: (b, i, h)),
                  pl.BlockSpec((1, S, ATTN_HEADS * LANES), lambda b, h, i: (b, 0, h)),
                  pl.BlockSpec((1, S, ATTN_HEADS * LANES), lambda b, h, i: (b, 0, h))],
        out_specs=pl.BlockSpec((1, t, ATTN_HEADS * MLA_V), lambda b, h, i: (b, i, h)),
        out_shape=jax.ShapeDtypeStruct((B, S, MLA_HEADS * MLA_V), BF16),
        compiler_params=_params(("parallel", "parallel", "arbitrary")),
        name="mla_attn",
    )(q, k, v)


def _retention_kernel(rq_ref, rk_ref, rv_ref, rg_ref, gn_ref, decay_ref, zeta_ref, xi_ref, cd_ref,
                      o_ref, r_ref, *, n_chunks):
    C = RET_CHUNK

    @pl.when(pl.program_id(1) == 0)
    def _():
        r_ref[...] = jnp.zeros(r_ref.shape, F32)

    def chunk(n, carry):
        r0 = pl.multiple_of(n * C, C)
        for h in range(RET_HEADS):
            hs = slice(h * LANES, (h + 1) * LANES)
            qc = rq_ref[0, pl.ds(r0, C), hs]
            kc = rk_ref[0, pl.ds(r0, C), hs]
            vc = rv_ref[0, pl.ds(r0, C), hs]
            state = r_ref[h]
            sc = _dot_nt(qc, kc) * decay_ref[h]
            inner = _dot(sc.astype(BF16), vc)
            cross = _dot(qc, state.astype(BF16)) * xi_ref[h]
            o = inner + cross
            kz = (kc.astype(F32) * zeta_ref[h]).T.astype(BF16)
            r_ref[h] = state * cd_ref[h] + _dot(kz, vc)
            mu = jnp.mean(o, axis=-1, keepdims=True)
            d = o - mu
            var = jnp.mean(d * d, axis=-1, keepdims=True)
            on = d * lax.rsqrt(var + EPS) * gn_ref[:, hs]
            g = rg_ref[0, pl.ds(r0, C), hs]
            o_ref[0, pl.ds(r0, C), hs] = (g * jax.nn.sigmoid(g) * on).astype(o_ref.dtype)
        return carry

    lax.fori_loop(0, n_chunks, chunk, 0, unroll=True)


def _retention(rq, rk, rv, rg, g_gn, decay, zeta, xi, cd, ts):
    B, S, W = rq.shape
    tok = pl.BlockSpec((1, ts, W), lambda b, i: (b, i, 0))
    return pl.pallas_call(
        functools.partial(_retention_kernel, n_chunks=ts // RET_CHUNK),
        grid=(B, S // ts),
        in_specs=[tok, tok, tok, tok, _const_spec(g_gn.shape), _const_spec(decay.shape),
                  _const_spec(zeta.shape), _const_spec(xi.shape), _const_spec(cd.shape)],
        out_specs=tok,
        out_shape=jax.ShapeDtypeStruct((B, S, W), BF16),
        scratch_shapes=[pltpu.VMEM((RET_HEADS, RET_DK, RET_DV), F32)],
        compiler_params=_params(("parallel", "arbitrary")),
        name="retention",
    )(rq, rk, rv, rg, g_gn, decay, zeta, xi, cd)


def _mem_kv_kernel(mem_ref, g_ref, w_ref, k_ref, v_ref):
    D = mem_ref.shape[-1]
    mn = _rms(mem_ref[0], g_ref[...]).astype(BF16)
    kv = _dot(mn, w_ref[...])
    k_ref[0] = kv[:, :D].astype(BF16)
    v_ref[0] = kv[:, D:].astype(BF16)


def _mem_kv(mem, g_mem, w_xkv):
    B, M, D = mem.shape
    blk = pl.BlockSpec((1, M, D), lambda b: (b, 0, 0))
    return pl.pallas_call(
        _mem_kv_kernel,
        grid=(B,),
        in_specs=[blk, _const_spec(g_mem.shape), _const_spec(w_xkv.shape)],
        out_specs=[blk, blk],
        out_shape=[jax.ShapeDtypeStruct((B, M, D), BF16)] * 2,
        compiler_params=_params(("parallel",)),
        name="mem_kv",
    )(mem, g_mem, w_xkv)


def _post_mix_kernel(x_ref, omla_ref, oret_ref, woa_ref, wob_ref, gx_ref, wxq_ref, kx_ref, vx_ref, wxo_ref,
                     gf_ref, wpq_ref, keys_ref, h2_ref, a3t_ref, st_ref, ox_ref):
    D = x_ref.shape[-1]
    h1 = x_ref[...] + _dot(omla_ref[...], woa_ref[...]) + _dot(oret_ref[...], wob_ref[...])
    a2 = _rms(h1, gx_ref[...]).astype(BF16)
    qx = _dot(a2, wxq_ref[...])
    hd = D // X_HEADS
    for h in range(X_HEADS):
        sl = slice(h * hd, (h + 1) * hd)
        s = _dot_nt(qx[:, sl].astype(BF16), kx_ref[0, :, sl]) * (hd ** -0.5)
        m = jnp.max(s, axis=-1, keepdims=True)
        p = jnp.exp(s - m)
        p = p / jnp.sum(p, axis=-1, keepdims=True)
        ox_ref[:, sl] = _dot(p.astype(BF16), vx_ref[0, :, sl]).astype(BF16)
    h2 = h1 + _dot(ox_ref[...], wxo_ref[...])
    h2_ref[...] = h2
    a3 = _rms(h2, gf_ref[...])
    a3t_ref[...] = a3.T.astype(BF16)
    qp = _dot(a3.astype(BF16), wpq_ref[...]).astype(BF16)
    half = PEER_DKEY // 2
    for hp in range(2 * PEER_HEADS):
        sc = _dot_nt(keys_ref[hp % 2], qp[:, hp * half:(hp + 1) * half])
        for tl in range(sc.shape[1] // LANES):
            st_ref[hp, tl] = sc[:, tl * LANES:(tl + 1) * LANES]


def _post_mix(x2, o_mla, o_ret, w_oa, w_ob, g_x, w_xq, kx, vx, w_xo, g_f, w_pq, keys, seq, tm):
    T, D = x2.shape
    nS = seq // tm
    tok = lambda w: pl.BlockSpec((tm, w), lambda i: (i, 0))
    memb = pl.BlockSpec((1,) + kx.shape[1:], lambda i: (i // nS, 0, 0))
    return pl.pallas_call(
        _post_mix_kernel,
        grid=(T // tm,),
        in_specs=[tok(D), tok(o_mla.shape[1]), tok(o_ret.shape[1]), _const_spec(w_oa.shape),
                  _const_spec(w_ob.shape), _const_spec(g_x.shape), _const_spec(w_xq.shape), memb, memb,
                  _const_spec(w_xo.shape), _const_spec(g_f.shape), _const_spec(w_pq.shape),
                  _const_spec(keys.shape)],
        out_specs=[tok(D), pl.BlockSpec((D, tm), lambda i: (0, i)),
                   pl.BlockSpec((2 * PEER_HEADS, tm // LANES, PEER_KEYS, LANES), lambda i: (0, i, 0, 0))],
        out_shape=[jax.ShapeDtypeStruct((T, D), F32), jax.ShapeDtypeStruct((D, T), BF16),
                   jax.ShapeDtypeStruct((2 * PEER_HEADS, T // LANES, PEER_KEYS, LANES), F32)],
        scratch_shapes=[pltpu.VMEM((tm, D), BF16)],
        compiler_params=_params(("parallel",)),
        name="post_mix",
    )(x2, o_mla, o_ret, w_oa, w_ob, g_x, w_xq, kx, vx, w_xo, g_f, w_pq, keys)


N_TOP = PEER_TOPK + 1
TOP_ROWS = 24
SUBLANES = 8


def _sort_network(n):
    def merge(lo, hi, r):
        step = r * 2
        if step < hi - lo:
            yield from merge(lo, hi, step)
            yield from merge(lo + r, hi, step)
            yield from [(i, i + r) for i in range(lo + r, hi - r, step)]
        else:
            yield (lo, lo + r)

    def sort(lo, hi):
        if hi - lo >= 1:
            mid = lo + (hi - lo) // 2
            yield from sort(lo, mid)
            yield from sort(mid + 1, hi)
            yield from merge(lo, hi, 1)

    return list(sort(0, n - 1))


def _top17(x):
    k = PEER_TOPK
    v = [x[SUBLANES * i:SUBLANES * (i + 1), :] for i in range(x.shape[0] // SUBLANES)]
    assert len(v) == k
    for i, j in _sort_network(k):
        v[i], v[j] = jnp.maximum(v[i], v[j]), jnp.minimum(v[i], v[j])
    dropped = jnp.full(v[0].shape, NEG, F32)
    for shift in (SUBLANES // 2, SUBLANES // 4, SUBLANES // 8):
        other = [pltpu.roll(a, SUBLANES - shift, axis=0) for a in v]
        lose = [jnp.minimum(v[i], other[k - 1 - i]) for i in range(k)]
        v = [jnp.maximum(v[i], other[k - 1 - i]) for i in range(k)]
        for a in lose + [pltpu.roll(dropped, SUBLANES - shift, axis=0)]:
            dropped = jnp.maximum(dropped, a)
        d = k // 2
        while d >= 1:
            for i in range(k):
                if i & d == 0:
                    v[i], v[i + d] = jnp.maximum(v[i], v[i + d]), jnp.minimum(v[i], v[i + d])
            d //= 2
    return [a[0:1, :] for a in v], dropped[0:1, :]


def _peer_route_kernel(st_ref, stats_ref, vs_ref):
    tl = st_ref.shape[-1]
    vs_ref[...] = jnp.full(vs_ref.shape, NEG, F32)
    for h in range(PEER_HEADS):
        for p in range(2):
            best, nxt = _top17(st_ref[2 * h + p, 0])
            for it, row in enumerate(best + [nxt]):
                vs_ref[p, it:it + 1, :] = row
        v1 = lambda i: vs_ref[0, i:i + 1, :]
        v2 = lambda i: vs_ref[1, i:i + 1, :]
        cands = [v1(0) + vs_ref[1, 8 * r:8 * r + 8, :] for r in range(TOP_ROWS // 8)]
        cands += [v1(i) + vs_ref[1, 0:8, :] for i in range(1, 8)]
        cands += [vs_ref[0, 8 * r:8 * r + 8, :] + v2(0) for r in range(1, TOP_ROWS // 8)]
        cands += [jnp.full((SUBLANES, tl), NEG, F32)] * (PEER_KEYS // SUBLANES - len(cands))
        best, c17 = _top17(jnp.concatenate(cands, axis=0))
        top = best[0]
        z = jnp.zeros((1, tl), F32)
        for m in best:
            z = z + jnp.exp(m - top)
        stats_ref[0, h:h + 1, :] = 0.5 * (best[-1] + c17)
        stats_ref[1, h:h + 1, :] = v1(0)
        stats_ref[2, h:h + 1, :] = v2(0)
        stats_ref[3, h:h + 1, :] = 1.0 / z


def _peer_route(st):
    HP, nb, K, tl = st.shape
    T = nb * tl
    return pl.pallas_call(
        _peer_route_kernel,
        grid=(nb,),
        in_specs=[pl.BlockSpec((HP, 1, K, tl), lambda i: (0, i, 0, 0))],
        out_specs=pl.BlockSpec((4, PEER_HEADS, tl), lambda i: (0, 0, i)),
        out_shape=jax.ShapeDtypeStruct((4, PEER_HEADS, T), F32),
        scratch_shapes=[pltpu.VMEM((2, TOP_ROWS, tl), F32)],
        compiler_params=_params(("parallel",)),
        name="peer_route",
    )(st)


EXPERT_CHUNKS = 2


def _peer_dense_kernel(a3t_ref, st_ref, stats_ref, u_ref, v_ref, h2_ref, g_ref, o_ref,
                       thr_ref, c1_ref, e2_ref, acc_ref, *, te, tm):
    j = pl.program_id(1)
    nk = PEER_KEYS
    n_al = te // nk
    nt = tm // LANES

    @pl.when(j == 0)
    def _():
        acc_ref[...] = jnp.zeros(acc_ref.shape, F32)
        for h in range(PEER_HEADS):
            for tl in range(nt):
                ls = slice(tl * LANES, (tl + 1) * LANES)
                s1 = st_ref[2 * h, tl]
                thr_ref[h, :, tl, :] = stats_ref[0, h:h + 1, ls] - s1
                c1_ref[h, :, tl, :] = jnp.exp(s1 - stats_ref[1, h:h + 1, ls]) * (0.5 * stats_ref[3, h:h + 1, ls])
                e2_ref[h, tl] = jnp.exp(st_ref[2 * h + 1, tl] - stats_ref[2, h:h + 1, ls])

    ce = te // EXPERT_CHUNKS
    acts = [_dot(u_ref[c * ce:(c + 1) * ce, :], a3t_ref[...]) for c in range(EXPERT_CHUNKS)]
    for c in range(EXPERT_CHUNKS):
        p_rows = []
        for al in range(ce // nk):
            a = j * n_al + c * (ce // nk) + al
            rs = slice(al * nk, (al + 1) * nk)
            tiles = []
            for tl in range(nt):
                w = None
                for h in range(PEER_HEADS):
                    thr = thr_ref[h, a, tl:tl + 1, :]
                    c1 = c1_ref[h, a, tl:tl + 1, :]
                    term = jnp.where(st_ref[2 * h + 1, tl] >= thr, e2_ref[h, tl] * c1, 0.0)
                    w = term if w is None else w + term
                tiles.append((_gelu_x2(acts[c][rs, tl * LANES:(tl + 1) * LANES]) * w).astype(BF16))
            p_rows.append(jnp.concatenate(tiles, axis=1))
        p = jnp.concatenate(p_rows, axis=0)
        acc_ref[...] += lax.dot_general(v_ref[c * ce:(c + 1) * ce, :], p, (((0,), (0,)), ((), ())),
                                        preferred_element_type=F32)

    @pl.when(j == pl.num_programs(1) - 1)
    def _():
        o_ref[...] = _rms(h2_ref[...] + acc_ref[...].T, g_ref[...])


def _peer_dense(a3t, st, stats, u_bf, v_bf, h2, g, tm, te):
    D, T = a3t.shape
    nt = tm // LANES
    return pl.pallas_call(
        functools.partial(_peer_dense_kernel, te=te, tm=tm),
        grid=(T // tm, u_bf.shape[0] // te),
        in_specs=[pl.BlockSpec((D, tm), lambda i, j: (0, i)),
                  pl.BlockSpec((st.shape[0], nt) + st.shape[2:], lambda i, j: (0, i, 0, 0)),
                  pl.BlockSpec(stats.shape[:2] + (tm,), lambda i, j: (0, 0, i)),
                  pl.BlockSpec((te, D), lambda i, j: (j, 0)),
                  pl.BlockSpec((te, D), lambda i, j: (j, 0)),
                  pl.BlockSpec((tm, D), lambda i, j: (i, 0)), _const_spec(g.shape)],
        out_specs=pl.BlockSpec((tm, D), lambda i, j: (i, 0)),
        out_shape=jax.ShapeDtypeStruct((T, D), F32),
        scratch_shapes=[pltpu.VMEM((PEER_HEADS, PEER_KEYS, nt, LANES), F32)] * 2
        + [pltpu.VMEM((PEER_HEADS, nt, PEER_KEYS, LANES), F32), pltpu.VMEM((D, tm), F32)],
        compiler_params=_params(("parallel", "arbitrary")),
        name="peer_dense",
    )(a3t, st, stats, u_bf, v_bf, h2, g)


def _rope_tables(seq):
    pos = jnp.arange(seq, dtype=F32)

    def tab(dim):
        inv = 1.0 / (ROPE_BASE ** (jnp.arange(0, dim, 2, dtype=F32) / dim))
        ang = pos[:, None] * inv[None, :]
        return jnp.cos(ang), jnp.sin(ang)

    cm, sm = tab(MLA_ROPE)
    pad = LANES - MLA_NOPE - MLA_ROPE
    cosm = jnp.concatenate([jnp.ones((seq, MLA_NOPE), F32), cm, cm, jnp.zeros((seq, pad), F32)], axis=1)
    sinm = jnp.concatenate([jnp.zeros((seq, MLA_NOPE), F32), -sm, sm, jnp.zeros((seq, pad), F32)], axis=1)
    cr, sr = tab(RET_DK)
    cosr = jnp.concatenate([cr, cr], axis=1)
    sinr = jnp.concatenate([-sr, sr], axis=1)
    return cosm, sinm, cosr, sinr


def _retention_tables():
    C = RET_CHUNK
    log_g = jnp.log(1.0 - 2.0 ** (-5.0 - jnp.arange(RET_HEADS, dtype=F32)))
    idx = jnp.arange(C, dtype=F32)
    rel = idx[:, None] - idx[None, :]
    decay = jnp.where(rel[None] >= 0, jnp.exp(jnp.maximum(rel, 0.0)[None] * log_g[:, None, None]), 0.0)
    zeta = jnp.exp((C - 1 - idx)[None, :] * log_g[:, None])
    xi = jnp.exp((idx + 1)[None, :] * log_g[:, None])
    cd = jnp.exp(C * log_g)
    bc = lambda t: jnp.broadcast_to(t[:, :, None], (RET_HEADS, C, LANES))
    return decay, bc(zeta), bc(xi), jnp.broadcast_to(cd[:, None, None], (RET_HEADS, C, LANES))


def _layer(h, mem, g_mix, w_in, g_q_lora, w_uq, g_kv_lora, w_ukv, g_ret_gn, w_o, g_xattn, g_mem,
           w_xq, w_xkv, w_xo, g_ffn, w_pq, sub_keys, u_experts, v_experts, g_out):
    B, S, D = h.shape
    T = B * S
    row = lambda g: g.reshape(1, -1)

    o_pe = MLA_Q_RANK + MLA_KV_RANK
    pe_pad = jnp.zeros((D, LANES), F32).at[:, MLA_NOPE:MLA_NOPE + MLA_ROPE].set(w_in[:, o_pe:o_pe + MLA_ROPE])
    w_in_ext = jnp.concatenate([w_in[:, :o_pe], pe_pad, w_in[:, o_pe + MLA_ROPE:]], axis=1).astype(BF16)
    dq = MLA_NOPE + MLA_ROPE
    w_uq_pad = jnp.pad(w_uq.reshape(MLA_Q_RANK, MLA_HEADS, dq), ((0, 0), (0, 0), (0, LANES - dq)))
    w_uq_pad = w_uq_pad.reshape(MLA_Q_RANK, MLA_HEADS * LANES).astype(BF16)
    w_ukv3 = w_ukv.reshape(MLA_KV_RANK, MLA_HEADS, MLA_NOPE + MLA_V)
    w_k = jnp.pad(w_ukv3[:, :, :MLA_NOPE], ((0, 0), (0, 0), (0, LANES - MLA_NOPE)))
    w_k = w_k.reshape(MLA_KV_RANK, MLA_HEADS * LANES).astype(BF16)
    w_v = jnp.pad(w_ukv3[:, :, MLA_NOPE:], ((0, 0), (0, 0), (0, LANES - MLA_V)))
    w_v = w_v.reshape(MLA_KV_RANK, MLA_HEADS * LANES).astype(BF16)

    cosm, sinm, cosr, sinr = _rope_tables(S)
    q, k, v, rq, rk, rv, rg = _in_proj(h.reshape(T, D), row(g_mix), w_in_ext, row(g_q_lora), w_uq_pad,
                                       row(g_kv_lora), w_k, w_v, cosm, sinm, cosr, sinr, S, min(1024, S))
    r3 = lambda t: t.reshape(B, S, t.shape[-1])
    o_mla = _mla_attn(r3(q), r3(k), r3(v), min(1024, S))
    o_ret = _retention(r3(rq), r3(rk), r3(rv), r3(rg), row(g_ret_gn), *_retention_tables(), min(1024, S))

    kx, vx = _mem_kv(mem, row(g_mem), w_xkv.astype(BF16))
    n_mla = MLA_HEADS * MLA_V
    w_o_bf = w_o.astype(BF16)
    h2, a3t, st = _post_mix(h.reshape(T, D), o_mla.reshape(T, -1), o_ret.reshape(T, -1), w_o_bf[:n_mla],
                            w_o_bf[n_mla:], row(g_xattn), w_xq.astype(BF16), kx, vx, w_xo.astype(BF16),
                            row(g_ffn), w_pq.astype(BF16), sub_keys.astype(BF16), S, min(512, S))
    stats = _peer_route(st)
    out = _peer_dense(a3t, st, stats, u_experts.astype(BF16), v_experts.astype(BF16), h2, row(g_out),
                      min(512, T), 2048)
    return out.reshape(B, S, D)


def kernel(x, mem, g_mix, w_in, g_q_lora, w_uq, g_kv_lora, w_ukv, g_ret_gn, w_o, g_xattn, g_mem, w_xq, w_xkv,
           w_xo, g_ffn, w_pq, sub_keys, u_experts, v_experts, g_final):
    depth = g_mix.shape[0]
    assert depth == 1, "the final norm is fused into the single layer's last kernel"
    l = 0
    return _layer(x, mem, g_mix[l], w_in[l], g_q_lora[l], w_uq[l], g_kv_lora[l], w_ukv[l], g_ret_gn[l], w_o[l],
                  g_xattn[l], g_mem[l], w_xq[l], w_xkv[l], w_xo[l], g_ffn[l], w_pq[l], sub_keys[l],
                  u_experts[l], v_experts[l], g_final)
```

```python
import functools
import math

import jax
import jax.numpy as jnp
import numpy as np
from jax import lax
from jax.experimental import pallas as pl
from jax.experimental.pallas import tpu as pltpu

F32 = jnp.float32
BF16 = jnp.bfloat16

LANES = 128
EPS = 1e-6
NEG = -1e30
ROPE_BASE = 10000.0
VMEM_LIMIT = 56 * 1024 * 1024

MLA_HEADS, MLA_Q_RANK, MLA_KV_RANK = 8, 384, 256
MLA_NOPE, MLA_ROPE, MLA_V = 64, 32, 64
RET_HEADS, RET_DK, RET_DV, RET_CHUNK = 4, 128, 128, 128
X_HEADS = 4
PEER_KEYS, PEER_HEADS, PEER_TOPK, PEER_DKEY = 128, 8, 16, 256


def _params(sem, vmem=VMEM_LIMIT):
    return pltpu.CompilerParams(dimension_semantics=sem, vmem_limit_bytes=vmem)


def _rms(x, g):
    return x * lax.rsqrt(jnp.mean(x * x, axis=-1, keepdims=True) + EPS) * g


def _dot(a, b):
    return jnp.dot(a, b, preferred_element_type=F32)


def _dot_nt(a, b):
    return lax.dot_general(a, b, (((1,), (1,)), ((), ())), preferred_element_type=F32)


def _gelu_x2(x):
    return x * (1.0 + lax.erf(x * (2.0 ** -0.5)))


def _const_spec(shape):
    nd = len(shape)
    return pl.BlockSpec(shape, lambda *_: (0,) * nd)


def _in_proj_kernel(x_ref, gmix_ref, win_ref, gq_ref, wuq_ref, gkv_ref, wk_ref, wv_ref,
                    cosm_ref, sinm_ref, cosr_ref, sinr_ref,
                    q_ref, k_ref, v_ref, rq_ref, rk_ref, rv_ref, rg_ref):
    x = x_ref[...]
    a = _rms(x, gmix_ref[...]).astype(BF16)
    proj = _dot(a, win_ref[...])
    o_cq, o_ckv, o_pe, o_rq = 0, MLA_Q_RANK, MLA_Q_RANK + MLA_KV_RANK, MLA_Q_RANK + MLA_KV_RANK + LANES
    rw = RET_HEADS * RET_DK
    cq = proj[:, o_cq:o_ckv]
    ckv = proj[:, o_ckv:o_pe]
    kpe = proj[:, o_pe:o_rq]
    cqn = _rms(cq, gq_ref[...]).astype(BF16)
    ckvn = _rms(ckv, gkv_ref[...]).astype(BF16)
    q = _dot(cqn, wuq_ref[...])
    kn = _dot(ckvn, wk_ref[...])
    vlane = lax.broadcasted_iota(jnp.int32, (x.shape[0], MLA_HEADS * LANES), 1) % LANES
    v_ref[...] = jnp.where(vlane == MLA_V, 1.0, _dot(ckvn, wv_ref[...])).astype(BF16)

    cosm, sinm = cosm_ref[...], sinm_ref[...]
    lane = lax.broadcasted_iota(jnp.int32, (x.shape[0], LANES), 1)
    first_half = lane < MLA_NOPE + MLA_ROPE // 2

    def rope_m(c):
        rot = jnp.where(first_half, pltpu.roll(c, LANES - MLA_ROPE // 2, axis=1),
                        pltpu.roll(c, MLA_ROPE // 2, axis=1))
        return c * cosm + rot * sinm

    qscale = (MLA_NOPE + MLA_ROPE) ** -0.5 * math.log2(math.e)
    kpe_r = rope_m(kpe)
    for h in range(MLA_HEADS):
        sl = slice(h * LANES, (h + 1) * LANES)
        q_ref[:, sl] = (rope_m(q[:, sl]) * qscale).astype(BF16)
        k_ref[:, sl] = (kn[:, sl] + kpe_r).astype(BF16)

    cosr, sinr = cosr_ref[...], sinr_ref[...]
    kscale = RET_DK ** -0.5
    for h in range(RET_HEADS):
        sl = slice(h * LANES, (h + 1) * LANES)
        c = proj[:, o_rq + h * LANES:o_rq + (h + 1) * LANES]
        rq_ref[:, sl] = (c * cosr + pltpu.roll(c, RET_DK // 2, axis=1) * sinr).astype(BF16)
        c = proj[:, o_rq + rw + h * LANES:o_rq + rw + (h + 1) * LANES]
        rk_ref[:, sl] = ((c * cosr + pltpu.roll(c, RET_DK // 2, axis=1) * sinr) * kscale).astype(BF16)
    rv_ref[...] = proj[:, o_rq + 2 * rw:o_rq + 3 * rw].astype(BF16)
    rg_ref[...] = proj[:, o_rq + 3 * rw:o_rq + 4 * rw]


def _in_proj(x2, g_mix, w_in_ext, g_q, w_uq_pad, g_kv, w_k, w_v, cosm, sinm, cosr, sinr, seq, tm):
    T, D = x2.shape
    nS = seq // tm
    tok = lambda w: pl.BlockSpec((tm, w), lambda i: (i, 0))
    pos = lambda w: pl.BlockSpec((tm, w), lambda i: (i % nS, 0))
    outs = [(T, MLA_HEADS * LANES, BF16), (T, MLA_HEADS * LANES, BF16), (T, MLA_HEADS * LANES, BF16),
            (T, RET_HEADS * RET_DK, BF16), (T, RET_HEADS * RET_DK, BF16), (T, RET_HEADS * RET_DV, BF16),
            (T, RET_HEADS * RET_DV, F32)]
    return pl.pallas_call(
        _in_proj_kernel,
        grid=(T // tm,),
        in_specs=[tok(D), _const_spec(g_mix.shape), _const_spec(w_in_ext.shape), _const_spec(g_q.shape),
                  _const_spec(w_uq_pad.shape), _const_spec(g_kv.shape), _const_spec(w_k.shape),
                  _const_spec(w_v.shape), pos(LANES), pos(LANES), pos(LANES), pos(LANES)],
        out_specs=[tok(w) for (_, w, _) in outs],
        out_shape=[jax.ShapeDtypeStruct((t, w), d) for (t, w, d) in outs],
        compiler_params=_params(("parallel",)),
        name="in_proj",
    )(x2, g_mix, w_in_ext, g_q, w_uq_pad, g_kv, w_k, w_v, cosm, sinm, cosr, sinr)


ATTN_HEADS = 4


def _mla_attn_kernel(q_ref, k_ref, v_ref, o_ref, *, t):
    i = pl.program_id(2)
    causal = (lax.broadcasted_iota(jnp.int32, (t, t), 1) <= lax.broadcasted_iota(jnp.int32, (t, t), 0))

    def step(j, carry, masked):
        k0 = pl.multiple_of(j * t, t)
        new = []
        for hh in range(ATTN_HEADS):
            m, acc = carry[hh]
            hs = slice(hh * LANES, (hh + 1) * LANES)
            s = _dot_nt(q_ref[0, :, hs], k_ref[0, pl.ds(k0, t), hs])
            if masked:
                s = jnp.where(causal, s, NEG)
            m_new = jnp.maximum(m, jnp.max(s, axis=-1, keepdims=True))
            p = jnp.exp2(s - m_new).astype(BF16)
            acc = jnp.exp2(m - m_new) * acc + _dot(p, v_ref[0, pl.ds(k0, t), hs])
            new.append((m_new, acc))
        return tuple(new)

    init = tuple((jnp.full((t, 1), NEG, F32), jnp.zeros((t, LANES), F32)) for _ in range(ATTN_HEADS))
    carry = lax.fori_loop(0, i, functools.partial(step, masked=False), init)
    carry = step(i, carry, True)
    outs = [acc[:, :MLA_V] / acc[:, MLA_V:MLA_V + 1] for (_, acc) in carry]
    o_ref[0] = jnp.concatenate(outs, axis=-1).astype(o_ref.dtype)


def _mla_attn(q, k, v, t):
    B, S, _ = q.shape
    return pl.pallas_call(
        functools.partial(_mla_attn_kernel, t=t),
        grid=(B, MLA_HEADS // ATTN_HEADS, S // t),
        in_specs=[pl.BlockSpec((1, t, ATTN_HEADS * LANES), lambda b, h, i: (b, i, h)),
                  pl.BlockSpec((1, S, ATTN_HEADS * LANES), lambda b, h, i: (b, 0, h)),
                  pl.BlockSpec((1, S, ATTN_HEADS * LANES), lambda b, h, i: (b, 0, h))],
        out_specs=pl.BlockSpec((1, t, ATTN_HEADS * MLA_V), lambda b, h, i: (b, i, h)),
        out_shape=jax.ShapeDtypeStruct((B, S, MLA_HEADS * MLA_V), BF16),
        compiler_params=_params(("parallel", "parallel", "arbitrary")),
        name="mla_attn",
    )(q, k, v)


def _retention_kernel(rq_ref, rk_ref, rv_ref, rg_ref, gn_ref, decay_ref, zeta_ref, xi_ref, cd_ref,
                      o_ref, r_ref, *, n_chunks):
    C = RET_CHUNK

    @pl.when(pl.program_id(1) == 0)
    def _():
        r_ref[...] = jnp.zeros(r_ref.shape, F32)

    def chunk(n, carry):
        r0 = pl.multiple_of(n * C, C)
        for h in range(RET_HEADS):
            hs = slice(h * LANES, (h + 1) * LANES)
            qc = rq_ref[0, pl.ds(r0, C), hs]
            kc = rk_ref[0, pl.ds(r0, C), hs]
            vc = rv_ref[0, pl.ds(r0, C), hs]
            state = r_ref[h]
            sc = _dot_nt(qc, kc) * decay_ref[h]
            inner = _dot(sc.astype(BF16), vc)
            cross = _dot(qc, state.astype(BF16)) * xi_ref[h]
            o = inner + cross
            kz = (kc.astype(F32) * zeta_ref[h]).T.astype(BF16)
            r_ref[h] = state * cd_ref[h] + _dot(kz, vc)
            mu = jnp.mean(o, axis=-1, keepdims=True)
            d = o - mu
            var = jnp.mean(d * d, axis=-1, keepdims=True)
            on = d * lax.rsqrt(var + EPS) * gn_ref[:, hs]
            g = rg_ref[0, pl.ds(r0, C), hs]
            o_ref[0, pl.ds(r0, C), hs] = (g * jax.nn.sigmoid(g) * on).astype(o_ref.dtype)
        return carry

    lax.fori_loop(0, n_chunks, chunk, 0, unroll=True)


def _retention(rq, rk, rv, rg, g_gn, decay, zeta, xi, cd, ts):
    B, S, W = rq.shape
    tok = pl.BlockSpec((1, ts, W), lambda b, i: (b, i, 0))
    return pl.pallas_call(
        functools.partial(_retention_kernel, n_chunks=ts // RET_CHUNK),
        grid=(B, S // ts),
        in_specs=[tok, tok, tok, tok, _const_spec(g_gn.shape), _const_spec(decay.shape),
                  _const_spec(zeta.shape), _const_spec(xi.shape), _const_spec(cd.shape)],
        out_specs=tok,
        out_shape=jax.ShapeDtypeStruct((B, S, W), BF16),
        scratch_shapes=[pltpu.VMEM((RET_HEADS, RET_DK, RET_DV), F32)],
        compiler_params=_params(("parallel", "arbitrary")),
        name="retention",
    )(rq, rk, rv, rg, g_gn, decay, zeta, xi, cd)


def _mem_kv_kernel(mem_ref, g_ref, w_ref, k_ref, v_ref):
    D = mem_ref.shape[-1]
    mn = _rms(mem_ref[0], g_ref[...]).astype(BF16)
    kv = _dot(mn, w_ref[...])
    k_ref[0] = kv[:, :D].astype(BF16)
    v_ref[0] = kv[:, D:].astype(BF16)


def _mem_kv(mem, g_mem, w_xkv):
    B, M, D = mem.shape
    blk = pl.BlockSpec((1, M, D), lambda b: (b, 0, 0))
    return pl.pallas_call(
        _mem_kv_kernel,
        grid=(B,),
        in_specs=[blk, _const_spec(g_mem.shape), _const_spec(w_xkv.shape)],
        out_specs=[blk, blk],
        out_shape=[jax.ShapeDtypeStruct((B, M, D), BF16)] * 2,
        compiler_params=_params(("parallel",)),
        name="mem_kv",
    )(mem, g_mem, w_xkv)


def _post_mix_kernel(x_ref, omla_ref, oret_ref, woa_ref, wob_ref, gx_ref, wxq_ref, kx_ref, vx_ref, wxo_ref,
                     gf_ref, wpq_ref, keys_ref, h2_ref, a3t_ref, st_ref, ox_ref):
    D = x_ref.shape[-1]
    h1 = x_ref[...] + _dot(omla_ref[...], woa_ref[...]) + _dot(oret_ref[...], wob_ref[...])
    a2 = _rms(h1, gx_ref[...]).astype(BF16)
    qx = _dot(a2, wxq_ref[...])
    hd = D // X_HEADS
    for h in range(X_HEADS):
        sl = slice(h * hd, (h + 1) * hd)
        s = _dot_nt(qx[:, sl].astype(BF16), kx_ref[0, :, sl]) * (hd ** -0.5)
        m = jnp.max(s, axis=-1, keepdims=True)
        p = jnp.exp(s - m)
        p = p / jnp.sum(p, axis=-1, keepdims=True)
        ox_ref[:, sl] = _dot(p.astype(BF16), vx_ref[0, :, sl]).astype(BF16)
    h2 = h1 + _dot(ox_ref[...], wxo_ref[...])
    h2_ref[...] = h2
    a3 = _rms(h2, gf_ref[...])
    a3t_ref[...] = a3.T.astype(BF16)
    qp = _dot(a3.astype(BF16), wpq_ref[...]).astype(BF16)
    half = PEER_DKEY // 2
    for hp in range(2 * PEER_HEADS):
        sc = _dot_nt(keys_ref[hp % 2], qp[:, hp * half:(hp + 1) * half])
        for tl in range(sc.shape[1] // LANES):
            st_ref[hp, tl] = sc[:, tl * LANES:(tl + 1) * LANES]


def _post_mix(x2, o_mla, o_ret, w_oa, w_ob, g_x, w_xq, kx, vx, w_xo, g_f, w_pq, keys, seq, tm):
    T, D = x2.shape
    nS = seq // tm
    tok = lambda w: pl.BlockSpec((tm, w), lambda i: (i, 0))
    memb = pl.BlockSpec((1,) + kx.shape[1:], lambda i: (i // nS, 0, 0))
    return pl.pallas_call(
        _post_mix_kernel,
        grid=(T // tm,),
        in_specs=[tok(D), tok(o_mla.shape[1]), tok(o_ret.shape[1]), _const_spec(w_oa.shape),
                  _const_spec(w_ob.shape), _const_spec(g_x.shape), _const_spec(w_xq.shape), memb, memb,
                  _const_spec(w_xo.shape), _const_spec(g_f.shape), _const_spec(w_pq.shape),
                  _const_spec(keys.shape)],
        out_specs=[tok(D), pl.BlockSpec((D, tm), lambda i: (0, i)),
                   pl.BlockSpec((2 * PEER_HEADS, tm // LANES, PEER_KEYS, LANES), lambda i: (0, i, 0, 0))],
        out_shape=[jax.ShapeDtypeStruct((T, D), F32), jax.ShapeDtypeStruct((D, T), BF16),
                   jax.ShapeDtypeStruct((2 * PEER_HEADS, T // LANES, PEER_KEYS, LANES), F32)],
        scratch_shapes=[pltpu.VMEM((tm, D), BF16)],
        compiler_params=_params(("parallel",)),
        name="post_mix",
    )(x2, o_mla, o_ret, w_oa, w_ob, g_x, w_xq, kx, vx, w_xo, g_f, w_pq, keys)


N_TOP = PEER_TOPK + 1
TOP_ROWS = 24
SUBLANES = 8


def _sort_network(n):
    def merge(lo, hi, r):
        step = r * 2
        if step < hi - lo:
            yield from merge(lo, hi, step)
            yield from merge(lo + r, hi, step)
            yield from [(i, i + r) for i in range(lo + r, hi - r, step)]
        else:
            yield (lo, lo + r)

    def sort(lo, hi):
        if hi - lo >= 1:
            mid = lo + (hi - lo) // 2
            yield from sort(lo, mid)
            yield from sort(mid + 1, hi)
            yield from merge(lo, hi, 1)

    return list(sort(0, n - 1))


def _top17(x):
    k = PEER_TOPK
    v = [x[SUBLANES * i:SUBLANES * (i + 1), :] for i in range(x.shape[0] // SUBLANES)]
    assert len(v) == k
    for i, j in _sort_network(k):
        v[i], v[j] = jnp.maximum(v[i], v[j]), jnp.minimum(v[i], v[j])
    dropped = jnp.full(v[0].shape, NEG, F32)
    for shift in (SUBLANES // 2, SUBLANES // 4, SUBLANES // 8):
        other = [pltpu.roll(a, SUBLANES - shift, axis=0) for a in v]
        lose = [jnp.minimum(v[i], other[k - 1 - i]) for i in range(k)]
        v = [jnp.maximum(v[i], other[k - 1 - i]) for i in range(k)]
        for a in lose + [pltpu.roll(dropped, SUBLANES - shift, axis=0)]:
            dropped = jnp.maximum(dropped, a)
        d = k // 2
        while d >= 1:
            for i in range(k):
                if i & d == 0:
                    v[i], v[i + d] = jnp.maximum(v[i], v[i + d]), jnp.minimum(v[i], v[i + d])
            d //= 2
    return [a[0:1, :] for a in v], dropped[0:1, :]


def _peer_route_kernel(st_ref, stats_ref, vs_ref):
    tl = st_ref.shape[-1]
    vs_ref[...] = jnp.full(vs_ref.shape, NEG, F32)
    for h in range(PEER_HEADS):
        for p in range(2):
            best, nxt = _top17(st_ref[2 * h + p, 0])
            for it, row in enumerate(best + [nxt]):
                vs_ref[p, it:it + 1, :] = row
        v1 = lambda i: vs_ref[0, i:i + 1, :]
        v2 = lambda i: vs_ref[1, i:i + 1, :]
        cands = [v1(0) + vs_ref[1, 8 * r:8 * r + 8, :] for r in range(TOP_ROWS // 8)]
        cands += [v1(i) + vs_ref[1, 0:8, :] for i in range(1, 8)]
        cands += [vs_ref[0, 8 * r:8 * r + 8, :] + v2(0) for r in range(1, TOP_ROWS // 8)]
        cands += [jnp.full((SUBLANES, tl), NEG, F32)] * (PEER_KEYS // SUBLANES - len(cands))
        best, c17 = _top17(jnp.concatenate(cands, axis=0))
        top = best[0]
        z = jnp.zeros((1, tl), F32)
        for m in best:
            z = z + jnp.exp(m - top)
        stats_ref[0, h:h + 1, :] = 0.5 * (best[-1] + c17)
        stats_ref[1, h:h + 1, :] = v1(0)
        stats_ref[2, h:h + 1, :] = v2(0)
        stats_ref[3, h:h + 1, :] = 1.0 / z


def _peer_route(st):
    HP, nb, K, tl = st.shape
    T = nb * tl
    return pl.pallas_call(
        _peer_route_kernel,
        grid=(nb,),
        in_specs=[pl.BlockSpec((HP, 1, K, tl), lambda i: (0, i, 0, 0))],
        out_specs=pl.BlockSpec((4, PEER_HEADS, tl), lambda i: (0, 0, i)),
        out_shape=jax.ShapeDtypeStruct((4, PEER_HEADS, T), F32),
        scratch_shapes=[pltpu.VMEM((2, TOP_ROWS, tl), F32)],
        compiler_params=_params(("parallel",)),
        name="peer_route",
    )(st)


EXPERT_CHUNKS = 2


def _peer_dense_kernel(a3t_ref, st_ref, stats_ref, u_ref, v_ref, h2_ref, g_ref, o_ref,
                       thr_ref, c1_ref, e2_ref, acc_ref, *, te, tm):
    j = pl.program_id(1)
    nk = PEER_KEYS
    n_al = te // nk
    nt = tm // LANES

    @pl.when(j == 0)
    def _():
        acc_ref[...] = jnp.zeros(acc_ref.shape, F32)
        for h in range(PEER_HEADS):
            for tl in range(nt):
                ls = slice(tl * LANES, (tl + 1) * LANES)
                s1 = st_ref[2 * h, tl]
                thr_ref[h, :, tl, :] = stats_ref[0, h:h + 1, ls] - s1
                c1_ref[h, :, tl, :] = jnp.exp(s1 - stats_ref[1, h:h + 1, ls]) * (0.5 * stats_ref[3, h:h + 1, ls])
                e2_ref[h, tl] = jnp.exp(st_ref[2 * h + 1, tl] - stats_ref[2, h:h + 1, ls])

    ce = te // EXPERT_CHUNKS
    acts = [_dot(u_ref[c * ce:(c + 1) * ce, :], a3t_ref[...]) for c in range(EXPERT_CHUNKS)]
    for c in range(EXPERT_CHUNKS):
        p_rows = []
        for al in range(ce // nk):
            a = j * n_al + c * (ce // nk) + al
            rs = slice(al * nk, (al + 1) * nk)
            tiles = []
            for tl in range(nt):
                w = None
                for h in range(PEER_HEADS):
                    thr = thr_ref[h, a, tl:tl + 1, :]
                    c1 = c1_ref[h, a, tl:tl + 1, :]
                    term = jnp.where(st_ref[2 * h + 1, tl] >= thr, e2_ref[h, tl] * c1, 0.0)
                    w = term if w is None else w + term
                tiles.append((_gelu_x2(acts[c][rs, tl * LANES:(tl + 1) * LANES]) * w).astype(BF16))
            p_rows.append(jnp.concatenate(tiles, axis=1))
        p = jnp.concatenate(p_rows, axis=0)
        acc_ref[...] += lax.dot_general(v_ref[c * ce:(c + 1) * ce, :], p, (((0,), (0,)), ((), ())),
                                        preferred_element_type=F32)

    @pl.when(j == pl.num_programs(1) - 1)
    def _():
        o_ref[...] = _rms(h2_ref[...] + acc_ref[...].T, g_ref[...])


def _peer_dense(a3t, st, stats, u_bf, v_bf, h2, g, tm, te):
    D, T = a3t.shape
    nt = tm // LANES
    return pl.pallas_call(
        functools.partial(_peer_dense_kernel, te=te, tm=tm),
        grid=(T // tm, u_bf.shape[0] // te),
        in_specs=[pl.BlockSpec((D, tm), lambda i, j: (0, i)),
                  pl.BlockSpec((st.shape[0], nt) + st.shape[2:], lambda i, j: (0, i, 0, 0)),
                  pl.BlockSpec(stats.shape[:2] + (tm,), lambda i, j: (0, 0, i)),
                  pl.BlockSpec((te, D), lambda i, j: (j, 0)),
                  pl.BlockSpec((te, D), lambda i, j: (j, 0)),
                  pl.BlockSpec((tm, D), lambda i, j: (i, 0)), _const_spec(g.shape)],
        out_specs=pl.BlockSpec((tm, D), lambda i, j: (i, 0)),
        out_shape=jax.ShapeDtypeStruct((T, D), F32),
        scratch_shapes=[pltpu.VMEM((PEER_HEADS, PEER_KEYS, nt, LANES), F32)] * 2
        + [pltpu.VMEM((PEER_HEADS, nt, PEER_KEYS, LANES), F32), pltpu.VMEM((D, tm), F32)],
        compiler_params=_params(("parallel", "arbitrary")),
        name="peer_dense",
    )(a3t, st, stats, u_bf, v_bf, h2, g)


def _rope_tables(seq):
    pos = np.arange(seq, dtype=np.float32)

    def tab(dim):
        inv = (1.0 / (ROPE_BASE ** (np.arange(0, dim, 2, dtype=np.float32) / dim))).astype(np.float32)
        ang = (pos[:, None] * inv[None, :]).astype(np.float32)
        return np.cos(ang), np.sin(ang)

    cm, sm = tab(MLA_ROPE)
    pad = LANES - MLA_NOPE - MLA_ROPE
    one, zero = np.ones((seq, MLA_NOPE), np.float32), np.zeros((seq, MLA_NOPE), np.float32)
    cosm = np.concatenate([one, cm, cm, np.zeros((seq, pad), np.float32)], axis=1)
    sinm = np.concatenate([zero, -sm, sm, np.zeros((seq, pad), np.float32)], axis=1)
    cr, sr = tab(RET_DK)
    cosr = np.concatenate([cr, cr], axis=1)
    sinr = np.concatenate([-sr, sr], axis=1)
    return tuple(jnp.asarray(t, F32) for t in (cosm, sinm, cosr, sinr))


def _retention_tables():
    C = RET_CHUNK
    f32 = np.float32
    log_g = np.log(f32(1.0) - f32(2.0) ** (f32(-5.0) - np.arange(RET_HEADS, dtype=f32))).astype(f32)
    idx = np.arange(C, dtype=f32)
    rel = idx[:, None] - idx[None, :]
    decay = np.where(rel[None] >= 0, np.exp(np.maximum(rel, 0.0)[None] * log_g[:, None, None]), 0.0).astype(f32)
    zeta = np.exp((C - 1 - idx)[None, :] * log_g[:, None]).astype(f32)
    xi = np.exp((idx + 1)[None, :] * log_g[:, None]).astype(f32)
    cd = np.exp(C * log_g).astype(f32)
    bc = lambda t: np.broadcast_to(t[:, :, None], (RET_HEADS, C, LANES))
    tabs = (decay, bc(zeta), bc(xi), np.broadcast_to(cd[:, None, None], (RET_HEADS, C, LANES)))
    return tuple(jnp.asarray(t, F32) for t in tabs)


def _layer(h, mem, g_mix, w_in, g_q_lora, w_uq, g_kv_lora, w_ukv, g_ret_gn, w_o, g_xattn, g_mem,
           w_xq, w_xkv, w_xo, g_ffn, w_pq, sub_keys, u_experts, v_experts, g_out):
    B, S, D = h.shape
    T = B * S
    row = lambda g: g.reshape(1, -1)

    o_pe = MLA_Q_RANK + MLA_KV_RANK
    pe_pad = jnp.zeros((D, LANES), F32).at[:, MLA_NOPE:MLA_NOPE + MLA_ROPE].set(w_in[:, o_pe:o_pe + MLA_ROPE])
    w_in_ext = jnp.concatenate([w_in[:, :o_pe], pe_pad, w_in[:, o_pe + MLA_ROPE:]], axis=1).astype(BF16)
    dq = MLA_NOPE + MLA_ROPE
    w_uq_pad = jnp.pad(w_uq.reshape(MLA_Q_RANK, MLA_HEADS, dq), ((0, 0), (0, 0), (0, LANES - dq)))
    w_uq_pad = w_uq_pad.reshape(MLA_Q_RANK, MLA_HEADS * LANES).astype(BF16)
    w_ukv3 = w_ukv.reshape(MLA_KV_RANK, MLA_HEADS, MLA_NOPE + MLA_V)
    w_k = jnp.pad(w_ukv3[:, :, :MLA_NOPE], ((0, 0), (0, 0), (0, LANES - MLA_NOPE)))
    w_k = w_k.reshape(MLA_KV_RANK, MLA_HEADS * LANES).astype(BF16)
    w_v = jnp.pad(w_ukv3[:, :, MLA_NOPE:], ((0, 0), (0, 0), (0, LANES - MLA_V)))
    w_v = w_v.reshape(MLA_KV_RANK, MLA_HEADS * LANES).astype(BF16)

    cosm, sinm, cosr, sinr = _rope_tables(S)
    q, k, v, rq, rk, rv, rg = _in_proj(h.reshape(T, D), row(g_mix), w_in_ext, row(g_q_lora), w_uq_pad,
                                       row(g_kv_lora), w_k, w_v, cosm, sinm, cosr, sinr, S, min(1024, S))
    r3 = lambda t: t.reshape(B, S, t.shape[-1])
    o_mla = _mla_attn(r3(q), r3(k), r3(v), min(1024, S))
    o_ret = _retention(r3(rq), r3(rk), r3(rv), r3(rg), row(g_ret_gn), *_retention_tables(), min(1024, S))

    kx, vx = _mem_kv(mem, row(g_mem), w_xkv.astype(BF16))
    n_mla = MLA_HEADS * MLA_V
    w_o_bf = w_o.astype(BF16)
    h2, a3t, st = _post_mix(h.reshape(T, D), o_mla.reshape(T, -1), o_ret.reshape(T, -1), w_o_bf[:n_mla],
                            w_o_bf[n_mla:], row(g_xattn), w_xq.astype(BF16), kx, vx, w_xo.astype(BF16),
                            row(g_ffn), w_pq.astype(BF16), sub_keys.astype(BF16), S, min(512, S))
    stats = _peer_route(st)
    out = _peer_dense(a3t, st, stats, u_experts.astype(BF16), v_experts.astype(BF16), h2, row(g_out),
                      min(512, T), 2048)
    return out.reshape(B, S, D)


def kernel(x, mem, g_mix, w_in, g_q_lora, w_uq, g_kv_lora, w_ukv, g_ret_gn, w_o, g_xattn, g_mem, w_xq, w_xkv,
           w_xo, g_ffn, w_pq, sub_keys, u_experts, v_experts, g_final):
    depth = g_mix.shape[0]
    assert depth == 1, "the final norm is fused into the single layer's last kernel"
    l = 0
    return _layer(x, mem, g_mix[l], w_in[l], g_q_lora[l], w_uq[l], g_kv_lora[l], w_ukv[l], g_ret_gn[l], w_o[l],
                  g_xattn[l], g_mem[l], w_xq[l], w_xkv[l], w_xo[l], g_ffn[l], w_pq[l], sub_keys[l],
                  u_experts[l], v_experts[l], g_final)
```

```python
import functools
import math

import jax
import jax.numpy as jnp
import numpy as np
from jax import lax
from jax.experimental import pallas as pl
from jax.experimental.pallas import tpu as pltpu

F32 = jnp.float32
BF16 = jnp.bfloat16

LANES = 128
EPS = 1e-6
NEG = -1e30
ROPE_BASE = 10000.0
VMEM_LIMIT = 56 * 1024 * 1024

MLA_HEADS, MLA_Q_RANK, MLA_KV_RANK = 8, 384, 256
MLA_NOPE, MLA_ROPE, MLA_V = 64, 32, 64
RET_HEADS, RET_DK, RET_DV, RET_CHUNK = 4, 128, 128, 128
X_HEADS = 4
PEER_KEYS, PEER_HEADS, PEER_TOPK, PEER_DKEY = 128, 8, 16, 256


def _params(sem, vmem=VMEM_LIMIT):
    return pltpu.CompilerParams(dimension_semantics=sem, vmem_limit_bytes=vmem)


def _rms(x, g):
    return x * lax.rsqrt(jnp.mean(x * x, axis=-1, keepdims=True) + EPS) * g


def _dot(a, b):
    return jnp.dot(a, b, preferred_element_type=F32)


def _dot_nt(a, b):
    return lax.dot_general(a, b, (((1,), (1,)), ((), ())), preferred_element_type=F32)


def _gelu_x2(x):
    return x * (1.0 + lax.erf(x * (2.0 ** -0.5)))


def _const_spec(shape):
    nd = len(shape)
    return pl.BlockSpec(shape, lambda *_: (0,) * nd)


def _in_proj_kernel(x_ref, gmix_ref, win_ref, gq_ref, wuq_ref, gkv_ref, wk_ref, wv_ref,
                    cosm_ref, sinm_ref, cosr_ref, sinr_ref,
                    q_ref, k_ref, v_ref, rq_ref, rk_ref, rv_ref, rg_ref):
    x = x_ref[...]
    a = _rms(x, gmix_ref[...]).astype(BF16)
    proj = _dot(a, win_ref[...])
    o_cq, o_ckv, o_pe, o_rq = 0, MLA_Q_RANK, MLA_Q_RANK + MLA_KV_RANK, MLA_Q_RANK + MLA_KV_RANK + LANES
    rw = RET_HEADS * RET_DK
    cq = proj[:, o_cq:o_ckv]
    ckv = proj[:, o_ckv:o_pe]
    kpe = proj[:, o_pe:o_rq]
    cqn = _rms(cq, gq_ref[...]).astype(BF16)
    ckvn = _rms(ckv, gkv_ref[...]).astype(BF16)
    q = _dot(cqn, wuq_ref[...])
    kn = _dot(ckvn, wk_ref[...])
    vlane = lax.broadcasted_iota(jnp.int32, (x.shape[0], MLA_HEADS * LANES), 1) % LANES
    v_ref[...] = jnp.where(vlane == MLA_V, 1.0, _dot(ckvn, wv_ref[...])).astype(BF16)

    cosm, sinm = cosm_ref[...], sinm_ref[...]
    lane = lax.broadcasted_iota(jnp.int32, (x.shape[0], LANES), 1)
    first_half = lane < MLA_NOPE + MLA_ROPE // 2

    def rope_m(c):
        rot = jnp.where(first_half, pltpu.roll(c, LANES - MLA_ROPE // 2, axis=1),
                        pltpu.roll(c, MLA_ROPE // 2, axis=1))
        return c * cosm + rot * sinm

    qscale = (MLA_NOPE + MLA_ROPE) ** -0.5 * math.log2(math.e)
    kpe_r = rope_m(kpe)
    for h in range(MLA_HEADS):
        sl = slice(h * LANES, (h + 1) * LANES)
        q_ref[:, sl] = (rope_m(q[:, sl]) * qscale).astype(BF16)
        k_ref[:, sl] = (kn[:, sl] + kpe_r).astype(BF16)

    cosr, sinr = cosr_ref[...], sinr_ref[...]
    kscale = RET_DK ** -0.5
    for h in range(RET_HEADS):
        sl = slice(h * LANES, (h + 1) * LANES)
        c = proj[:, o_rq + h * LANES:o_rq + (h + 1) * LANES]
        rq_ref[:, sl] = (c * cosr + pltpu.roll(c, RET_DK // 2, axis=1) * sinr).astype(BF16)
        c = proj[:, o_rq + rw + h * LANES:o_rq + rw + (h + 1) * LANES]
        rk_ref[:, sl] = ((c * cosr + pltpu.roll(c, RET_DK // 2, axis=1) * sinr) * kscale).astype(BF16)
    rv_ref[...] = proj[:, o_rq + 2 * rw:o_rq + 3 * rw].astype(BF16)
    rg_ref[...] = proj[:, o_rq + 3 * rw:o_rq + 4 * rw]


def _in_proj(x2, g_mix, w_in_ext, g_q, w_uq_pad, g_kv, w_k, w_v, cosm, sinm, cosr, sinr, seq, tm):
    T, D = x2.shape
    nS = seq // tm
    tok = lambda w: pl.BlockSpec((tm, w), lambda i: (i, 0))
    pos = lambda w: pl.BlockSpec((tm, w), lambda i: (i % nS, 0))
    outs = [(T, MLA_HEADS * LANES, BF16), (T, MLA_HEADS * LANES, BF16), (T, MLA_HEADS * LANES, BF16),
            (T, RET_HEADS * RET_DK, BF16), (T, RET_HEADS * RET_DK, BF16), (T, RET_HEADS * RET_DV, BF16),
            (T, RET_HEADS * RET_DV, F32)]
    return pl.pallas_call(
        _in_proj_kernel,
        grid=(T // tm,),
        in_specs=[tok(D), _const_spec(g_mix.shape), _const_spec(w_in_ext.shape), _const_spec(g_q.shape),
                  _const_spec(w_uq_pad.shape), _const_spec(g_kv.shape), _const_spec(w_k.shape),
                  _const_spec(w_v.shape), pos(LANES), pos(LANES), pos(LANES), pos(LANES)],
        out_specs=[tok(w) for (_, w, _) in outs],
        out_shape=[jax.ShapeDtypeStruct((t, w), d) for (t, w, d) in outs],
        compiler_params=_params(("parallel",)),
        name="in_proj",
    )(x2, g_mix, w_in_ext, g_q, w_uq_pad, g_kv, w_k, w_v, cosm, sinm, cosr, sinr)


ATTN_HEADS = 4


def _mla_attn_kernel(q_ref, k_ref, v_ref, o_ref, *, t):
    i = pl.program_id(2)
    causal = (lax.broadcasted_iota(jnp.int32, (t, t), 1) <= lax.broadcasted_iota(jnp.int32, (t, t), 0))

    def step(j, carry, masked):
        k0 = pl.multiple_of(j * t, t)
        new = []
        for hh in range(ATTN_HEADS):
            m, acc = carry[hh]
            hs = slice(hh * LANES, (hh + 1) * LANES)
            s = _dot_nt(q_ref[0, :, hs], k_ref[0, pl.ds(k0, t), hs])
            if masked:
                s = jnp.where(causal, s, NEG)
            m_new = jnp.maximum(m, jnp.max(s, axis=-1, keepdims=True))
            p = jnp.exp2(s - m_new).astype(BF16)
            acc = jnp.exp2(m - m_new) * acc + _dot(p, v_ref[0, pl.ds(k0, t), hs])
            new.append((m_new, acc))
        return tuple(new)

    init = tuple((jnp.full((t, 1), NEG, F32), jnp.zeros((t, LANES), F32)) for _ in range(ATTN_HEADS))
    carry = lax.fori_loop(0, i, functools.partial(step, masked=False), init)
    carry = step(i, carry, True)
    outs = [acc[:, :MLA_V] / acc[:, MLA_V:MLA_V + 1] for (_, acc) in carry]
    o_ref[0] = jnp.concatenate(outs, axis=-1).astype(o_ref.dtype)


def _mla_attn(q, k, v, t):
    B, S, _ = q.shape
    return pl.pallas_call(
        functools.partial(_mla_attn_kernel, t=t),
        grid=(B, MLA_HEADS // ATTN_HEADS, S // t),
        in_specs=[pl.BlockSpec((1, t, ATTN_HEADS * LANES), lambda b, h, i: (b, i, h)),
                  pl.BlockSpec((1, S, ATTN_HEADS * LANES), lambda b, h, i: (b, 0, h)),
                  pl.BlockSpec((1, S, ATTN_HEADS * LANES), lambda b, h, i: (b, 0, h))],
        out_specs=pl.BlockSpec((1, t, ATTN_HEADS * MLA_V), lambda b, h, i: (b, i, h)),
        out_shape=jax.ShapeDtypeStruct((B, S, MLA_HEADS * MLA_V), BF16),
        compiler_params=_params(("parallel", "parallel", "arbitrary")),
        name="mla_attn",
    )(q, k, v)


def _retention_kernel(rq_ref, rk_ref, rv_ref, rg_ref, gn_ref, decay_ref, zeta_ref, xi_ref, cd_ref,
                      o_ref, r_ref, *, n_chunks):
    C = RET_CHUNK

    @pl.when(pl.program_id(1) == 0)
    def _():
        r_ref[...] = jnp.zeros(r_ref.shape, F32)

    def chunk(n, carry):
        r0 = pl.multiple_of(n * C, C)
        for h in range(RET_HEADS):
            hs = slice(h * LANES, (h + 1) * LANES)
            qc = rq_ref[0, pl.ds(r0, C), hs]
            kc = rk_ref[0, pl.ds(r0, C), hs]
            vc = rv_ref[0, pl.ds(r0, C), hs]
            state = r_ref[h]
            sc = _dot_nt(qc, kc) * decay_ref[h]
            inner = _dot(sc.astype(BF16), vc)
            cross = _dot(qc, state.astype(BF16)) * xi_ref[h]
            o = inner + cross
            kz = (kc.astype(F32) * zeta_ref[h]).T.astype(BF16)
            r_ref[h] = state * cd_ref[h] + _dot(kz, vc)
            mu = jnp.mean(o, axis=-1, keepdims=True)
            d = o - mu
            var = jnp.mean(d * d, axis=-1, keepdims=True)
            on = d * lax.rsqrt(var + EPS) * gn_ref[:, hs]
            g = rg_ref[0, pl.ds(r0, C), hs]
            o_ref[0, pl.ds(r0, C), hs] = (g * jax.nn.sigmoid(g) * on).astype(o_ref.dtype)
        return carry

    lax.fori_loop(0, n_chunks, chunk, 0, unroll=True)


def _retention(rq, rk, rv, rg, g_gn, decay, zeta, xi, cd, ts):
    B, S, W = rq.shape
    tok = pl.BlockSpec((1, ts, W), lambda b, i: (b, i, 0))
    return pl.pallas_call(
        functools.partial(_retention_kernel, n_chunks=ts // RET_CHUNK),
        grid=(B, S // ts),
        in_specs=[tok, tok, tok, tok, _const_spec(g_gn.shape), _const_spec(decay.shape),
                  _const_spec(zeta.shape), _const_spec(xi.shape), _const_spec(cd.shape)],
        out_specs=tok,
        out_shape=jax.ShapeDtypeStruct((B, S, W), BF16),
        scratch_shapes=[pltpu.VMEM((RET_HEADS, RET_DK, RET_DV), F32)],
        compiler_params=_params(("parallel", "arbitrary")),
        name="retention",
    )(rq, rk, rv, rg, g_gn, decay, zeta, xi, cd)


def _mem_kv_kernel(mem_ref, g_ref, w_ref, k_ref, v_ref):
    D = mem_ref.shape[-1]
    mn = _rms(mem_ref[0], g_ref[...]).astype(BF16)
    kv = _dot(mn, w_ref[...])
    k_ref[0] = kv[:, :D].astype(BF16)
    v_ref[0] = kv[:, D:].astype(BF16)


def _mem_kv(mem, g_mem, w_xkv):
    B, M, D = mem.shape
    blk = pl.BlockSpec((1, M, D), lambda b: (b, 0, 0))
    return pl.pallas_call(
        _mem_kv_kernel,
        grid=(B,),
        in_specs=[blk, _const_spec(g_mem.shape), _const_spec(w_xkv.shape)],
        out_specs=[blk, blk],
        out_shape=[jax.ShapeDtypeStruct((B, M, D), BF16)] * 2,
        compiler_params=_params(("parallel",)),
        name="mem_kv",
    )(mem, g_mem, w_xkv)


def _post_mix_kernel(x_ref, omla_ref, oret_ref, woa_ref, wob_ref, gx_ref, wxq_ref, kx_ref, vx_ref, wxo_ref,
                     gf_ref, wpq_ref, keys_ref, h2_ref, a3t_ref, st_ref, ox_ref):
    D = x_ref.shape[-1]
    h1 = x_ref[...] + _dot(omla_ref[...], woa_ref[...]) + _dot(oret_ref[...], wob_ref[...])
    a2 = _rms(h1, gx_ref[...]).astype(BF16)
    qx = _dot(a2, wxq_ref[...])
    hd = D // X_HEADS
    for h in range(X_HEADS):
        sl = slice(h * hd, (h + 1) * hd)
        s = _dot_nt(qx[:, sl].astype(BF16), kx_ref[0, :, sl]) * (hd ** -0.5)
        m = jnp.max(s, axis=-1, keepdims=True)
        p = jnp.exp(s - m)
        p = p / jnp.sum(p, axis=-1, keepdims=True)
        ox_ref[:, sl] = _dot(p.astype(BF16), vx_ref[0, :, sl]).astype(BF16)
    h2 = h1 + _dot(ox_ref[...], wxo_ref[...])
    h2_ref[...] = h2
    a3 = _rms(h2, gf_ref[...])
    a3t_ref[...] = a3.T.astype(BF16)
    qp = _dot(a3.astype(BF16), wpq_ref[...]).astype(BF16)
    half = PEER_DKEY // 2
    for hp in range(2 * PEER_HEADS):
        sc = _dot_nt(keys_ref[hp % 2], qp[:, hp * half:(hp + 1) * half])
        for tl in range(sc.shape[1] // LANES):
            st_ref[hp, tl] = sc[:, tl * LANES:(tl + 1) * LANES]


def _post_mix(x2, o_mla, o_ret, w_oa, w_ob, g_x, w_xq, kx, vx, w_xo, g_f, w_pq, keys, seq, tm):
    T, D = x2.shape
    nS = seq // tm
    tok = lambda w: pl.BlockSpec((tm, w), lambda i: (i, 0))
    memb = pl.BlockSpec((1,) + kx.shape[1:], lambda i: (i // nS, 0, 0))
    return pl.pallas_call(
        _post_mix_kernel,
        grid=(T // tm,),
        in_specs=[tok(D), tok(o_mla.shape[1]), tok(o_ret.shape[1]), _const_spec(w_oa.shape),
                  _const_spec(w_ob.shape), _const_spec(g_x.shape), _const_spec(w_xq.shape), memb, memb,
                  _const_spec(w_xo.shape), _const_spec(g_f.shape), _const_spec(w_pq.shape),
                  _const_spec(keys.shape)],
        out_specs=[tok(D), pl.BlockSpec((D, tm), lambda i: (0, i)),
                   pl.BlockSpec((2 * PEER_HEADS, tm // LANES, PEER_KEYS, LANES), lambda i: (0, i, 0, 0))],
        out_shape=[jax.ShapeDtypeStruct((T, D), F32), jax.ShapeDtypeStruct((D, T), BF16),
                   jax.ShapeDtypeStruct((2 * PEER_HEADS, T // LANES, PEER_KEYS, LANES), F32)],
        scratch_shapes=[pltpu.VMEM((tm, D), BF16)],
        compiler_params=_params(("parallel",)),
        name="post_mix",
    )(x2, o_mla, o_ret, w_oa, w_ob, g_x, w_xq, kx, vx, w_xo, g_f, w_pq, keys)


N_TOP = PEER_TOPK + 1
TOP_ROWS = 24
SUBLANES = 8


def _sort_network(n):
    def merge(lo, hi, r):
        step = r * 2
        if step < hi - lo:
            yield from merge(lo, hi, step)
            yield from merge(lo + r, hi, step)
            yield from [(i, i + r) for i in range(lo + r, hi - r, step)]
        else:
            yield (lo, lo + r)

    def sort(lo, hi):
        if hi - lo >= 1:
            mid = lo + (hi - lo) // 2
            yield from sort(lo, mid)
            yield from sort(mid + 1, hi)
            yield from merge(lo, hi, 1)

    return list(sort(0, n - 1))


def _top17(x):
    k = PEER_TOPK
    v = [x[SUBLANES * i:SUBLANES * (i + 1), :] for i in range(x.shape[0] // SUBLANES)]
    assert len(v) == k
    for i, j in _sort_network(k):
        v[i], v[j] = jnp.maximum(v[i], v[j]), jnp.minimum(v[i], v[j])
    dropped = jnp.full(v[0].shape, NEG, F32)
    for shift in (SUBLANES // 2, SUBLANES // 4, SUBLANES // 8):
        other = [pltpu.roll(a, SUBLANES - shift, axis=0) for a in v]
        lose = [jnp.minimum(v[i], other[k - 1 - i]) for i in range(k)]
        v = [jnp.maximum(v[i], other[k - 1 - i]) for i in range(k)]
        for a in lose + [pltpu.roll(dropped, SUBLANES - shift, axis=0)]:
            dropped = jnp.maximum(dropped, a)
        d = k // 2
        while d >= 1:
            for i in range(k):
                if i & d == 0:
                    v[i], v[i + d] = jnp.maximum(v[i], v[i + d]), jnp.minimum(v[i], v[i + d])
            d //= 2
    return [a[0:1, :] for a in v], dropped[0:1, :]


def _peer_route_kernel(st_ref, stats_ref, vs_ref):
    tl = st_ref.shape[-1]
    vs_ref[...] = jnp.full(vs_ref.shape, NEG, F32)
    for h in range(PEER_HEADS):
        for p in range(2):
            best, nxt = _top17(st_ref[2 * h + p, 0])
            for it, row in enumerate(best + [nxt]):
                vs_ref[p, it:it + 1, :] = row
        v1 = lambda i: vs_ref[0, i:i + 1, :]
        v2 = lambda i: vs_ref[1, i:i + 1, :]
        cands = [v1(0) + vs_ref[1, 8 * r:8 * r + 8, :] for r in range(TOP_ROWS // 8)]
        cands += [v1(i) + vs_ref[1, 0:8, :] for i in range(1, 8)]
        cands += [vs_ref[0, 8 * r:8 * r + 8, :] + v2(0) for r in range(1, TOP_ROWS // 8)]
        cands += [jnp.full((SUBLANES, tl), NEG, F32)] * (PEER_KEYS // SUBLANES - len(cands))
        best, c17 = _top17(jnp.concatenate(cands, axis=0))
        top = best[0]
        z = jnp.zeros((1, tl), F32)
        for m in best:
            z = z + jnp.exp(m - top)
        stats_ref[0, h:h + 1, :] = 0.5 * (best[-1] + c17)
        stats_ref[1, h:h + 1, :] = v1(0)
        stats_ref[2, h:h + 1, :] = v2(0)
        stats_ref[3, h:h + 1, :] = 1.0 / z


def _peer_route(st):
    HP, nb, K, tl = st.shape
    T = nb * tl
    return pl.pallas_call(
        _peer_route_kernel,
        grid=(nb,),
        in_specs=[pl.BlockSpec((HP, 1, K, tl), lambda i: (0, i, 0, 0))],
        out_specs=pl.BlockSpec((4, PEER_HEADS, tl), lambda i: (0, 0, i)),
        out_shape=jax.ShapeDtypeStruct((4, PEER_HEADS, T), F32),
        scratch_shapes=[pltpu.VMEM((2, TOP_ROWS, tl), F32)],
        compiler_params=_params(("parallel",)),
        name="peer_route",
    )(st)


EXPERT_CHUNKS = 2


def _peer_dense_kernel(a3t_ref, st_ref, stats_ref, u_ref, v_ref, h2_ref, g_ref, o_ref,
                       thr_ref, c1_ref, e2_ref, acc_ref, *, te, tm):
    j = pl.program_id(1)
    nk = PEER_KEYS
    n_al = te // nk
    nt = tm // LANES

    @pl.when(j == 0)
    def _():
        acc_ref[...] = jnp.zeros(acc_ref.shape, F32)
        for h in range(PEER_HEADS):
            for tl in range(nt):
                ls = slice(tl * LANES, (tl + 1) * LANES)
                s1 = st_ref[2 * h, tl]
                thr_ref[h, :, tl, :] = stats_ref[0, h:h + 1, ls] - s1
                c1_ref[h, :, tl, :] = jnp.exp(s1 - stats_ref[1, h:h + 1, ls]) * (0.5 * stats_ref[3, h:h + 1, ls])
                e2_ref[h, tl] = jnp.exp(st_ref[2 * h + 1, tl] - stats_ref[2, h:h + 1, ls])

    ce = te // EXPERT_CHUNKS
    acts = [_dot(u_ref[c * ce:(c + 1) * ce, :], a3t_ref[...]) for c in range(EXPERT_CHUNKS)]
    for c in range(EXPERT_CHUNKS):
        p_rows = []
        for al in range(ce // nk):
            a = j * n_al + c * (ce // nk) + al
            rs = slice(al * nk, (al + 1) * nk)
            tiles = []
            for tl in range(nt):
                w = None
                for h in range(PEER_HEADS):
                    thr = thr_ref[h, a, tl:tl + 1, :]
                    c1 = c1_ref[h, a, tl:tl + 1, :]
                    term = jnp.where(st_ref[2 * h + 1, tl] >= thr, e2_ref[h, tl] * c1, 0.0)
                    w = term if w is None else w + term
                tiles.append((_gelu_x2(acts[c][rs, tl * LANES:(tl + 1) * LANES]) * w).astype(BF16))
            p_rows.append(jnp.concatenate(tiles, axis=1))
        p = jnp.concatenate(p_rows, axis=0)
        acc_ref[...] += lax.dot_general(v_ref[c * ce:(c + 1) * ce, :], p, (((0,), (0,)), ((), ())),
                                        preferred_element_type=F32)

    @pl.when(j == pl.num_programs(1) - 1)
    def _():
        o_ref[...] = _rms(h2_ref[...] + acc_ref[...].T, g_ref[...])


def _peer_dense(a3t, st, stats, u_bf, v_bf, h2, g, tm, te):
    D, T = a3t.shape
    nt = tm // LANES
    return pl.pallas_call(
        functools.partial(_peer_dense_kernel, te=te, tm=tm),
        grid=(T // tm, u_bf.shape[0] // te),
        in_specs=[pl.BlockSpec((D, tm), lambda i, j: (0, i)),
                  pl.BlockSpec((st.shape[0], nt) + st.shape[2:], lambda i, j: (0, i, 0, 0)),
                  pl.BlockSpec(stats.shape[:2] + (tm,), lambda i, j: (0, 0, i)),
                  pl.BlockSpec((te, D), lambda i, j: (j, 0)),
                  pl.BlockSpec((te, D), lambda i, j: (j, 0)),
                  pl.BlockSpec((tm, D), lambda i, j: (i, 0)), _const_spec(g.shape)],
        out_specs=pl.BlockSpec((tm, D), lambda i, j: (i, 0)),
        out_shape=jax.ShapeDtypeStruct((T, D), F32),
        scratch_shapes=[pltpu.VMEM((PEER_HEADS, PEER_KEYS, nt, LANES), F32)] * 2
        + [pltpu.VMEM((PEER_HEADS, nt, PEER_KEYS, LANES), F32), pltpu.VMEM((D, tm), F32)],
        compiler_params=_params(("parallel", "arbitrary")),
        name="peer_dense",
    )(a3t, st, stats, u_bf, v_bf, h2, g)


def _rope_tables(seq):
    pos = np.arange(seq, dtype=np.float64)

    def tab(dim):
        inv = 1.0 / (ROPE_BASE ** (np.arange(0, dim, 2, dtype=np.float64) / dim))
        ang = pos[:, None] * inv[None, :]
        return np.cos(ang), np.sin(ang)

    cm, sm = tab(MLA_ROPE)
    pad = LANES - MLA_NOPE - MLA_ROPE
    one, zero = np.ones((seq, MLA_NOPE)), np.zeros((seq, MLA_NOPE))
    cosm = np.concatenate([one, cm, cm, np.zeros((seq, pad))], axis=1)
    sinm = np.concatenate([zero, -sm, sm, np.zeros((seq, pad))], axis=1)
    cr, sr = tab(RET_DK)
    cosr = np.concatenate([cr, cr], axis=1)
    sinr = np.concatenate([-sr, sr], axis=1)
    return tuple(jnp.asarray(t, F32) for t in (cosm, sinm, cosr, sinr))


def _retention_tables():
    C = RET_CHUNK
    log_g = np.log(1.0 - 2.0 ** (-5.0 - np.arange(RET_HEADS, dtype=np.float64)))
    idx = np.arange(C, dtype=np.float64)
    rel = idx[:, None] - idx[None, :]
    decay = np.where(rel[None] >= 0, np.exp(np.maximum(rel, 0.0)[None] * log_g[:, None, None]), 0.0)
    zeta = np.exp((C - 1 - idx)[None, :] * log_g[:, None])
    xi = np.exp((idx + 1)[None, :] * log_g[:, None])
    cd = np.exp(C * log_g)
    bc = lambda t: np.broadcast_to(t[:, :, None], (RET_HEADS, C, LANES))
    tabs = (decay, bc(zeta), bc(xi), np.broadcast_to(cd[:, None, None], (RET_HEADS, C, LANES)))
    return tuple(jnp.asarray(t, F32) for t in tabs)


def _layer(h, mem, g_mix, w_in, g_q_lora, w_uq, g_kv_lora, w_ukv, g_ret_gn, w_o, g_xattn, g_mem,
           w_xq, w_xkv, w_xo, g_ffn, w_pq, sub_keys, u_experts, v_experts, g_out):
    B, S, D = h.shape
    T = B * S
    row = lambda g: g.reshape(1, -1)

    o_pe = MLA_Q_RANK + MLA_KV_RANK
    pe_pad = jnp.zeros((D, LANES), F32).at[:, MLA_NOPE:MLA_NOPE + MLA_ROPE].set(w_in[:, o_pe:o_pe + MLA_ROPE])
    w_in_ext = jnp.concatenate([w_in[:, :o_pe], pe_pad, w_in[:, o_pe + MLA_ROPE:]], axis=1).astype(BF16)
    dq = MLA_NOPE + MLA_ROPE
    w_uq_pad = jnp.pad(w_uq.reshape(MLA_Q_RANK, MLA_HEADS, dq), ((0, 0), (0, 0), (0, LANES - dq)))
    w_uq_pad = w_uq_pad.reshape(MLA_Q_RANK, MLA_HEADS * LANES).astype(BF16)
    w_ukv3 = w_ukv.reshape(MLA_KV_RANK, MLA_HEADS, MLA_NOPE + MLA_V)
    w_k = jnp.pad(w_ukv3[:, :, :MLA_NOPE], ((0, 0), (0, 0), (0, LANES - MLA_NOPE)))
    w_k = w_k.reshape(MLA_KV_RANK, MLA_HEADS * LANES).astype(BF16)
    w_v = jnp.pad(w_ukv3[:, :, MLA_NOPE:], ((0, 0), (0, 0), (0, LANES - MLA_V)))
    w_v = w_v.reshape(MLA_KV_RANK, MLA_HEADS * LANES).astype(BF16)

    cosm, sinm, cosr, sinr = _rope_tables(S)
    q, k, v, rq, rk, rv, rg = _in_proj(h.reshape(T, D), row(g_mix), w_in_ext, row(g_q_lora), w_uq_pad,
                                       row(g_kv_lora), w_k, w_v, cosm, sinm, cosr, sinr, S, min(1024, S))
    r3 = lambda t: t.reshape(B, S, t.shape[-1])
    o_mla = _mla_attn(r3(q), r3(k), r3(v), min(1024, S))
    o_ret = _retention(r3(rq), r3(rk), r3(rv), r3(rg), row(g_ret_gn), *_retention_tables(), min(1024, S))

    kx, vx = _mem_kv(mem, row(g_mem), w_xkv.astype(BF16))
    n_mla = MLA_HEADS * MLA_V
    w_o_bf = w_o.astype(BF16)
    h2, a3t, st = _post_mix(h.reshape(T, D), o_mla.reshape(T, -1), o_ret.reshape(T, -1), w_o_bf[:n_mla],
                            w_o_bf[n_mla:], row(g_xattn), w_xq.astype(BF16), kx, vx, w_xo.astype(BF16),
                            row(g_ffn), w_pq.astype(BF16), sub_keys.astype(BF16), S, min(512, S))
    stats = _peer_route(st)
    out = _peer_dense(a3t, st, stats, u_experts.astype(BF16), v_experts.astype(BF16), h2, row(g_out),
                      min(512, T), 2048)
    return out.reshape(B, S, D)


def kernel(x, mem, g_mix, w_in, g_q_lora, w_uq, g_kv_lora, w_ukv, g_ret_gn, w_o, g_xattn, g_mem, w_xq, w_xkv,
           w_xo, g_ffn, w_pq, sub_keys, u_experts, v_experts, g_final):
    depth = g_mix.shape[0]
    assert depth == 1, "the final norm is fused into the single layer's last kernel"
    l = 0
    return _layer(x, mem, g_mix[l], w_in[l], g_q_lora[l], w_uq[l], g_kv_lora[l], w_ukv[l], g_ret_gn[l], w_o[l],
                  g_xattn[l], g_mem[l], w_xq[l], w_xkv[l], w_xo[l], g_ffn[l], w_pq[l], sub_keys[l],
                  u_experts[l], v_experts[l], g_final)
```

```python
import functools
import math

import jax
import jax.numpy as jnp
import numpy as np
from jax import lax
from jax.experimental import pallas as pl
from jax.experimental.pallas import tpu as pltpu

F32 = jnp.float32
BF16 = jnp.bfloat16

LANES = 128
EPS = 1e-6
NEG = -1e30
ROPE_BASE = 10000.0
VMEM_LIMIT = 56 * 1024 * 1024

MLA_HEADS, MLA_Q_RANK, MLA_KV_RANK = 8, 384, 256
MLA_NOPE, MLA_ROPE, MLA_V = 64, 32, 64
RET_HEADS, RET_DK, RET_DV, RET_CHUNK = 4, 128, 128, 128
X_HEADS = 4
PEER_KEYS, PEER_HEADS, PEER_TOPK, PEER_DKEY = 128, 8, 16, 256


def _params(sem, vmem=VMEM_LIMIT):
    return pltpu.CompilerParams(dimension_semantics=sem, vmem_limit_bytes=vmem)


def _rms(x, g):
    return x * lax.rsqrt(jnp.mean(x * x, axis=-1, keepdims=True) + EPS) * g


def _dot(a, b):
    return jnp.dot(a, b, preferred_element_type=F32)


def _dot_nt(a, b):
    return lax.dot_general(a, b, (((1,), (1,)), ((), ())), preferred_element_type=F32)


def _gelu_x2(x):
    return x * (1.0 + lax.erf(x * (2.0 ** -0.5)))


def _const_spec(shape):
    nd = len(shape)
    return pl.BlockSpec(shape, lambda *_: (0,) * nd)


def _in_proj_kernel(x_ref, gmix_ref, win_ref, gq_ref, wuq_ref, gkv_ref, wk_ref, wv_ref,
                    cosm_ref, sinm_ref, cosr_ref, sinr_ref,
                    q_ref, k_ref, v_ref, rq_ref, rk_ref, rv_ref, rg_ref):
    x = x_ref[...]
    a = _rms(x, gmix_ref[...]).astype(BF16)
    proj = _dot(a, win_ref[...])
    o_cq, o_ckv, o_pe, o_rq = 0, MLA_Q_RANK, MLA_Q_RANK + MLA_KV_RANK, MLA_Q_RANK + MLA_KV_RANK + LANES
    rw = RET_HEADS * RET_DK
    cq = proj[:, o_cq:o_ckv]
    ckv = proj[:, o_ckv:o_pe]
    kpe = proj[:, o_pe:o_rq]
    cqn = _rms(cq, gq_ref[...]).astype(BF16)
    ckvn = _rms(ckv, gkv_ref[...]).astype(BF16)
    q = _dot(cqn, wuq_ref[...])
    kn = _dot(ckvn, wk_ref[...])
    vlane = lax.broadcasted_iota(jnp.int32, (x.shape[0], MLA_HEADS * LANES), 1) % LANES
    v_ref[...] = jnp.where(vlane == MLA_V, 1.0, _dot(ckvn, wv_ref[...])).astype(BF16)

    cosm, sinm = cosm_ref[...], sinm_ref[...]
    lane = lax.broadcasted_iota(jnp.int32, (x.shape[0], LANES), 1)
    first_half = lane < MLA_NOPE + MLA_ROPE // 2

    def rope_m(c):
        rot = jnp.where(first_half, pltpu.roll(c, LANES - MLA_ROPE // 2, axis=1),
                        pltpu.roll(c, MLA_ROPE // 2, axis=1))
        return c * cosm + rot * sinm

    qscale = (MLA_NOPE + MLA_ROPE) ** -0.5 * math.log2(math.e)
    kpe_r = rope_m(kpe)
    for h in range(MLA_HEADS):
        sl = slice(h * LANES, (h + 1) * LANES)
        q_ref[:, sl] = (rope_m(q[:, sl]) * qscale).astype(BF16)
        k_ref[:, sl] = (kn[:, sl] + kpe_r).astype(BF16)

    cosr, sinr = cosr_ref[...], sinr_ref[...]
    kscale = RET_DK ** -0.5
    for h in range(RET_HEADS):
        sl = slice(h * LANES, (h + 1) * LANES)
        c = proj[:, o_rq + h * LANES:o_rq + (h + 1) * LANES]
        rq_ref[:, sl] = (c * cosr + pltpu.roll(c, RET_DK // 2, axis=1) * sinr).astype(BF16)
        c = proj[:, o_rq + rw + h * LANES:o_rq + rw + (h + 1) * LANES]
        rk_ref[:, sl] = ((c * cosr + pltpu.roll(c, RET_DK // 2, axis=1) * sinr) * kscale).astype(BF16)
    rv_ref[...] = proj[:, o_rq + 2 * rw:o_rq + 3 * rw].astype(BF16)
    rg_ref[...] = proj[:, o_rq + 3 * rw:o_rq + 4 * rw]


def _in_proj(x2, g_mix, w_in_ext, g_q, w_uq_pad, g_kv, w_k, w_v, cosm, sinm, cosr, sinr, seq, tm):
    T, D = x2.shape
    nS = seq // tm
    tok = lambda w: pl.BlockSpec((tm, w), lambda i: (i, 0))
    pos = lambda w: pl.BlockSpec((tm, w), lambda i: (i % nS, 0))
    outs = [(T, MLA_HEADS * LANES, BF16), (T, MLA_HEADS * LANES, BF16), (T, MLA_HEADS * LANES, BF16),
            (T, RET_HEADS * RET_DK, BF16), (T, RET_HEADS * RET_DK, BF16), (T, RET_HEADS * RET_DV, BF16),
            (T, RET_HEADS * RET_DV, F32)]
    return pl.pallas_call(
        _in_proj_kernel,
        grid=(T // tm,),
        in_specs=[tok(D), _const_spec(g_mix.shape), _const_spec(w_in_ext.shape), _const_spec(g_q.shape),
                  _const_spec(w_uq_pad.shape), _const_spec(g_kv.shape), _const_spec(w_k.shape),
                  _const_spec(w_v.shape), pos(LANES), pos(LANES), pos(LANES), pos(LANES)],
        out_specs=[tok(w) for (_, w, _) in outs],
        out_shape=[jax.ShapeDtypeStruct((t, w), d) for (t, w, d) in outs],
        compiler_params=_params(("parallel",)),
        name="in_proj",
    )(x2, g_mix, w_in_ext, g_q, w_uq_pad, g_kv, w_k, w_v, cosm, sinm, cosr, sinr)


ATTN_HEADS = 4


def _mla_attn_kernel(q_ref, k_ref, v_ref, o_ref, *, t):
    i = pl.program_id(2)
    causal = (lax.broadcasted_iota(jnp.int32, (t, t), 1) <= lax.broadcasted_iota(jnp.int32, (t, t), 0))

    def step(j, carry, masked):
        k0 = pl.multiple_of(j * t, t)
        new = []
        for hh in range(ATTN_HEADS):
            m, acc = carry[hh]
            hs = slice(hh * LANES, (hh + 1) * LANES)
            s = _dot_nt(q_ref[0, :, hs], k_ref[0, pl.ds(k0, t), hs])
            if masked:
                s = jnp.where(causal, s, NEG)
            m_new = jnp.maximum(m, jnp.max(s, axis=-1, keepdims=True))
            p = jnp.exp2(s - m_new).astype(BF16)
            acc = jnp.exp2(m - m_new) * acc + _dot(p, v_ref[0, pl.ds(k0, t), hs])
            new.append((m_new, acc))
        return tuple(new)

    init = tuple((jnp.full((t, 1), NEG, F32), jnp.zeros((t, LANES), F32)) for _ in range(ATTN_HEADS))
    carry = lax.fori_loop(0, i, functools.partial(step, masked=False), init)
    carry = step(i, carry, True)
    outs = [acc[:, :MLA_V] / acc[:, MLA_V:MLA_V + 1] for (_, acc) in carry]
    o_ref[0] = jnp.concatenate(outs, axis=-1).astype(o_ref.dtype)


def _mla_attn(q, k, v, t):
    B, S, _ = q.shape
    return pl.pallas_call(
        functools.partial(_mla_attn_kernel, t=t),
        grid=(B, MLA_HEADS // ATTN_HEADS, S // t),
        in_specs=[pl.BlockSpec((1, t, ATTN_HEADS * LANES), lambda b, h, i: (b, i, h)),
                  pl.BlockSpec((1, S, ATTN_HEADS * LANES), lambda b, h, i: (b, 0, h)),
                  pl.BlockSpec((1, S, ATTN_HEADS * LANES), lambda b, h, i: (b, 0, h))],
        out_specs=pl.BlockSpec((1, t, ATTN_HEADS * MLA_V), lambda b, h, i: (b, i, h)),
        out_shape=jax.ShapeDtypeStruct((B, S, MLA_HEADS * MLA_V), BF16),
        compiler_params=_params(("parallel", "parallel", "arbitrary")),
        name="mla_attn",
    )(q, k, v)


def _retention_kernel(rq_ref, rk_ref, rv_ref, rg_ref, gn_ref, decay_ref, zeta_ref, xi_ref, cd_ref,
                      o_ref, r_ref, *, n_chunks):
    C = RET_CHUNK

    @pl.when(pl.program_id(1) == 0)
    def _():
        r_ref[...] = jnp.zeros(r_ref.shape, F32)

    def chunk(n, carry):
        r0 = pl.multiple_of(n * C, C)
        for h in range(RET_HEADS):
            hs = slice(h * LANES, (h + 1) * LANES)
            qc = rq_ref[0, pl.ds(r0, C), hs]
            kc = rk_ref[0, pl.ds(r0, C), hs]
            vc = rv_ref[0, pl.ds(r0, C), hs]
            state = r_ref[h]
            sc = _dot_nt(qc, kc) * decay_ref[h]
            inner = _dot(sc.astype(BF16), vc)
            cross = _dot(qc, state.astype(BF16)) * xi_ref[h]
            o = inner + cross
            kz = (kc.astype(F32) * zeta_ref[h]).T.astype(BF16)
            r_ref[h] = state * cd_ref[h] + _dot(kz, vc)
            mu = jnp.mean(o, axis=-1, keepdims=True)
            d = o - mu
            var = jnp.mean(d * d, axis=-1, keepdims=True)
            on = d * lax.rsqrt(var + EPS) * gn_ref[:, hs]
            g = rg_ref[0, pl.ds(r0, C), hs]
            o_ref[0, pl.ds(r0, C), hs] = (g * jax.nn.sigmoid(g) * on).astype(o_ref.dtype)
        return carry

    lax.fori_loop(0, n_chunks, chunk, 0, unroll=True)


def _retention(rq, rk, rv, rg, g_gn, decay, zeta, xi, cd, ts):
    B, S, W = rq.shape
    tok = pl.BlockSpec((1, ts, W), lambda b, i: (b, i, 0))
    return pl.pallas_call(
        functools.partial(_retention_kernel, n_chunks=ts // RET_CHUNK),
        grid=(B, S // ts),
        in_specs=[tok, tok, tok, tok, _const_spec(g_gn.shape), _const_spec(decay.shape),
                  _const_spec(zeta.shape), _const_spec(xi.shape), _const_spec(cd.shape)],
        out_specs=tok,
        out_shape=jax.ShapeDtypeStruct((B, S, W), BF16),
        scratch_shapes=[pltpu.VMEM((RET_HEADS, RET_DK, RET_DV), F32)],
        compiler_params=_params(("parallel", "arbitrary")),
        name="retention",
    )(rq, rk, rv, rg, g_gn, decay, zeta, xi, cd)


def _post_mix_kernel(x_ref, omla_ref, oret_ref, woa_ref, wob_ref, gx_ref, wxq_ref, mem_ref, gm_ref, wxkv_ref,
                     wxo_ref, gf_ref, wpq_ref, keys_ref, h2_ref, a3t_ref, st_ref, ox_ref, kx_ref, vx_ref, *, n_seq):
    D = x_ref.shape[-1]

    @pl.when(pl.program_id(0) % n_seq == 0)
    def _():
        mn = _rms(mem_ref[0], gm_ref[...]).astype(BF16)
        kv = _dot(mn, wxkv_ref[...])
        kx_ref[0] = kv[:, :D].astype(BF16)
        vx_ref[0] = kv[:, D:].astype(BF16)

    h1 = x_ref[...] + _dot(omla_ref[...], woa_ref[...]) + _dot(oret_ref[...], wob_ref[...])
    a2 = _rms(h1, gx_ref[...]).astype(BF16)
    qx = _dot(a2, wxq_ref[...])
    hd = D // X_HEADS
    for h in range(X_HEADS):
        sl = slice(h * hd, (h + 1) * hd)
        s = _dot_nt(qx[:, sl].astype(BF16), kx_ref[0, :, sl]) * (hd ** -0.5)
        m = jnp.max(s, axis=-1, keepdims=True)
        p = jnp.exp(s - m)
        p = p / jnp.sum(p, axis=-1, keepdims=True)
        ox_ref[:, sl] = _dot(p.astype(BF16), vx_ref[0, :, sl]).astype(BF16)
    h2 = h1 + _dot(ox_ref[...], wxo_ref[...])
    h2_ref[...] = h2
    a3 = _rms(h2, gf_ref[...])
    a3t_ref[...] = a3.T.astype(BF16)
    qp = _dot(a3.astype(BF16), wpq_ref[...]).astype(BF16)
    half = PEER_DKEY // 2
    for hp in range(2 * PEER_HEADS):
        sc = _dot_nt(keys_ref[hp % 2], qp[:, hp * half:(hp + 1) * half])
        for tl in range(sc.shape[1] // LANES):
            st_ref[hp, tl] = sc[:, tl * LANES:(tl + 1) * LANES]


def _post_mix(x2, o_mla, o_ret, w_oa, w_ob, g_x, w_xq, mem, g_m, w_xkv, w_xo, g_f, w_pq, keys, seq, tm):
    T, D = x2.shape
    nS = seq // tm
    tok = lambda w: pl.BlockSpec((tm, w), lambda i: (i, 0))
    memb = pl.BlockSpec((1,) + mem.shape[1:], lambda i: (i // nS, 0, 0))
    return pl.pallas_call(
        functools.partial(_post_mix_kernel, n_seq=nS),
        grid=(T // tm,),
        in_specs=[tok(D), tok(o_mla.shape[1]), tok(o_ret.shape[1]), _const_spec(w_oa.shape),
                  _const_spec(w_ob.shape), _const_spec(g_x.shape), _const_spec(w_xq.shape), memb,
                  _const_spec(g_m.shape), _const_spec(w_xkv.shape),
                  _const_spec(w_xo.shape), _const_spec(g_f.shape), _const_spec(w_pq.shape),
                  _const_spec(keys.shape)],
        out_specs=[tok(D), pl.BlockSpec((D, tm), lambda i: (0, i)),
                   pl.BlockSpec((2 * PEER_HEADS, tm // LANES, PEER_KEYS, LANES), lambda i: (0, i, 0, 0))],
        out_shape=[jax.ShapeDtypeStruct((T, D), F32), jax.ShapeDtypeStruct((D, T), BF16),
                   jax.ShapeDtypeStruct((2 * PEER_HEADS, T // LANES, PEER_KEYS, LANES), F32)],
        scratch_shapes=[pltpu.VMEM((tm, D), BF16)] + [pltpu.VMEM((1,) + mem.shape[1:], BF16)] * 2,
        compiler_params=_params(("arbitrary",)),
        name="post_mix",
    )(x2, o_mla, o_ret, w_oa, w_ob, g_x, w_xq, mem, g_m, w_xkv, w_xo, g_f, w_pq, keys)


N_TOP = PEER_TOPK + 1
TOP_ROWS = 24
SUBLANES = 8


def _sort_network(n):
    def merge(lo, hi, r):
        step = r * 2
        if step < hi - lo:
            yield from merge(lo, hi, step)
            yield from merge(lo + r, hi, step)
            yield from [(i, i + r) for i in range(lo + r, hi - r, step)]
        else:
            yield (lo, lo + r)

    def sort(lo, hi):
        if hi - lo >= 1:
            mid = lo + (hi - lo) // 2
            yield from sort(lo, mid)
            yield from sort(mid + 1, hi)
            yield from merge(lo, hi, 1)

    return list(sort(0, n - 1))


def _top17(x):
    k = PEER_TOPK
    v = [x[SUBLANES * i:SUBLANES * (i + 1), :] for i in range(x.shape[0] // SUBLANES)]
    assert len(v) == k
    for i, j in _sort_network(k):
        v[i], v[j] = jnp.maximum(v[i], v[j]), jnp.minimum(v[i], v[j])
    dropped = jnp.full(v[0].shape, NEG, F32)
    for shift in (SUBLANES // 2, SUBLANES // 4, SUBLANES // 8):
        other = [pltpu.roll(a, SUBLANES - shift, axis=0) for a in v]
        lose = [jnp.minimum(v[i], other[k - 1 - i]) for i in range(k)]
        v = [jnp.maximum(v[i], other[k - 1 - i]) for i in range(k)]
        for a in lose + [pltpu.roll(dropped, SUBLANES - shift, axis=0)]:
            dropped = jnp.maximum(dropped, a)
        d = k // 2
        while d >= 1:
            for i in range(k):
                if i & d == 0:
                    v[i], v[i + d] = jnp.maximum(v[i], v[i + d]), jnp.minimum(v[i], v[i + d])
            d //= 2
    return [a[0:1, :] for a in v], dropped[0:1, :]


def _peer_route_kernel(st_ref, stats_ref, vs_ref):
    tl = st_ref.shape[-1]
    vs_ref[...] = jnp.full(vs_ref.shape, NEG, F32)
    for h in range(PEER_HEADS):
        for p in range(2):
            best, nxt = _top17(st_ref[2 * h + p, 0])
            for it, row in enumerate(best + [nxt]):
                vs_ref[p, it:it + 1, :] = row
        v1 = lambda i: vs_ref[0, i:i + 1, :]
        v2 = lambda i: vs_ref[1, i:i + 1, :]
        cands = [v1(0) + vs_ref[1, 8 * r:8 * r + 8, :] for r in range(TOP_ROWS // 8)]
        cands += [v1(i) + vs_ref[1, 0:8, :] for i in range(1, 8)]
        cands += [vs_ref[0, 8 * r:8 * r + 8, :] + v2(0) for r in range(1, TOP_ROWS // 8)]
        cands += [jnp.full((SUBLANES, tl), NEG, F32)] * (PEER_KEYS // SUBLANES - len(cands))
        best, c17 = _top17(jnp.concatenate(cands, axis=0))
        top = best[0]
        z = jnp.zeros((1, tl), F32)
        for m in best:
            z = z + jnp.exp(m - top)
        stats_ref[0, h:h + 1, :] = 0.5 * (best[-1] + c17)
        stats_ref[1, h:h + 1, :] = v1(0)
        stats_ref[2, h:h + 1, :] = v2(0)
        stats_ref[3, h:h + 1, :] = 1.0 / z


def _peer_route(st):
    HP, nb, K, tl = st.shape
    T = nb * tl
    return pl.pallas_call(
        _peer_route_kernel,
        grid=(nb,),
        in_specs=[pl.BlockSpec((HP, 1, K, tl), lambda i: (0, i, 0, 0))],
        out_specs=pl.BlockSpec((4, PEER_HEADS, tl), lambda i: (0, 0, i)),
        out_shape=jax.ShapeDtypeStruct((4, PEER_HEADS, T), F32),
        scratch_shapes=[pltpu.VMEM((2, TOP_ROWS, tl), F32)],
        compiler_params=_params(("parallel",)),
        name="peer_route",
    )(st)


EXPERT_CHUNKS = 2


def _peer_dense_kernel(a3t_ref, st_ref, stats_ref, u_ref, v_ref, h2_ref, g_ref, o_ref,
                       thr_ref, c1_ref, e2_ref, acc_ref, *, te, tm):
    j = pl.program_id(1)
    nk = PEER_KEYS
    n_al = te // nk
    nt = tm // LANES

    @pl.when(j == 0)
    def _():
        acc_ref[...] = jnp.zeros(acc_ref.shape, F32)
        for h in range(PEER_HEADS):
            for tl in range(nt):
                ls = slice(tl * LANES, (tl + 1) * LANES)
                s1 = st_ref[2 * h, tl]
                thr_ref[h, :, tl, :] = stats_ref[0, h:h + 1, ls] - s1
                c1_ref[h, :, tl, :] = jnp.exp(s1 - stats_ref[1, h:h + 1, ls]) * (0.5 * stats_ref[3, h:h + 1, ls])
                e2_ref[h, tl] = jnp.exp(st_ref[2 * h + 1, tl] - stats_ref[2, h:h + 1, ls])

    ce = te // EXPERT_CHUNKS
    acts = [_dot(u_ref[c * ce:(c + 1) * ce, :], a3t_ref[...]) for c in range(EXPERT_CHUNKS)]
    for c in range(EXPERT_CHUNKS):
        p_rows = []
        for al in range(ce // nk):
            a = j * n_al + c * (ce // nk) + al
            rs = slice(al * nk, (al + 1) * nk)
            tiles = []
            for tl in range(nt):
                w = None
                for h in range(PEER_HEADS):
                    thr = thr_ref[h, a, tl:tl + 1, :]
                    c1 = c1_ref[h, a, tl:tl + 1, :]
                    term = jnp.where(st_ref[2 * h + 1, tl] >= thr, e2_ref[h, tl] * c1, 0.0)
                    w = term if w is None else w + term
                tiles.append((_gelu_x2(acts[c][rs, tl * LANES:(tl + 1) * LANES]) * w).astype(BF16))
            p_rows.append(jnp.concatenate(tiles, axis=1))
        p = jnp.concatenate(p_rows, axis=0)
        acc_ref[...] += lax.dot_general(v_ref[c * ce:(c + 1) * ce, :], p, (((0,), (0,)), ((), ())),
                                        preferred_element_type=F32)

    @pl.when(j == pl.num_programs(1) - 1)
    def _():
        o_ref[...] = _rms(h2_ref[...] + acc_ref[...].T, g_ref[...])


def _peer_dense(a3t, st, stats, u_bf, v_bf, h2, g, tm, te):
    D, T = a3t.shape
    nt = tm // LANES
    return pl.pallas_call(
        functools.partial(_peer_dense_kernel, te=te, tm=tm),
        grid=(T // tm, u_bf.shape[0] // te),
        in_specs=[pl.BlockSpec((D, tm), lambda i, j: (0, i)),
                  pl.BlockSpec((st.shape[0], nt) + st.shape[2:], lambda i, j: (0, i, 0, 0)),
                  pl.BlockSpec(stats.shape[:2] + (tm,), lambda i, j: (0, 0, i)),
                  pl.BlockSpec((te, D), lambda i, j: (j, 0)),
                  pl.BlockSpec((te, D), lambda i, j: (j, 0)),
                  pl.BlockSpec((tm, D), lambda i, j: (i, 0)), _const_spec(g.shape)],
        out_specs=pl.BlockSpec((tm, D), lambda i, j: (i, 0)),
        out_shape=jax.ShapeDtypeStruct((T, D), F32),
        scratch_shapes=[pltpu.VMEM((PEER_HEADS, PEER_KEYS, nt, LANES), F32)] * 2
        + [pltpu.VMEM((PEER_HEADS, nt, PEER_KEYS, LANES), F32), pltpu.VMEM((D, tm), F32)],
        compiler_params=_params(("parallel", "arbitrary")),
        name="peer_dense",
    )(a3t, st, stats, u_bf, v_bf, h2, g)


def _rope_tables(seq):
    pos = np.arange(seq, dtype=np.float64)

    def tab(dim):
        inv = 1.0 / (ROPE_BASE ** (np.arange(0, dim, 2, dtype=np.float64) / dim))
        ang = pos[:, None] * inv[None, :]
        return np.cos(ang), np.sin(ang)

    cm, sm = tab(MLA_ROPE)
    pad = LANES - MLA_NOPE - MLA_ROPE
    one, zero = np.ones((seq, MLA_NOPE)), np.zeros((seq, MLA_NOPE))
    cosm = np.concatenate([one, cm, cm, np.zeros((seq, pad))], axis=1)
    sinm = np.concatenate([zero, -sm, sm, np.zeros((seq, pad))], axis=1)
    cr, sr = tab(RET_DK)
    cosr = np.concatenate([cr, cr], axis=1)
    sinr = np.concatenate([-sr, sr], axis=1)
    return tuple(jnp.asarray(t, F32) for t in (cosm, sinm, cosr, sinr))


def _retention_tables():
    C = RET_CHUNK
    log_g = np.log(1.0 - 2.0 ** (-5.0 - np.arange(RET_HEADS, dtype=np.float64)))
    idx = np.arange(C, dtype=np.float64)
    rel = idx[:, None] - idx[None, :]
    decay = np.where(rel[None] >= 0, np.exp(np.maximum(rel, 0.0)[None] * log_g[:, None, None]), 0.0)
    zeta = np.exp((C - 1 - idx)[None, :] * log_g[:, None])
    xi = np.exp((idx + 1)[None, :] * log_g[:, None])
    cd = np.exp(C * log_g)
    bc = lambda t: np.broadcast_to(t[:, :, None], (RET_HEADS, C, LANES))
    tabs = (decay, bc(zeta), bc(xi), np.broadcast_to(cd[:, None, None], (RET_HEADS, C, LANES)))
    return tuple(jnp.asarray(t, F32) for t in tabs)


def _layer(h, mem, g_mix, w_in, g_q_lora, w_uq, g_kv_lora, w_ukv, g_ret_gn, w_o, g_xattn, g_mem,
           w_xq, w_xkv, w_xo, g_ffn, w_pq, sub_keys, u_experts, v_experts, g_out):
    B, S, D = h.shape
    T = B * S
    row = lambda g: g.reshape(1, -1)

    o_pe = MLA_Q_RANK + MLA_KV_RANK
    pe_pad = jnp.zeros((D, LANES), F32).at[:, MLA_NOPE:MLA_NOPE + MLA_ROPE].set(w_in[:, o_pe:o_pe + MLA_ROPE])
    w_in_ext = jnp.concatenate([w_in[:, :o_pe], pe_pad, w_in[:, o_pe + MLA_ROPE:]], axis=1).astype(BF16)
    dq = MLA_NOPE + MLA_ROPE
    w_uq_pad = jnp.pad(w_uq.reshape(MLA_Q_RANK, MLA_HEADS, dq), ((0, 0), (0, 0), (0, LANES - dq)))
    w_uq_pad = w_uq_pad.reshape(MLA_Q_RANK, MLA_HEADS * LANES).astype(BF16)
    w_ukv3 = w_ukv.reshape(MLA_KV_RANK, MLA_HEADS, MLA_NOPE + MLA_V)
    w_k = jnp.pad(w_ukv3[:, :, :MLA_NOPE], ((0, 0), (0, 0), (0, LANES - MLA_NOPE)))
    w_k = w_k.reshape(MLA_KV_RANK, MLA_HEADS * LANES).astype(BF16)
    w_v = jnp.pad(w_ukv3[:, :, MLA_NOPE:], ((0, 0), (0, 0), (0, LANES - MLA_V)))
    w_v = w_v.reshape(MLA_KV_RANK, MLA_HEADS * LANES).astype(BF16)

    cosm, sinm, cosr, sinr = _rope_tables(S)
    q, k, v, rq, rk, rv, rg = _in_proj(h.reshape(T, D), row(g_mix), w_in_ext, row(g_q_lora), w_uq_pad,
                                       row(g_kv_lora), w_k, w_v, cosm, sinm, cosr, sinr, S, min(1024, S))
    r3 = lambda t: t.reshape(B, S, t.shape[-1])
    o_mla = _mla_attn(r3(q), r3(k), r3(v), min(1024, S))
    o_ret = _retention(r3(rq), r3(rk), r3(rv), r3(rg), row(g_ret_gn), *_retention_tables(), min(1024, S))

    n_mla = MLA_HEADS * MLA_V
    w_o_bf = w_o.astype(BF16)
    h2, a3t, st = _post_mix(h.reshape(T, D), o_mla.reshape(T, -1), o_ret.reshape(T, -1), w_o_bf[:n_mla],
                            w_o_bf[n_mla:], row(g_xattn), w_xq.astype(BF16), mem, row(g_mem), w_xkv.astype(BF16),
                            w_xo.astype(BF16),
                            row(g_ffn), w_pq.astype(BF16), sub_keys.astype(BF16), S, min(512, S))
    stats = _peer_route(st)
    out = _peer_dense(a3t, st, stats, u_experts.astype(BF16), v_experts.astype(BF16), h2, row(g_out),
                      min(512, T), 2048)
    return out.reshape(B, S, D)


def kernel(x, mem, g_mix, w_in, g_q_lora, w_uq, g_kv_lora, w_ukv, g_ret_gn, w_o, g_xattn, g_mem, w_xq, w_xkv,
           w_xo, g_ffn, w_pq, sub_keys, u_experts, v_experts, g_final):
    depth = g_mix.shape[0]
    assert depth == 1, "the final norm is fused into the single layer's last kernel"
    l = 0
    return _layer(x, mem, g_mix[l], w_in[l], g_q_lora[l], w_uq[l], g_kv_lora[l], w_ukv[l], g_ret_gn[l], w_o[l],
                  g_xattn[l], g_mem[l], w_xq[l], w_xkv[l], w_xo[l], g_ffn[l], w_pq[l], sub_keys[l],
                  u_experts[l], v_experts[l], g_final)
```
